```python
import jax, jax.numpy as jnp
from jax import lax
import numpy as np

D_MODEL = 2048
BATCH = 4
SEQ = 8192
DEPTH = 2

HEAD_DIM = 64
H_SB = 8
DIL_CONFIGS = ((128, 1), (512, 4), (2048, 16))
N_DIL_GROUPS = 3
H_DIL_PER_GROUP = 4
H_DIL = N_DIL_GROUPS * H_DIL_PER_GROUP
H_FOX = 12
W_SB = H_SB * HEAD_DIM
W_DIL = H_DIL * HEAD_DIM
W_DIL_OUT = H_DIL_PER_GROUP * HEAD_DIM
W_FOX = H_FOX * HEAD_DIM
N_BRANCHES = 3
ROPE_THETA = 500000.0
ROPE_DIMS = HEAD_DIM // 4
Q_BLOCK = 128
N_EXPERTS = 32
TOP_K = 4
D_EXPERT = 1536
SWIGLU_ALPHA = 1.702
SWIGLU_LIMIT = 7.0
EXPERT_ROW_BLOCK = 128
LN_EPS = 1e-5
DEEPNORM_ALPHA = (2.0 * DEPTH) ** 0.25
DEEPNORM_BETA = (8.0 * DEPTH) ** -0.25
IN_WIDTHS = (W_SB, W_SB, W_SB, W_DIL, W_DIL, W_DIL, W_FOX, W_FOX, W_FOX, H_FOX)
N_IN = 3 * W_SB + 3 * W_DIL + 3 * W_FOX + H_FOX

kernel_name = "hybrid_sb_dilated_fox_moe_deepnorm_adaln"


def layer_norm(x):
    xf = x.astype(jnp.float32)
    mu = jnp.mean(xf, axis=-1, keepdims=True)
    var = jnp.mean(jnp.square(xf - mu), axis=-1, keepdims=True)
    return ((xf - mu) * lax.rsqrt(var + LN_EPS)).astype(x.dtype)


def layer_norm_affine(x, g, b):
    return layer_norm(x) * g + b


def rope_tables(seq, dtype):
    pos = jnp.arange(seq, dtype=jnp.float32)
    inv = ROPE_THETA ** (-jnp.arange(0, ROPE_DIMS, 2, dtype=jnp.float32) / ROPE_DIMS)
    ang = pos[:, None] * inv[None, :]
    return jnp.cos(ang).astype(dtype), jnp.sin(ang).astype(dtype)


def apply_partial_rope(t, cos, sin):
    rot, rest = t[..., :ROPE_DIMS], t[..., ROPE_DIMS:]
    half = ROPE_DIMS // 2
    x1, x2 = rot[..., :half], rot[..., half:]
    c = cos[None, :, None, :]
    s = sin[None, :, None, :]
    return jnp.concatenate([x1 * c - x2 * s, x2 * c + x1 * s, rest], axis=-1)


def to_query_blocks(t):
    b, s = t.shape[:2]
    t = t.reshape((b, s // Q_BLOCK, Q_BLOCK) + t.shape[2:])
    return jnp.moveaxis(t, 1, 0)


def from_query_blocks(t):
    t = jnp.moveaxis(t, 0, 1)
    return t.reshape((t.shape[0], t.shape[1] * t.shape[2]) + t.shape[3:])


def stick_breaking_attention(q, k, v):
    seq = q.shape[1]
    scale = HEAD_DIM ** -0.5
    key_pos = jnp.arange(seq)

    def block(args):
        qi, i = args
        q_pos = i * Q_BLOCK + jnp.arange(Q_BLOCK)
        z = jnp.einsum('bqhd,bkhd->bhqk', qi, k, preferred_element_type=jnp.float32) * scale
        causal = key_pos[None, :] < q_pos[:, None]
        log_stop = jnp.where(causal, jax.nn.log_sigmoid(z), -jnp.inf)
        log_cont = jnp.where(causal, jax.nn.log_sigmoid(-z), 0.0)
        between = lax.cumsum(log_cont, axis=3, reverse=True) - log_cont
        w = jnp.exp(log_stop + between)
        return jnp.einsum('bhqk,bkhd->bqhd', w.astype(v.dtype), v)

    out = lax.map(block, (to_query_blocks(q), jnp.arange(seq // Q_BLOCK)))
    return from_query_blocks(out)


def dilated_window_attention(q, k, v):
    seq = q.shape[1]
    scale = HEAD_DIM ** -0.5
    k_groups = [k[:, :, g] for g in range(N_DIL_GROUPS)]
    v_groups = [v[:, :, g] for g in range(N_DIL_GROUPS)]

    def block(args):
        qi, i = args
        q_pos = i * Q_BLOCK + jnp.arange(Q_BLOCK)
        outs, lses = [], []
        for g, (window, dilation) in enumerate(DIL_CONFIGS):
            offsets = dilation * jnp.arange(window // dilation + 1)
            idx = q_pos[:, None] - offsets[None, :]
            valid = idx >= 0
            idx = jnp.maximum(idx, 0)
            kg = jnp.take(k_groups[g], idx, axis=1)
            vg = jnp.take(v_groups[g], idx, axis=1)
            logits = jnp.einsum('bqhd,bqkhd->bhqk', qi[:, :, g], kg,
                                preferred_element_type=jnp.float32) * scale
            logits = jnp.where(valid[None, None], logits, -jnp.inf)
            m = jnp.max(logits, axis=-1, keepdims=True)
            e = jnp.exp(logits - m)
            den = jnp.sum(e, axis=-1, keepdims=True)
            outs.append(jnp.einsum('bhqk,bqkhd->bqhd', (e / den).astype(v.dtype), vg))
            lses.append((m + jnp.log(den))[..., 0])
        mix = jax.nn.softmax(jnp.stack(lses), axis=0)
        mix = jnp.transpose(mix, (0, 1, 3, 2))[..., None].astype(v.dtype)
        return jnp.sum(jnp.stack(outs) * mix, axis=0)

    out = lax.map(block, (to_query_blocks(q), jnp.arange(seq // Q_BLOCK)))
    return from_query_blocks(out)


def forgetting_attention(q, k, v, log_f):
    seq = q.shape[1]
    scale = HEAD_DIM ** -0.5
    cum = jnp.transpose(jnp.cumsum(log_f, axis=1), (0, 2, 1))
    key_pos = jnp.arange(seq)

    def block(args):
        qi, i = args
        q_pos = i * Q_BLOCK + jnp.arange(Q_BLOCK)
        cum_q = lax.dynamic_slice_in_dim(cum, i * Q_BLOCK, Q_BLOCK, axis=2)
        logits = jnp.einsum('bqhd,bkhd->bhqk', qi, k, preferred_element_type=jnp.float32) * scale
        logits = logits + cum_q[..., :, None] - cum[:, :, None, :]
        logits = jnp.where(key_pos[None, :] <= q_pos[:, None], logits, -jnp.inf)
        p = jax.nn.softmax(logits, axis=-1)
        return jnp.einsum('bhqk,bkhd->bqhd', p.astype(v.dtype), v)

    out = lax.map(block, (to_query_blocks(q), jnp.arange(seq // Q_BLOCK)))
    return from_query_blocks(out)


def mixer_sublayer(h, cos, sin, w_in, b_forget, w_gate, b_gate,
                   w_proj_sb, w_proj_dil, w_proj_fox, w_out):
    b, s, d = h.shape
    proj = h @ w_in
    splits = np.cumsum(IN_WIDTHS)[:-1].tolist()
    q_sb, k_sb, v_sb, q_dl, k_dl, v_dl, q_fx, k_fx, v_fx, f_logit = jnp.split(proj, splits, axis=-1)

    heads = lambda t, n: t.reshape(b, s, n, HEAD_DIM)
    o_sb = stick_breaking_attention(heads(q_sb, H_SB), heads(k_sb, H_SB), heads(v_sb, H_SB))

    grp = lambda t: t.reshape(b, s, N_DIL_GROUPS, H_DIL_PER_GROUP, HEAD_DIM)
    q_dl = grp(apply_partial_rope(heads(q_dl, H_DIL), cos, sin))
    k_dl = grp(apply_partial_rope(heads(k_dl, H_DIL), cos, sin))
    o_dl = dilated_window_attention(q_dl, k_dl, grp(v_dl))

    log_f = jax.nn.log_sigmoid((f_logit + b_forget).astype(jnp.float32))
    o_fx = forgetting_attention(heads(q_fx, H_FOX), heads(k_fx, H_FOX), heads(v_fx, H_FOX), log_f)

    gates = jax.nn.sigmoid(h @ w_gate + b_gate).reshape(b, s, N_BRANCHES, d)
    merged = (gates[:, :, 0] * (o_sb.reshape(b, s, W_SB) @ w_proj_sb)
              + gates[:, :, 1] * (o_dl.reshape(b, s, W_DIL_OUT) @ w_proj_dil)
              + gates[:, :, 2] * (o_fx.reshape(b, s, W_FOX) @ w_proj_fox))
    return merged @ w_out


def moe_ffn(h, w_router, b_router, w_up, b_up, w_down, b_down):
    b, s, d = h.shape
    n_tok = b * s
    xt = h.reshape(n_tok, d)
    logits = jnp.einsum('td,de->te', xt, w_router, preferred_element_type=jnp.float32)
    logits = logits + b_router.astype(jnp.float32)
    top_logits, top_idx = lax.top_k(logits, TOP_K)
    top_w = jax.nn.softmax(top_logits, axis=-1)

    n_assign = n_tok * TOP_K
    flat_e = top_idx.reshape(n_assign)
    order = jnp.argsort(flat_e)
    sorted_e = flat_e[order]
    sorted_tok = (order // TOP_K).astype(jnp.int32)
    sorted_w = top_w.reshape(n_assign)[order]
    counts = jnp.bincount(flat_e, length=N_EXPERTS)
    padded = (counts + EXPERT_ROW_BLOCK - 1) // EXPERT_ROW_BLOCK * EXPERT_ROW_BLOCK
    start = jnp.cumsum(counts) - counts
    padded_end = jnp.cumsum(padded)
    padded_start = padded_end - padded
    dest = padded_start[sorted_e] + (jnp.arange(n_assign) - start[sorted_e])
    n_rows = n_assign + N_EXPERTS * EXPERT_ROW_BLOCK
    n_blocks = n_rows // EXPERT_ROW_BLOCK
    row_tok = jnp.zeros((n_rows,), jnp.int32).at[dest].set(sorted_tok)
    row_w = jnp.zeros((n_rows,), jnp.float32).at[dest].set(sorted_w)
    block_e = jnp.minimum(
        jnp.searchsorted(padded_end, jnp.arange(n_blocks) * EXPERT_ROW_BLOCK, side='right'),
        N_EXPERTS - 1)

    def expert_block(args):
        tok, wgt, e = args
        xb = xt[tok]
        up = xb @ w_up[e] + b_up[e]
        glu = jnp.minimum(up[:, 0::2], SWIGLU_LIMIT)
        lin = jnp.clip(up[:, 1::2], -SWIGLU_LIMIT, SWIGLU_LIMIT)
        act = glu * jax.nn.sigmoid(SWIGLU_ALPHA * glu) * (lin + 1)
        y = act @ w_down[e] + b_down[e]
        return y * wgt[:, None].astype(y.dtype)

    ys = lax.map(expert_block, (row_tok.reshape(n_blocks, EXPERT_ROW_BLOCK),
                                row_w.reshape(n_blocks, EXPERT_ROW_BLOCK), block_e))
    out = jax.ops.segment_sum(ys.reshape(n_rows, d), row_tok, num_segments=n_tok)
    return out.reshape(b, s, d)


def setup_inputs(seed: int = 0) -> dict:
    key = jax.random.key(seed)
    ks = jax.random.split(key, 24)
    L, D, E, F = DEPTH, D_MODEL, N_EXPERTS, D_EXPERT

    def nrm(k, shape, scale):
        return jax.random.normal(k, shape, jnp.float32) * scale

    col_scale = jnp.concatenate([
        jnp.full((wd,), DEEPNORM_BETA if j in (2, 5, 8) else 1.0, jnp.float32)
        for j, wd in enumerate(IN_WIDTHS)])
    return {
        'x': nrm(ks[0], (BATCH, SEQ, D), 1.0),
        'c': nrm(ks[1], (BATCH, D), 1.0),
        'w_ada': nrm(ks[2], (L, D, 6 * D), D ** -0.5),
        'b_ada': nrm(ks[3], (L, 6 * D), 0.01),
        'w_in': nrm(ks[4], (L, D, N_IN), D ** -0.5) * col_scale,
        'b_forget': jax.random.uniform(ks[5], (L, H_FOX), jnp.float32, 1.0, 5.0),
        'w_gate': nrm(ks[6], (L, D, N_BRANCHES * D), D ** -0.5),
        'b_gate': nrm(ks[7], (L, N_BRANCHES * D), 0.01),
        'w_proj_sb': nrm(ks[8], (L, W_SB, D), W_SB ** -0.5),
        'w_proj_dil': nrm(ks[9], (L, W_DIL_OUT, D), W_DIL_OUT ** -0.5),
        'w_proj_fox': nrm(ks[10], (L, W_FOX, D), W_FOX ** -0.5),
        'w_out': nrm(ks[11], (L, D, D), D ** -0.5 * DEEPNORM_BETA),
        'ln1_g': 1.0 + nrm(ks[12], (L, D), 0.01),
        'ln1_b': nrm(ks[13], (L, D), 0.01),
        'w_router': nrm(ks[14], (L, D, E), D ** -0.5),
        'b_router': nrm(ks[15], (L, E), 0.01),
        'w_up': nrm(ks[16], (L, E, D, 2 * F), D ** -0.5),
        'b_up': nrm(ks[17], (L, E, 2 * F), 0.01),
        'w_down': nrm(ks[18], (L, E, F, D), F ** -0.5 * DEEPNORM_BETA),
        'b_down': nrm(ks[19], (L, E, D), 0.01),
        'ln2_g': 1.0 + nrm(ks[20], (L, D), 0.01),
        'ln2_b': nrm(ks[21], (L, D), 0.01),
    }


def reference(x, c, w_ada, b_ada, w_in, b_forget, w_gate, b_gate, w_proj_sb, w_proj_dil,
              w_proj_fox, w_out, ln1_g, ln1_b, w_router, b_router, w_up, b_up, w_down,
              b_down, ln2_g, ln2_b):
    cos, sin = rope_tables(x.shape[1], x.dtype)
    c_act = jax.nn.silu(c)
    for l in range(DEPTH):
        mod = c_act @ w_ada[l] + b_ada[l]
        shift_m, scale_m, gate_m, shift_f, scale_f, gate_f = [
            m[:, None, :] for m in jnp.split(mod, 6, axis=-1)]
        h = layer_norm(x) * (1 + scale_m) + shift_m
        y = mixer_sublayer(h, cos, sin, w_in[l], b_forget[l], w_gate[l], b_gate[l],
                           w_proj_sb[l], w_proj_dil[l], w_proj_fox[l], w_out[l])
        x = layer_norm_affine(DEEPNORM_ALPHA * x + gate_m * y, ln1_g[l], ln1_b[l])
        h = layer_norm(x) * (1 + scale_f) + shift_f
        y = moe_ffn(h, w_router[l], b_router[l], w_up[l], b_up[l], w_down[l], b_down[l])
        x = layer_norm_affine(DEEPNORM_ALPHA * x + gate_f * y, ln2_g[l], ln2_b[l])
    return x
```

```python
import functools

import numpy as np
import jax
import jax.numpy as jnp
from jax import lax
from jax.experimental import pallas as pl
from jax.experimental.pallas import tpu as pltpu

F32 = jnp.float32
BF16 = jnp.bfloat16

HEAD_DIM = 64
H_SB = 8
DIL_CONFIGS = ((128, 1), (512, 4), (2048, 16))
H_DIL_PER_GROUP = 4
H_FOX = 12
ROPE_THETA = 500000.0
ROPE_DIMS = HEAD_DIM // 4
TOP_K = 4
SWIGLU_ALPHA = 1.702
SWIGLU_LIMIT = 7.0
LN_EPS = 1e-5
QK_SCALE = HEAD_DIM ** -0.5

LANES = 128
SUBLANES = 8
VMEM_BUDGET_MB = 56

N_DIL_GROUPS = len(DIL_CONFIGS)
W_SB = H_SB * HEAD_DIM
W_DIL = N_DIL_GROUPS * H_DIL_PER_GROUP * HEAD_DIM
W_DIL_OUT = H_DIL_PER_GROUP * HEAD_DIM
W_FOX = H_FOX * HEAD_DIM
N_QKV = 3 * (W_SB + W_DIL + W_FOX)
COL_SB = (0, W_SB // LANES, 2 * W_SB // LANES)
_DL0 = 3 * W_SB // LANES
COL_DL = (_DL0, _DL0 + W_DIL // LANES, _DL0 + 2 * W_DIL // LANES)
_FX0 = _DL0 + 3 * W_DIL // LANES
COL_FX = (_FX0, _FX0 + W_FOX // LANES, _FX0 + 2 * W_FOX // LANES)
BAND = 128
NEG_BIG = -1e30


def _cparams(semantics, vmem_mb=VMEM_BUDGET_MB):
    return pltpu.CompilerParams(dimension_semantics=semantics, vmem_limit_bytes=vmem_mb * 2 ** 20)


def _dot(a, b):
    return jnp.dot(a, b, preferred_element_type=F32)


def _dot_t(a, b):
    return lax.dot_general(a, b, (((1,), (1,)), ((), ())), preferred_element_type=F32)


def _ln(x):
    mu = jnp.mean(x, axis=-1, keepdims=True)
    xc = x - mu
    var = jnp.mean(xc * xc, axis=-1, keepdims=True)
    return xc * lax.rsqrt(var + LN_EPS)


def _softplus_neg_abs(z):
    return jnp.log1p(jnp.exp(-jnp.abs(z)))


def _split3(x):
    hi = x.astype(BF16)
    r1 = x - hi.astype(F32)
    mid = r1.astype(BF16)
    lo = (r1 - mid.astype(F32)).astype(BF16)
    return hi, mid, lo


def _mod_kernel(c_ref, w_ref, b_ref, o_ref):
    c = c_ref[...]
    act = (c * jax.nn.sigmoid(c)).astype(BF16)
    o_ref[...] = _dot(act, w_ref[...].astype(BF16)) + b_ref[...]


def adaln_mod(c, w_ada, b_ada):
    n_layers, d, n = w_ada.shape
    b = c.shape[0]
    assert b <= SUBLANES
    tn = 512
    c_pad = jnp.zeros((SUBLANES, d), F32).at[:b].set(c)
    out = pl.pallas_call(
        _mod_kernel,
        grid=(n_layers, n // tn),
        in_specs=[pl.BlockSpec((SUBLANES, d), lambda l, j: (0, 0)),
                  pl.BlockSpec((None, d, tn), lambda l, j: (l, 0, j)),
                  pl.BlockSpec((None, 1, tn), lambda l, j: (l, 0, j))],
        out_specs=pl.BlockSpec((None, SUBLANES, tn), lambda l, j: (l, 0, j)),
        out_shape=jax.ShapeDtypeStruct((n_layers, SUBLANES, n), F32),
        compiler_params=_cparams(("arbitrary", "arbitrary"), 32),
        name="adaln_mod",
    )(c_pad, w_ada, b_ada.reshape(n_layers, 1, n))
    return out[:, :b].reshape(n_layers, b, 6, d).transpose(0, 2, 1, 3)[:, :, :, None, :]


def _mod_spec(comp, d, rows_per_batch_blocks):
    return pl.BlockSpec((None, None, 1, d), lambda i, *_: (comp, i // rows_per_batch_blocks, 0, 0))


def _ln_mod_kernel(x_ref, sc_ref, sh_ref, h_ref):
    h_ref[...] = (_ln(x_ref[...]) * (1.0 + sc_ref[...]) + sh_ref[...]).astype(h_ref.dtype)


def ln_mod(x2d, mod_l, seq, comp_shift, comp_scale):
    t, d = x2d.shape
    ts = min(512, seq)
    bpb = seq // ts
    return pl.pallas_call(
        _ln_mod_kernel,
        grid=(t // ts,),
        in_specs=[pl.BlockSpec((ts, d), lambda i: (i, 0)),
                  _mod_spec(comp_scale, d, bpb), _mod_spec(comp_shift, d, bpb)],
        out_specs=pl.BlockSpec((ts, d), lambda i: (i, 0)),
        out_shape=jax.ShapeDtypeStruct((t, d), BF16),
        compiler_params=_cparams(("arbitrary",), 32),
        name="ln_mod",
    )(x2d, mod_l, mod_l)


def _inproj_kernel(h_ref, w_ref, cos_ref, s1_ref, s2_ref, o_ref, *, rope_lo, rope_hi, tn):
    j = pl.program_id(1)
    acc = _dot(h_ref[...], w_ref[...])
    is_rope = jnp.logical_and(j >= rope_lo, j < rope_hi)

    @pl.when(is_rope)
    def _():
        c, s1, s2 = cos_ref[...], s1_ref[...], s2_ref[...]
        for blk in range(tn // LANES):
            a = acc[:, blk * LANES:(blk + 1) * LANES]
            r = a * c + pltpu.roll(a, LANES - ROPE_DIMS // 2, 1) * s1 + pltpu.roll(a, ROPE_DIMS // 2, 1) * s2
            o_ref[:, blk * LANES:(blk + 1) * LANES] = r.astype(o_ref.dtype)

    @pl.when(jnp.logical_not(is_rope))
    def _():
        o_ref[...] = acc.astype(o_ref.dtype)


def rope_pair_tables(seq):
    pos = jnp.arange(seq, dtype=F32)
    inv = ROPE_THETA ** (-jnp.arange(0, ROPE_DIMS, 2, dtype=F32) / ROPE_DIMS)
    ang = pos[:, None] * inv[None, :]
    cos, sin = jnp.cos(ang), jnp.sin(ang)
    half = ROPE_DIMS // 2
    head_c = jnp.concatenate([cos, cos, jnp.ones((seq, HEAD_DIM - ROPE_DIMS), F32)], axis=1)
    head_s1 = jnp.concatenate([-sin, jnp.zeros((seq, HEAD_DIM - half), F32)], axis=1)
    head_s2 = jnp.concatenate([jnp.zeros((seq, half), F32), sin,
                               jnp.zeros((seq, HEAD_DIM - ROPE_DIMS), F32)], axis=1)
    two = lambda a: jnp.concatenate([a, a], axis=1)
    return two(head_c), two(head_s1), two(head_s2)


def in_projection(h, w_qkv, rope_tabs, seq):
    t, d = h.shape
    n = w_qkv.shape[1]
    tm = min(1024, seq)
    tn = 512
    assert seq % tm == 0 and n % tn == 0
    rope_lo, rope_hi = COL_DL[0] * LANES, COL_DL[2] * LANES
    assert rope_lo % tn == 0 and rope_hi % tn == 0
    spb = seq // tm
    tab_spec = pl.BlockSpec((tm, LANES), lambda i, j: (i % spb, 0))
    return pl.pallas_call(
        functools.partial(_inproj_kernel, rope_lo=rope_lo // tn, rope_hi=rope_hi // tn, tn=tn),
        grid=(t // tm, n // tn),
        in_specs=[pl.BlockSpec((tm, d), lambda i, j: (i, 0)),
                  pl.BlockSpec((d, tn), lambda i, j: (0, j)),
                  tab_spec, tab_spec, tab_spec],
        out_specs=pl.BlockSpec((tm, tn), lambda i, j: (i, j)),
        out_shape=jax.ShapeDtypeStruct((t, n), BF16),
        compiler_params=_cparams(("arbitrary", "arbitrary"), 48),
        name="in_projection",
    )(h, w_qkv, *rope_tabs)


def _forget_kernel(h_ref, w_ref, b_ref, tri_ref, o_ref, carry_ref):
    @pl.when(pl.program_id(1) == 0)
    def _():
        carry_ref[...] = jnp.zeros_like(carry_ref)

    f = _dot(h_ref[...], w_ref[...]) + b_ref[...]
    log_f = jnp.minimum(f, 0.0) - _softplus_neg_abs(f)
    tri = tri_ref[...]
    hi, mid, lo = _split3(log_f)
    cum = _dot(tri, hi) + _dot(tri, mid) + _dot(tri, lo) + carry_ref[0:1, :]
    o_ref[...] = cum
    carry_ref[...] = jnp.broadcast_to(cum[-1:, :], carry_ref.shape)


def forget_cumsum(h, w_f, b_f, batch, seq):
    t, d = h.shape
    ts = min(256, seq)
    nsb = seq // ts
    w_pad = jnp.zeros((d, LANES), BF16).at[:, :H_FOX].set(w_f.astype(BF16))
    b_pad = jnp.zeros((1, LANES), F32).at[0, :H_FOX].set(b_f)
    tri = (np.arange(ts)[:, None] >= np.arange(ts)[None, :]).astype(np.float32)
    return pl.pallas_call(
        _forget_kernel,
        grid=(batch, nsb),
        in_specs=[pl.BlockSpec((ts, d), lambda b, s: (b * nsb + s, 0)),
                  pl.BlockSpec((d, LANES), lambda b, s: (0, 0)),
                  pl.BlockSpec((1, LANES), lambda b, s: (0, 0)),
                  pl.BlockSpec((ts, ts), lambda b, s: (0, 0))],
        out_specs=pl.BlockSpec((ts, LANES), lambda b, s: (b * nsb + s, 0)),
        out_shape=jax.ShapeDtypeStruct((t, LANES), F32),
        scratch_shapes=[pltpu.VMEM((SUBLANES, LANES), F32)],
        compiler_params=_cparams(("arbitrary", "arbitrary"), 32),
        name="forget_cumsum",
    )(h, w_pad, b_pad, jnp.asarray(tri, BF16))


FLAG_FIRST, FLAG_LAST, FLAG_MASKED = 1, 2, 4


def _causal_schedule(seq, tq, tk, strict, reverse):
    qi, kb, fl = [], [], []
    for i in range(seq // tq):
        q_lo, q_hi = i * tq, (i + 1) * tq - 1
        k_max = q_hi - 1 if strict else q_hi
        blocks = list(range(max(k_max, 0) // tk + 1))
        if reverse:
            blocks = blocks[::-1]
        for n, kblk in enumerate(blocks):
            k_hi = kblk * tk + tk - 1
            masked = (k_hi >= q_lo) if strict else (k_hi > q_lo)
            qi.append(i)
            kb.append(kblk)
            fl.append((FLAG_FIRST if n == 0 else 0) | (FLAG_LAST if n == len(blocks) - 1 else 0)
                      | (FLAG_MASKED if masked else 0))
    as_i32 = lambda a: jnp.asarray(np.asarray(a, np.int32))
    return as_i32(qi), as_i32(kb), as_i32(fl)


def _pair_masks(tq):
    lane = lax.broadcasted_iota(jnp.int32, (tq, LANES), 1)
    return lane < HEAD_DIM


def _positions(qi, kb, tq, tk):
    qpos = qi * tq + lax.broadcasted_iota(jnp.int32, (tq, tk), 0)
    kpos = kb * tk + lax.broadcasted_iota(jnp.int32, (tq, tk), 1)
    return qpos, kpos


def _sb_kernel(qi_ref, kb_ref, fl_ref, q_ref, k_ref, v_ref, u_ref, o_ref,
               qa_s, qb_s, acc_a, acc_b, run_a, run_b, *, tq, tk):
    s = pl.program_id(2)
    flags = fl_ref[s]
    head_a = _pair_masks(tq)

    @pl.when((flags & FLAG_FIRST) != 0)
    def _():
        q = q_ref[...] * QK_SCALE
        qa_s[...] = jnp.where(head_a, q, 0).astype(BF16)
        qb_s[...] = jnp.where(head_a, 0, q).astype(BF16)
        for ref in (acc_a, acc_b, run_a, run_b):
            ref[...] = jnp.zeros_like(ref)

    def step(mask):
        k, v, upper = k_ref[...], v_ref[...], u_ref[...]
        for q_s, acc, run in ((qa_s, acc_a, run_a), (qb_s, acc_b, run_b)):
            z = _dot_t(q_s[...], k)
            soft = _softplus_neg_abs(z)
            log_stop = jnp.minimum(z, 0.0) - soft
            log_cont = jnp.minimum(-z, 0.0) - soft
            if mask is not None:
                log_cont = jnp.where(mask, log_cont, 0.0)
            hi = log_cont.astype(BF16)
            lo = (log_cont - hi.astype(F32)).astype(BF16)
            between = _dot(hi, upper) + _dot(lo, upper)
            running = run[...]
            w = jnp.exp(log_stop + between + running)
            if mask is not None:
                w = jnp.where(mask, w, 0.0)
            acc[...] += _dot(w.astype(BF16), v)
            run[...] = running + between[:, 0:1] + log_cont[:, 0:1]

    masked = (flags & FLAG_MASKED) != 0

    @pl.when(masked)
    def _():
        qpos, kpos = _positions(qi_ref[s], kb_ref[s], tq, tk)
        step(kpos < qpos)

    @pl.when(jnp.logical_not(masked))
    def _():
        step(None)

    @pl.when((flags & FLAG_LAST) != 0)
    def _():
        o_ref[...] = jnp.where(head_a, acc_a[...], acc_b[...]).astype(o_ref.dtype)


def stick_breaking_attention(qkv, batch, seq):
    tq, tk = min(512, seq), min(256, seq)
    sched = _causal_schedule(seq, tq, tk, strict=True, reverse=True)
    n_steps = int(sched[0].shape[0])
    n_pairs = W_SB // LANES
    qkv3 = qkv.reshape(batch, seq, qkv.shape[-1])
    upper = jnp.asarray((np.arange(tk)[:, None] > np.arange(tk)[None, :]).astype(np.float32), BF16)
    qc, kc, vc = COL_SB
    grid_spec = pltpu.PrefetchScalarGridSpec(
        num_scalar_prefetch=3,
        grid=(batch, n_pairs, n_steps),
        in_specs=[pl.BlockSpec((None, tq, LANES), lambda b, p, s, qi, kb, fl: (b, qi[s], qc + p)),
                  pl.BlockSpec((None, tk, LANES), lambda b, p, s, qi, kb, fl: (b, kb[s], kc + p)),
                  pl.BlockSpec((None, tk, LANES), lambda b, p, s, qi, kb, fl: (b, kb[s], vc + p)),
                  pl.BlockSpec((tk, tk), lambda b, p, s, qi, kb, fl: (0, 0))],
        out_specs=pl.BlockSpec((None, tq, LANES), lambda b, p, s, qi, kb, fl: (b, qi[s], p)),
        scratch_shapes=[pltpu.VMEM((tq, LANES), BF16), pltpu.VMEM((tq, LANES), BF16),
                        pltpu.VMEM((tq, LANES), F32), pltpu.VMEM((tq, LANES), F32),
                        pltpu.VMEM((tq, 1), F32), pltpu.VMEM((tq, 1), F32)])
    out = pl.pallas_call(
        functools.partial(_sb_kernel, tq=tq, tk=tk),
        grid_spec=grid_spec,
        out_shape=jax.ShapeDtypeStruct((batch, seq, W_SB), BF16),
        compiler_params=_cparams(("arbitrary", "arbitrary", "arbitrary"), 32),
        name="stick_breaking_attention",
    )(*sched, qkv3, qkv3, qkv3, upper)
    return out.reshape(batch * seq, W_SB)


def _fox_kernel(qi_ref, kb_ref, fl_ref, q_ref, k_ref, v_ref, cq_ref, ck_ref, o_ref,
                qa_s, qb_s, acc_a, acc_b, m_a, m_b, l_a, l_b, cq_a, cq_b, *, tq, tk):
    p = pl.program_id(1)
    s = pl.program_id(2)
    flags = fl_ref[s]
    head_a = _pair_masks(tq)

    @pl.when((flags & FLAG_FIRST) != 0)
    def _():
        q = q_ref[...] * QK_SCALE
        qa_s[...] = jnp.where(head_a, q, 0).astype(BF16)
        qb_s[...] = jnp.where(head_a, 0, q).astype(BF16)
        cq = cq_ref[...]
        lane = lax.broadcasted_iota(jnp.int32, (tq, LANES), 1)
        cq_a[...] = jnp.sum(jnp.where(lane == 2 * p, cq, 0.0), axis=1, keepdims=True)
        cq_b[...] = jnp.sum(jnp.where(lane == 2 * p + 1, cq, 0.0), axis=1, keepdims=True)
        for ref in (acc_a, acc_b, l_a, l_b):
            ref[...] = jnp.zeros_like(ref)
        m_a[...] = jnp.full_like(m_a, NEG_BIG)
        m_b[...] = jnp.full_like(m_b, NEG_BIG)

    def step(mask):
        k, v, ck = k_ref[...], v_ref[...], ck_ref[...]
        heads = ((qa_s, acc_a, m_a, l_a, cq_a), (qb_s, acc_b, m_b, l_b, cq_b))
        for idx, (q_s, acc, m_s, l_s, cq_s) in enumerate(heads):
            z = _dot_t(q_s[...], k) + cq_s[...] - ck[idx:idx + 1, :]
            if mask is not None:
                z = jnp.where(mask, z, NEG_BIG)
            m_old = m_s[...]
            m_new = jnp.maximum(m_old, jnp.max(z, axis=1, keepdims=True))
            alpha = jnp.exp(m_old - m_new)
            prob = jnp.exp(z - m_new)
            l_s[...] = alpha * l_s[...] + jnp.sum(prob, axis=1, keepdims=True)
            acc[...] = alpha * acc[...] + _dot(prob.astype(BF16), v)
            m_s[...] = m_new

    masked = (flags & FLAG_MASKED) != 0

    @pl.when(masked)
    def _():
        qpos, kpos = _positions(qi_ref[s], kb_ref[s], tq, tk)
        step(kpos <= qpos)

    @pl.when(jnp.logical_not(masked))
    def _():
        step(None)

    @pl.when((flags & FLAG_LAST) != 0)
    def _():
        out = jnp.where(head_a, acc_a[...] / l_a[...], acc_b[...] / l_b[...])
        o_ref[...] = out.astype(o_ref.dtype)


def forgetting_attention(qkv, cum, batch, seq):
    tq, tk = min(512, seq), min(256, seq)
    sched = _causal_schedule(seq, tq, tk, strict=False, reverse=False)
    n_steps = int(sched[0].shape[0])
    n_pairs = W_FOX // LANES
    qkv3 = qkv.reshape(batch, seq, qkv.shape[-1])
    cum3 = cum.reshape(batch, seq, LANES)
    cum_rows = cum3[:, :, :H_FOX].transpose(0, 2, 1).reshape(batch, n_pairs, 2, seq)
    qc, kc, vc = COL_FX
    col_scratch = lambda: pltpu.VMEM((tq, 1), F32)
    grid_spec = pltpu.PrefetchScalarGridSpec(
        num_scalar_prefetch=3,
        grid=(batch, n_pairs, n_steps),
        in_specs=[pl.BlockSpec((None, tq, LANES), lambda b, p, s, qi, kb, fl: (b, qi[s], qc + p)),
                  pl.BlockSpec((None, tk, LANES), lambda b, p, s, qi, kb, fl: (b, kb[s], kc + p)),
                  pl.BlockSpec((None, tk, LANES), lambda b, p, s, qi, kb, fl: (b, kb[s], vc + p)),
                  pl.BlockSpec((None, tq, LANES), lambda b, p, s, qi, kb, fl: (b, qi[s], 0)),
                  pl.BlockSpec((None, None, 2, tk), lambda b, p, s, qi, kb, fl: (b, p, 0, kb[s]))],
        out_specs=pl.BlockSpec((None, tq, LANES), lambda b, p, s, qi, kb, fl: (b, qi[s], p)),
        scratch_shapes=[pltpu.VMEM((tq, LANES), BF16), pltpu.VMEM((tq, LANES), BF16),
                        pltpu.VMEM((tq, LANES), F32), pltpu.VMEM((tq, LANES), F32),
                        col_scratch(), col_scratch(), col_scratch(), col_scratch(),
                        col_scratch(), col_scratch()])
    out = pl.pallas_call(
        functools.partial(_fox_kernel, tq=tq, tk=tk),
        grid_spec=grid_spec,
        out_shape=jax.ShapeDtypeStruct((batch, seq, W_FOX), BF16),
        compiler_params=_cparams(("arbitrary", "arbitrary", "arbitrary"), 32),
        name="forgetting_attention",
    )(*sched, qkv3, qkv3, qkv3, cum3, cum_rows)
    return out.reshape(batch * seq, W_FOX)


def _band_kernel(q_ref, kc_ref, vc_ref, kp_ref, vp_ref, o_ref, lse_ref, *, tq):
    i = pl.program_id(2)
    head_a = _pair_masks(tq)
    q = q_ref[...] * QK_SCALE
    kc, vc, kp, vp = kc_ref[...], vc_ref[...], kp_ref[...], vp_ref[...]
    diff = lax.broadcasted_iota(jnp.int32, (tq, tq), 0) - lax.broadcasted_iota(jnp.int32, (tq, tq), 1)
    mask_c = jnp.logical_and(diff >= 0, diff <= BAND)
    rp = lax.broadcasted_iota(jnp.int32, (tq, BAND), 0)
    cp = lax.broadcasted_iota(jnp.int32, (tq, BAND), 1)
    mask_p = jnp.logical_and(cp >= rp, i > 0)
    outs, lses = [], []
    for sel in (head_a, jnp.logical_not(head_a)):
        qh = jnp.where(sel, q, 0).astype(BF16)
        sc = jnp.where(mask_c, _dot_t(qh, kc), NEG_BIG)
        sp = jnp.where(mask_p, _dot_t(qh, kp), NEG_BIG)
        m = jnp.maximum(jnp.max(sc, axis=1, keepdims=True), jnp.max(sp, axis=1, keepdims=True))
        ec = jnp.exp(sc - m)
        ep = jnp.exp(sp - m)
        den = jnp.sum(ec, axis=1, keepdims=True) + jnp.sum(ep, axis=1, keepdims=True)
        outs.append((_dot(ec.astype(BF16), vc) + _dot(ep.astype(BF16), vp)) / den)
        lses.append(m + jnp.log(den))
    o_ref[...] = jnp.where(head_a, outs[0], outs[1])
    lse_ref[...] = jnp.where(head_a, lses[0], lses[1])


def _band_attention(q, k, v):
    nb, u, w = q.shape
    tq = min(256, u)
    assert u % tq == 0 and tq % BAND == 0
    sub = tq // BAND
    cur = pl.BlockSpec((None, tq, LANES), lambda n, p, i: (n, i, p))
    prev = pl.BlockSpec((None, BAND, LANES), lambda n, p, i: (n, jnp.maximum(i * sub - 1, 0), p))
    shp = jax.ShapeDtypeStruct((nb, u, w), F32)
    return pl.pallas_call(
        functools.partial(_band_kernel, tq=tq),
        grid=(nb, w // LANES, u // tq),
        in_specs=[cur, cur, cur, prev, prev],
        out_specs=[cur, cur],
        out_shape=[shp, shp],
        compiler_params=_cparams(("arbitrary", "arbitrary", "arbitrary"), 32),
        name="band_attention",
    )(q, k, v, k, v)


def _dil_mix_kernel(o0, o1, o2, l0, l1, l2, out_ref):
    a, b, c = l0[...], l1[...], l2[...]
    m = jnp.maximum(jnp.maximum(a, b), c)
    ea, eb, ec = jnp.exp(a - m), jnp.exp(b - m), jnp.exp(c - m)
    tot = ea + eb + ec
    out_ref[...] = ((o0[...] * ea + o1[...] * eb + o2[...] * ec) / tot).astype(out_ref.dtype)


def dilated_window_attention(qkv, batch, seq):
    t = batch * seq
    qkv3 = qkv.reshape(batch, seq, qkv.shape[-1])
    gw = W_DIL_OUT
    outs, lses = [], []
    for g, (window, dil) in enumerate(DIL_CONFIGS):
        assert window // dil == BAND and seq % (dil * BAND) == 0
        u = seq // dil

        def split(col):
            a = lax.slice_in_dim(qkv3, col * LANES + g * gw, col * LANES + (g + 1) * gw, axis=2)
            return a.reshape(batch, u, dil, gw).transpose(0, 2, 1, 3).reshape(batch * dil, u, gw)

        o, lse = _band_attention(split(COL_DL[0]), split(COL_DL[1]), split(COL_DL[2]))
        merge = lambda a: a.reshape(batch, dil, u, gw).transpose(0, 2, 1, 3).reshape(t, gw)
        outs.append(merge(o))
        lses.append(merge(lse))
    tm = min(1024, t)
    spec = pl.BlockSpec((tm, gw), lambda i: (i, 0))
    return pl.pallas_call(
        _dil_mix_kernel,
        grid=(t // tm,),
        in_specs=[spec] * 6,
        out_specs=spec,
        out_shape=jax.ShapeDtypeStruct((t, gw), BF16),
        compiler_params=_cparams(("arbitrary",), 32),
        name="dilated_mix",
    )(*outs, *lses)


def _merge_kernel(h_ref, wg0, wg1, wg2, bg0, bg1, bg2, o0, o1, o2, wp0, wp1, wp2, out_ref):
    h = h_ref[...]

    def branch(wg, bg, o, wp):
        return jax.nn.sigmoid(_dot(h, wg[...]) + bg[...]) * _dot(o[...], wp[...])

    merged = branch(wg0, bg0, o0, wp0) + branch(wg1, bg1, o1, wp1) + branch(wg2, bg2, o2, wp2)
    out_ref[...] = merged.astype(out_ref.dtype)


def gated_merge(h, w_gate, b_gate, branch_outs, branch_projs):
    t, d = h.shape
    tm, tn = min(512, t), min(512, d)
    nj = d // tn
    b_gate2 = b_gate.reshape(1, -1)
    gate_w = [pl.BlockSpec((d, tn), lambda j, i, br=br: (0, br * nj + j)) for br in range(3)]
    gate_b = [pl.BlockSpec((1, tn), lambda j, i, br=br: (0, br * nj + j)) for br in range(3)]
    o_specs = [pl.BlockSpec((tm, o.shape[1]), lambda j, i: (i, 0)) for o in branch_outs]
    p_specs = [pl.BlockSpec((w.shape[0], tn), lambda j, i: (0, j)) for w in branch_projs]
    return pl.pallas_call(
        _merge_kernel,
        grid=(nj, t // tm),
        in_specs=[pl.BlockSpec((tm, d), lambda j, i: (i, 0))] + gate_w + gate_b + o_specs + p_specs,
        out_specs=pl.BlockSpec((tm, tn), lambda j, i: (i, j)),
        out_shape=jax.ShapeDtypeStruct((t, d), BF16),
        compiler_params=_cparams(("arbitrary", "arbitrary"), 48),
        name="gated_merge",
    )(h, w_gate, w_gate, w_gate, b_gate2, b_gate2, b_gate2, *branch_outs, *branch_projs)


def _outproj_kernel(m_ref, w_ref, x_ref, gate_ref, g_ref, b_ref, sc_ref, sh_ref, wrh_ref, wrl_ref, br_ref,
                    x1_ref, h2_ref, lg_ref, *, alpha):
    y = _dot(m_ref[...], w_ref[...])
    x1 = _ln(alpha * x_ref[...] + gate_ref[...] * y) * g_ref[...] + b_ref[...]
    x1_ref[...] = x1
    h2 = _ln(x1) * (1.0 + sc_ref[...]) + sh_ref[...]
    h2_ref[...] = h2
    hi = h2.astype(BF16)
    lo = (h2 - hi.astype(F32)).astype(BF16)
    wrh = wrh_ref[...]
    lg_ref[...] = _dot(hi, wrh) + _dot(lo, wrh) + _dot(hi, wrl_ref[...]) + br_ref[...]


def out_projection(merged, w_out, x2d, mod_l, ln_g, ln_b, w_router, b_router, seq, alpha):
    t, d = x2d.shape
    tm = min(256, seq)
    bpb = seq // tm
    n_exp = w_router.shape[1]
    wr = jnp.zeros((d, LANES), F32).at[:, :n_exp].set(w_router)
    wr_hi = wr.astype(BF16)
    wr_lo = (wr - wr_hi.astype(F32)).astype(BF16)
    br = jnp.zeros((1, LANES), F32).at[0, :n_exp].set(b_router)
    row = pl.BlockSpec((tm, d), lambda i: (i, 0))
    vec = pl.BlockSpec((1, d), lambda i: (0, 0))
    wr_spec = pl.BlockSpec((d, LANES), lambda i: (0, 0))
    return pl.pallas_call(
        functools.partial(_outproj_kernel, alpha=alpha),
        grid=(t // tm,),
        in_specs=[row, pl.BlockSpec((d, d), lambda i: (0, 0)), row, _mod_spec(2, d, bpb), vec, vec,
                  _mod_spec(4, d, bpb), _mod_spec(3, d, bpb), wr_spec, wr_spec,
                  pl.BlockSpec((1, LANES), lambda i: (0, 0))],
        out_specs=[row, row, pl.BlockSpec((tm, LANES), lambda i: (i, 0))],
        out_shape=[jax.ShapeDtypeStruct((t, d), F32), jax.ShapeDtypeStruct((t, d), F32),
                   jax.ShapeDtypeStruct((t, LANES), F32)],
        compiler_params=_cparams(("arbitrary",), 48),
        name="out_projection",
    )(merged, w_out, x2d, mod_l, ln_g.reshape(1, d), ln_b.reshape(1, d), mod_l, mod_l, wr_hi, wr_lo, br)


def _route_kernel(lg_ref, tri_ref, ir_ref, w_ref, cnt_ref, carry_ref, *, n_exp):
    @pl.when(pl.program_id(0) == 0)
    def _():
        carry_ref[...] = jnp.zeros_like(carry_ref)

    tr = lg_ref.shape[0]
    lane = lax.broadcasted_iota(jnp.int32, (tr, LANES), 1)
    logits = jnp.where(lane < n_exp, lg_ref[...], -jnp.inf)
    vals, sels, idxs = [], [], []
    for _ in range(TOP_K):
        m = jnp.max(logits, axis=1, keepdims=True)
        idx = jnp.min(jnp.where(logits == m, lane, LANES), axis=1, keepdims=True)
        sel = lane == idx
        vals.append(m)
        idxs.append(idx)
        sels.append(sel)
        logits = jnp.where(sel, -jnp.inf, logits)
    exps = [jnp.exp(v - vals[0]) for v in vals]
    tot = exps[0] + exps[1] + exps[2] + exps[3]
    onehot = jnp.zeros((tr, LANES), F32)
    for sel in sels:
        onehot = onehot + jnp.where(sel, 1.0, 0.0)
    before = _dot(tri_ref[...], onehot.astype(BF16)) + carry_ref[0:1, :]
    ir = jnp.zeros((tr, LANES), jnp.int32)
    wt = jnp.zeros((tr, LANES), F32)
    for k in range(TOP_K):
        rank = jnp.sum(jnp.where(sels[k], before, 0.0), axis=1, keepdims=True).astype(jnp.int32)
        ir = jnp.where(lane == k, idxs[k], ir)
        ir = jnp.where(lane == TOP_K + k, rank, ir)
        wt = jnp.where(lane == k, exps[k] / tot, wt)
    ir_ref[...] = ir
    w_ref[...] = wt
    total = before[-1:, :] + onehot[-1:, :]
    carry_ref[...] = jnp.broadcast_to(total, carry_ref.shape)
    cnt_ref[...] = jnp.broadcast_to(total, cnt_ref.shape)


def route(logits, n_exp):
    t = logits.shape[0]
    tr = min(512, t)
    tri = jnp.asarray((np.arange(tr)[:, None] > np.arange(tr)[None, :]).astype(np.float32), BF16)
    row = pl.BlockSpec((tr, LANES), lambda i: (i, 0))
    return pl.pallas_call(
        functools.partial(_route_kernel, n_exp=n_exp),
        grid=(t // tr,),
        in_specs=[row, pl.BlockSpec((tr, tr), lambda i: (0, 0))],
        out_specs=[row, row, pl.BlockSpec((SUBLANES, LANES), lambda i: (0, 0))],
        out_shape=[jax.ShapeDtypeStruct((t, LANES), jnp.int32), jax.ShapeDtypeStruct((t, LANES), F32),
                   jax.ShapeDtypeStruct((SUBLANES, LANES), F32)],
        scratch_shapes=[pltpu.VMEM((SUBLANES, LANES), F32)],
        compiler_params=_cparams(("arbitrary",), 32),
        name="route",
    )(logits, tri)


def _gather_rows_kernel(tok_ref, src_ref, dst_ref, sem, *, rows):
    base = pl.program_id(0) * rows

    def row_copy(r, tok):
        return pltpu.make_async_copy(src_ref.at[pl.ds(tok, 1), :], dst_ref.at[pl.ds(base + r, 1), :], sem)

    def issue(r, carry):
        row_copy(r, tok_ref[0, r]).start()
        return carry

    def drain(r, carry):
        row_copy(r, 0).wait()
        return carry

    lax.fori_loop(0, rows, issue, 0)
    lax.fori_loop(0, rows, drain, 0)


def gather_rows(src, row_tok, rows_per_step):
    n_rows = row_tok.shape[0]
    d = src.shape[1]
    steps = n_rows // rows_per_step
    assert steps * rows_per_step == n_rows
    return pl.pallas_call(
        functools.partial(_gather_rows_kernel, rows=rows_per_step),
        grid=(steps,),
        in_specs=[pl.BlockSpec((None, 1, rows_per_step), lambda i: (i, 0, 0), memory_space=pltpu.SMEM),
                  pl.BlockSpec(memory_space=pl.ANY)],
        out_specs=pl.BlockSpec(memory_space=pl.ANY),
        out_shape=jax.ShapeDtypeStruct((n_rows, d), src.dtype),
        scratch_shapes=[pltpu.SemaphoreType.DMA(())],
        compiler_params=_cparams(("arbitrary",), 16),
        name="moe_dispatch_gather",
    )(row_tok.reshape(steps, 1, rows_per_step), src)


def _expert_kernel(be_ref, x_ref, wg_ref, wl_ref, bg_ref, bl_ref, wd_ref, bd_ref, y_ref):
    x = x_ref[...].astype(BF16)
    glu = jnp.minimum(_dot(x, wg_ref[...]) + bg_ref[...], SWIGLU_LIMIT)
    lin = jnp.clip(_dot(x, wl_ref[...]) + bl_ref[...], -SWIGLU_LIMIT, SWIGLU_LIMIT)
    act = glu * jax.nn.sigmoid(SWIGLU_ALPHA * glu) * (lin + 1.0)
    y_ref[...] = _dot(act.astype(BF16), wd_ref[...]) + bd_ref[...]


def expert_ffn(xs, block_e, w_glu, w_lin, b_glu, b_lin, w_down, b_down, row_block):
    n_rows, d = xs.shape
    n_exp, _, f = w_glu.shape
    once = pl.Buffered(1)
    grid_spec = pltpu.PrefetchScalarGridSpec(
        num_scalar_prefetch=1,
        grid=(n_rows // row_block,),
        in_specs=[pl.BlockSpec((row_block, d), lambda i, be: (i, 0)),
                  pl.BlockSpec((None, d, f), lambda i, be: (be[i], 0, 0), pipeline_mode=once),
                  pl.BlockSpec((None, d, f), lambda i, be: (be[i], 0, 0), pipeline_mode=once),
                  pl.BlockSpec((None, 1, f), lambda i, be: (be[i], 0, 0)),
                  pl.BlockSpec((None, 1, f), lambda i, be: (be[i], 0, 0)),
                  pl.BlockSpec((None, f, d), lambda i, be: (be[i], 0, 0), pipeline_mode=once),
                  pl.BlockSpec((None, 1, d), lambda i, be: (be[i], 0, 0))],
        out_specs=pl.BlockSpec((row_block, d), lambda i, be: (i, 0)))
    return pl.pallas_call(
        _expert_kernel,
        grid_spec=grid_spec,
        out_shape=jax.ShapeDtypeStruct((n_rows, d), F32),
        compiler_params=_cparams(("arbitrary",)),
        name="expert_ffn",
    )(block_e, xs, w_glu, w_lin, b_glu.reshape(n_exp, 1, f), b_lin.reshape(n_exp, 1, f),
      w_down, b_down.reshape(n_exp, 1, d))


def _combine_kernel(dest_ref, ys_ref, w_ref, x_ref, gate_ref, g_ref, b_ref, sc_ref, sh_ref,
                    x2_ref, h_ref, buf, sem, *, tc, alpha):
    def row_copy(tok, k, dest):
        return pltpu.make_async_copy(ys_ref.at[pl.ds(dest, 1), :], buf.at[k, pl.ds(tok, 1), :], sem)

    def issue(tok, carry):
        for k in range(TOP_K):
            row_copy(tok, k, dest_ref[0, tok * TOP_K + k]).start()
        return carry

    def drain(tok, carry):
        for k in range(TOP_K):
            row_copy(tok, k, 0).wait()
        return carry

    lax.fori_loop(0, tc, issue, 0)
    lax.fori_loop(0, tc, drain, 0)
    w = w_ref[...]
    y = w[:, 0:1] * buf[0]
    for k in range(1, TOP_K):
        y = y + w[:, k:k + 1] * buf[k]
    x2 = _ln(alpha * x_ref[...] + gate_ref[...] * y) * g_ref[...] + b_ref[...]
    x2_ref[...] = x2
    h_ref[...] = (_ln(x2) * (1.0 + sc_ref[...]) + sh_ref[...]).astype(h_ref.dtype)


def moe_combine(ys, dest, top_w, x1, mod_l, mod_next, ln_g, ln_b, seq, alpha):
    t, d = x1.shape
    tc = min(128, seq)
    bpb = seq // tc
    row = pl.BlockSpec((tc, d), lambda i: (i, 0))
    vec = pl.BlockSpec((1, d), lambda i: (0, 0))
    return pl.pallas_call(
        functools.partial(_combine_kernel, tc=tc, alpha=alpha),
        grid=(t // tc,),
        in_specs=[pl.BlockSpec((None, 1, tc * TOP_K), lambda i: (i, 0, 0), memory_space=pltpu.SMEM),
                  pl.BlockSpec(memory_space=pl.ANY),
                  pl.BlockSpec((tc, LANES), lambda i: (i, 0)),
                  row, _mod_spec(5, d, bpb), vec, vec, _mod_spec(1, d, bpb), _mod_spec(0, d, bpb)],
        out_specs=[row, row],
        out_shape=[jax.ShapeDtypeStruct((t, d), F32), jax.ShapeDtypeStruct((t, d), BF16)],
        scratch_shapes=[pltpu.VMEM((TOP_K, tc, d), F32), pltpu.SemaphoreType.DMA(())],
        compiler_params=_cparams(("arbitrary",), 32),
        name="moe_combine",
    )(dest.reshape(t // tc, 1, tc * TOP_K), ys, top_w, x1, mod_l, ln_g.reshape(1, d), ln_b.reshape(1, d),
      mod_next, mod_next)


def moe_ffn(h2, logits, w_glu, w_lin, b_glu, b_lin, w_down, b_down):
    t, d = h2.shape
    n_exp = w_glu.shape[0]
    row_block = 256
    idx_rank, top_w, counts = route(logits, n_exp)
    top_idx = idx_rank[:, :TOP_K]
    rank = idx_rank[:, TOP_K:2 * TOP_K]
    counts = counts[0, :n_exp].astype(jnp.int32)
    padded = (counts + row_block - 1) // row_block * row_block
    padded_end = jnp.cumsum(padded)
    padded_start = padded_end - padded
    dest = padded_start[top_idx] + rank
    n_rows = t * TOP_K + n_exp * row_block
    tok_ids = jnp.broadcast_to(jnp.arange(t, dtype=jnp.int32)[:, None], (t, TOP_K))
    row_tok = jnp.zeros((n_rows,), jnp.int32).at[dest.reshape(-1)].set(tok_ids.reshape(-1))
    n_blocks = n_rows // row_block
    block_e = jnp.minimum(
        jnp.searchsorted(padded_end, jnp.arange(n_blocks, dtype=jnp.int32) * row_block, side='right'),
        n_exp - 1).astype(jnp.int32)
    xs = gather_rows(h2, row_tok, rows_per_step=min(1024, n_rows))
    ys = expert_ffn(xs, block_e, w_glu, w_lin, b_glu, b_lin, w_down, b_down, row_block)
    return ys, dest.astype(jnp.int32), top_w


def kernel(x, c, w_ada, b_ada, w_in, b_forget, w_gate, b_gate, w_proj_sb, w_proj_dil, w_proj_fox, w_out,
           ln1_g, ln1_b, w_router, b_router, w_up, b_up, w_down, b_down, ln2_g, ln2_b):
    batch, seq, d = x.shape
    depth = w_ada.shape[0]
    t = batch * seq
    alpha = (2.0 * depth) ** 0.25
    rope_tabs = rope_pair_tables(seq)
    mod = adaln_mod(c, w_ada, b_ada)
    x2d = x.reshape(t, d)
    h = ln_mod(x2d, mod[0], seq, comp_shift=0, comp_scale=1)
    for l in range(depth):
        w_in_l = w_in[l]
        qkv = in_projection(h, w_in_l[:, :N_QKV].astype(BF16), rope_tabs, seq)
        cum = forget_cumsum(h, w_in_l[:, N_QKV:], b_forget[l], batch, seq)
        o_sb = stick_breaking_attention(qkv, batch, seq)
        o_dl = dilated_window_attention(qkv, batch, seq)
        o_fx = forgetting_attention(qkv, cum, batch, seq)
        merged = gated_merge(h, w_gate[l].astype(BF16), b_gate[l], (o_sb, o_dl, o_fx),
                             (w_proj_sb[l].astype(BF16), w_proj_dil[l].astype(BF16),
                              w_proj_fox[l].astype(BF16)))
        x1, h2, logits = out_projection(merged, w_out[l].astype(BF16), x2d, mod[l], ln1_g[l], ln1_b[l],
                                        w_router[l], b_router[l], seq, alpha)
        w_up_l = w_up[l]
        ys, dest, top_w = moe_ffn(h2, logits,
                                  w_up_l[:, :, 0::2].astype(BF16), w_up_l[:, :, 1::2].astype(BF16),
                                  b_up[l][:, 0::2], b_up[l][:, 1::2], w_down[l].astype(BF16), b_down[l])
        mod_next = mod[min(l + 1, depth - 1)]
        x2d, h = moe_combine(ys, dest, top_w, x1, mod[l], mod_next, ln2_g[l], ln2_b[l], seq, alpha)
    return x2d.reshape(batch, seq, d)
```

```python
import functools

import numpy as np
import jax
import jax.numpy as jnp
from jax import lax
from jax.experimental import pallas as pl
from jax.experimental.pallas import tpu as pltpu

F32 = jnp.float32
BF16 = jnp.bfloat16

HEAD_DIM = 64
H_SB = 8
DIL_CONFIGS = ((128, 1), (512, 4), (2048, 16))
H_DIL_PER_GROUP = 4
H_FOX = 12
ROPE_THETA = 500000.0
ROPE_DIMS = HEAD_DIM // 4
TOP_K = 4
SWIGLU_ALPHA = 1.702
SWIGLU_LIMIT = 7.0
LN_EPS = 1e-5
QK_SCALE = HEAD_DIM ** -0.5

LANES = 128
SUBLANES = 8
VMEM_BUDGET_MB = 56

N_DIL_GROUPS = len(DIL_CONFIGS)
W_SB = H_SB * HEAD_DIM
W_DIL = N_DIL_GROUPS * H_DIL_PER_GROUP * HEAD_DIM
W_DIL_OUT = H_DIL_PER_GROUP * HEAD_DIM
W_FOX = H_FOX * HEAD_DIM
N_QKV = 3 * (W_SB + W_DIL + W_FOX)
COL_SB = (0, W_SB // LANES, 2 * W_SB // LANES)
_DL0 = 3 * W_SB // LANES
COL_DL = (_DL0, _DL0 + W_DIL // LANES, _DL0 + 2 * W_DIL // LANES)
_FX0 = _DL0 + 3 * W_DIL // LANES
COL_FX = (_FX0, _FX0 + W_FOX // LANES, _FX0 + 2 * W_FOX // LANES)
BAND = 128
NEG_BIG = -1e30


def _cparams(semantics, vmem_mb=VMEM_BUDGET_MB):
    return pltpu.CompilerParams(dimension_semantics=semantics, vmem_limit_bytes=vmem_mb * 2 ** 20)


def _dot(a, b):
    return jnp.dot(a, b, preferred_element_type=F32)


def _dot_t(a, b):
    return lax.dot_general(a, b, (((1,), (1,)), ((), ())), preferred_element_type=F32)


def _ln(x):
    mu = jnp.mean(x, axis=-1, keepdims=True)
    xc = x - mu
    var = jnp.mean(xc * xc, axis=-1, keepdims=True)
    return xc * lax.rsqrt(var + LN_EPS)


def _softplus_neg_abs(z):
    return jnp.log1p(jnp.exp(-jnp.abs(z)))


def _split3(x):
    hi = x.astype(BF16)
    r1 = x - hi.astype(F32)
    mid = r1.astype(BF16)
    lo = (r1 - mid.astype(F32)).astype(BF16)
    return hi, mid, lo


def _mod_kernel(c_ref, w_ref, b_ref, o_ref):
    c = c_ref[...]
    act = (c * jax.nn.sigmoid(c)).astype(BF16)
    o_ref[...] = _dot(act, w_ref[...].astype(BF16)) + b_ref[...]


def adaln_mod(c, w_ada, b_ada):
    n_layers, d, n = w_ada.shape
    b = c.shape[0]
    assert b <= SUBLANES
    tn = 512
    c_pad = jnp.zeros((SUBLANES, d), F32).at[:b].set(c)
    out = pl.pallas_call(
        _mod_kernel,
        grid=(n_layers, n // tn),
        in_specs=[pl.BlockSpec((SUBLANES, d), lambda l, j: (0, 0)),
                  pl.BlockSpec((None, d, tn), lambda l, j: (l, 0, j)),
                  pl.BlockSpec((None, 1, tn), lambda l, j: (l, 0, j))],
        out_specs=pl.BlockSpec((None, SUBLANES, tn), lambda l, j: (l, 0, j)),
        out_shape=jax.ShapeDtypeStruct((n_layers, SUBLANES, n), F32),
        compiler_params=_cparams(("arbitrary", "arbitrary"), 32),
        name="adaln_mod",
    )(c_pad, w_ada, b_ada.reshape(n_layers, 1, n))
    return out[:, :b].reshape(n_layers, b, 6, d).transpose(0, 2, 1, 3)[:, :, :, None, :]


def _mod_spec(comp, d, rows_per_batch_blocks):
    return pl.BlockSpec((None, None, 1, d), lambda i, *_: (comp, i // rows_per_batch_blocks, 0, 0))


def _ln_mod_kernel(x_ref, sc_ref, sh_ref, h_ref):
    h_ref[...] = (_ln(x_ref[...]) * (1.0 + sc_ref[...]) + sh_ref[...]).astype(h_ref.dtype)


def ln_mod(x2d, mod_l, seq, comp_shift, comp_scale):
    t, d = x2d.shape
    ts = min(512, seq)
    bpb = seq // ts
    return pl.pallas_call(
        _ln_mod_kernel,
        grid=(t // ts,),
        in_specs=[pl.BlockSpec((ts, d), lambda i: (i, 0)),
                  _mod_spec(comp_scale, d, bpb), _mod_spec(comp_shift, d, bpb)],
        out_specs=pl.BlockSpec((ts, d), lambda i: (i, 0)),
        out_shape=jax.ShapeDtypeStruct((t, d), BF16),
        compiler_params=_cparams(("arbitrary",), 32),
        name="ln_mod",
    )(x2d, mod_l, mod_l)


def _inproj_kernel(h_ref, w_ref, cos_ref, s1_ref, s2_ref, o_ref, *, rope_lo, rope_hi, tn):
    j = pl.program_id(1)
    acc = _dot(h_ref[...], w_ref[...])
    is_rope = jnp.logical_and(j >= rope_lo, j < rope_hi)

    @pl.when(is_rope)
    def _():
        c, s1, s2 = cos_ref[...], s1_ref[...], s2_ref[...]
        for blk in range(tn // LANES):
            a = acc[:, blk * LANES:(blk + 1) * LANES]
            r = a * c + pltpu.roll(a, LANES - ROPE_DIMS // 2, 1) * s1 + pltpu.roll(a, ROPE_DIMS // 2, 1) * s2
            o_ref[:, blk * LANES:(blk + 1) * LANES] = r.astype(o_ref.dtype)

    @pl.when(jnp.logical_not(is_rope))
    def _():
        o_ref[...] = acc.astype(o_ref.dtype)


def rope_pair_tables(seq):
    pos = jnp.arange(seq, dtype=F32)
    inv = ROPE_THETA ** (-jnp.arange(0, ROPE_DIMS, 2, dtype=F32) / ROPE_DIMS)
    ang = pos[:, None] * inv[None, :]
    cos, sin = jnp.cos(ang), jnp.sin(ang)
    half = ROPE_DIMS // 2
    head_c = jnp.concatenate([cos, cos, jnp.ones((seq, HEAD_DIM - ROPE_DIMS), F32)], axis=1)
    head_s1 = jnp.concatenate([-sin, jnp.zeros((seq, HEAD_DIM - half), F32)], axis=1)
    head_s2 = jnp.concatenate([jnp.zeros((seq, half), F32), sin,
                               jnp.zeros((seq, HEAD_DIM - ROPE_DIMS), F32)], axis=1)
    two = lambda a: jnp.concatenate([a, a], axis=1)
    return two(head_c), two(head_s1), two(head_s2)


def in_projection(h, w_qkv, rope_tabs, seq):
    t, d = h.shape
    n = w_qkv.shape[1]
    tm = min(1024, seq)
    tn = 512
    assert seq % tm == 0 and n % tn == 0
    rope_lo, rope_hi = COL_DL[0] * LANES, COL_DL[2] * LANES
    assert rope_lo % tn == 0 and rope_hi % tn == 0
    spb = seq // tm
    tab_spec = pl.BlockSpec((tm, LANES), lambda i, j: (i % spb, 0))
    return pl.pallas_call(
        functools.partial(_inproj_kernel, rope_lo=rope_lo // tn, rope_hi=rope_hi // tn, tn=tn),
        grid=(t // tm, n // tn),
        in_specs=[pl.BlockSpec((tm, d), lambda i, j: (i, 0)),
                  pl.BlockSpec((d, tn), lambda i, j: (0, j)),
                  tab_spec, tab_spec, tab_spec],
        out_specs=pl.BlockSpec((tm, tn), lambda i, j: (i, j)),
        out_shape=jax.ShapeDtypeStruct((t, n), BF16),
        compiler_params=_cparams(("arbitrary", "arbitrary"), 48),
        name="in_projection",
    )(h, w_qkv, *rope_tabs)


FEAT_PER_HEAD = 6


def _forget_feature_maps():
    width = (H_FOX // 2) * LANES
    pq = np.zeros((3 * LANES, width), np.float32)
    pk = np.zeros((3 * LANES, width), np.float32)
    cq = np.zeros((1, width), np.float32)
    ck = np.zeros((1, width), np.float32)
    for head in range(H_FOX):
        base = (head // 2) * LANES + (head % 2) * FEAT_PER_HEAD
        for piece in range(3):
            pq[piece * LANES + head, base + piece] = 1.0
            pk[piece * LANES + head, base + 3 + piece] = -1.0
        cq[0, base + 3:base + 6] = 1.0
        ck[0, base:base + 3] = 1.0
    return pq, pk, cq, ck


def _forget_kernel(h_ref, w_ref, b_ref, tri_ref, pq_ref, pk_ref, cq_ref, ck_ref, qf_ref, kf_ref, carry_ref):
    @pl.when(pl.program_id(1) == 0)
    def _():
        carry_ref[...] = jnp.zeros_like(carry_ref)

    f = _dot(h_ref[...], w_ref[...]) + b_ref[...]
    log_f = jnp.minimum(f, 0.0) - _softplus_neg_abs(f)
    tri = tri_ref[...]
    hi, mid, lo = _split3(log_f)
    cum = _dot(tri, hi) + _dot(tri, mid) + _dot(tri, lo) + carry_ref[0:1, :]
    carry_ref[...] = jnp.broadcast_to(cum[-1:, :], carry_ref.shape)
    pieces = jnp.concatenate(_split3(cum), axis=1)
    qf_ref[...] = (_dot(pieces, pq_ref[...]) + cq_ref[...]).astype(qf_ref.dtype)
    kf_ref[...] = (_dot(pieces, pk_ref[...]) + ck_ref[...]).astype(kf_ref.dtype)


def forget_features(h, w_f, b_f, batch, seq):
    t, d = h.shape
    ts = min(256, seq)
    nsb = seq // ts
    w_pad = jnp.zeros((d, LANES), BF16).at[:, :H_FOX].set(w_f.astype(BF16))
    b_pad = jnp.zeros((1, LANES), F32).at[0, :H_FOX].set(b_f)
    tri = (np.arange(ts)[:, None] >= np.arange(ts)[None, :]).astype(np.float32)
    pq, pk, cq, ck = _forget_feature_maps()
    width = pq.shape[1]
    const = lambda shape: pl.BlockSpec(shape, lambda b, s: (0, 0))
    out_spec = pl.BlockSpec((ts, width), lambda b, s: (b * nsb + s, 0))
    out_shape = jax.ShapeDtypeStruct((t, width), BF16)
    return pl.pallas_call(
        _forget_kernel,
        grid=(batch, nsb),
        in_specs=[pl.BlockSpec((ts, d), lambda b, s: (b * nsb + s, 0)),
                  const((d, LANES)), const((1, LANES)), const((ts, ts)),
                  const(pq.shape), const(pk.shape), const(cq.shape), const(ck.shape)],
        out_specs=[out_spec, out_spec],
        out_shape=[out_shape, out_shape],
        scratch_shapes=[pltpu.VMEM((SUBLANES, LANES), F32)],
        compiler_params=_cparams(("arbitrary", "arbitrary"), 32),
        name="forget_features",
    )(h, w_pad, b_pad, jnp.asarray(tri, BF16), jnp.asarray(pq, BF16), jnp.asarray(pk, BF16),
      jnp.asarray(cq), jnp.asarray(ck))


FLAG_FIRST, FLAG_LAST, FLAG_MASKED = 1, 2, 4


def _causal_schedule(seq, tq, tk, strict, reverse):
    qi, kb, fl = [], [], []
    for i in range(seq // tq):
        q_lo, q_hi = i * tq, (i + 1) * tq - 1
        k_max = q_hi - 1 if strict else q_hi
        blocks = list(range(max(k_max, 0) // tk + 1))
        if reverse:
            blocks = blocks[::-1]
        for n, kblk in enumerate(blocks):
            k_hi = kblk * tk + tk - 1
            masked = (k_hi >= q_lo) if strict else (k_hi > q_lo)
            qi.append(i)
            kb.append(kblk)
            fl.append((FLAG_FIRST if n == 0 else 0) | (FLAG_LAST if n == len(blocks) - 1 else 0)
                      | (FLAG_MASKED if masked else 0))
    as_i32 = lambda a: jnp.asarray(np.asarray(a, np.int32))
    return as_i32(qi), as_i32(kb), as_i32(fl)


def _pair_masks(tq):
    lane = lax.broadcasted_iota(jnp.int32, (tq, LANES), 1)
    return lane < HEAD_DIM


KEY_SUB = 128
QRY_SUB = 128
SUFFIX_SUB = 256


def _diag_tile_kind(ks, qs, key_sub, qry_sub, strict):
    k_lo, k_hi = ks * key_sub, ks * key_sub + key_sub - 1
    q_lo, q_hi = qs * qry_sub, qs * qry_sub + qry_sub - 1
    if strict:
        if k_lo >= q_hi:
            return 'dead'
        return 'full' if k_hi < q_lo else 'partial'
    if k_lo > q_hi:
        return 'dead'
    return 'full' if k_hi <= q_lo else 'partial'


def _tile_allowed(ks, qs, key_sub, qry_sub, strict):
    kpos = ks * key_sub + lax.broadcasted_iota(jnp.int32, (key_sub, qry_sub), 0)
    qpos = qs * qry_sub + lax.broadcasted_iota(jnp.int32, (key_sub, qry_sub), 1)
    return kpos < qpos if strict else kpos <= qpos


def _pair_transpose_in(qkv3, col, width):
    return lax.slice_in_dim(qkv3, col * LANES, col * LANES + width, axis=2).transpose(0, 2, 1)


def _sb_kernel(qi_ref, kb_ref, fl_ref, q_ref, k_ref, vt_ref, later_ref, o_ref,
               qx_s, acc_s, run_s, first_s, z_s, hi_s, lo_s, btw_s, w_s, *, tq, tk, key_sub, qry_sub):
    s = pl.program_id(2)
    flags = fl_ref[s]

    @pl.when((flags & FLAG_FIRST) != 0)
    def _():
        head_a = _pair_masks(tq)
        q = q_ref[...] * QK_SCALE
        qx_s[0] = jnp.where(head_a, q, 0).astype(BF16)
        qx_s[1] = jnp.where(head_a, 0, q).astype(BF16)
        acc_s[...] = jnp.zeros_like(acc_s)
        run_s[...] = jnp.zeros_like(run_s)

    def step(diag):
        suf = later_ref.shape[0]
        n_suf = tk // suf
        per_suf = suf // key_sub
        tiles = [(ks, qs) for ks in range(tk // key_sub) for qs in range(tq // qry_sub)]
        kind_of = lambda ks, qs: _diag_tile_kind(ks, qs, key_sub, qry_sub, True) if diag else 'full'
        k = k_ref[...]
        for head in range(2):
            z_s[head] = _dot_t(k, qx_s[head])
        for head in range(2):
            for ks, qs in tiles:
                kl = slice(ks * key_sub, (ks + 1) * key_sub)
                ql = slice(qs * qry_sub, (qs + 1) * qry_sub)
                first_row = head * n_suf + ks // per_suf
                kind = kind_of(ks, qs)
                if kind == 'dead':
                    hi_s[head, kl, ql] = jnp.zeros((key_sub, qry_sub), BF16)
                    lo_s[head, kl, ql] = jnp.zeros((key_sub, qry_sub), BF16)
                    if ks % per_suf == 0:
                        first_s[first_row:first_row + 1, ql] = jnp.zeros((1, qry_sub), F32)
                    continue
                z = z_s[head, kl, ql]
                log_stop = jnp.minimum(z, 0.0) - jnp.log(1.0 + jnp.exp(-jnp.abs(z)))
                log_cont = log_stop - z
                if kind == 'partial':
                    log_cont = jnp.where(_tile_allowed(ks, qs, key_sub, qry_sub, True), log_cont, 0.0)
                z_s[head, kl, ql] = log_stop
                hi = log_cont.astype(BF16)
                hi_s[head, kl, ql] = hi
                lo_s[head, kl, ql] = (log_cont - hi.astype(F32)).astype(BF16)
                if ks % per_suf == 0:
                    first_s[first_row:first_row + 1, ql] = log_cont[0:1, :]
        later = later_ref[...]
        for head in range(2):
            for blk in range(n_suf):
                bl = slice(blk * suf, (blk + 1) * suf)
                btw_s[head, bl, :] = _dot(later, hi_s[head, bl, :]) + _dot(later, lo_s[head, bl, :])

        def block_sum(head, blk, ql):
            row = head * n_suf + blk
            return btw_s[head, blk * suf:blk * suf + 1, ql] + first_s[row:row + 1, ql]

        for head in range(2):
            for ks, qs in tiles:
                kl = slice(ks * key_sub, (ks + 1) * key_sub)
                ql = slice(qs * qry_sub, (qs + 1) * qry_sub)
                kind = kind_of(ks, qs)
                if kind == 'dead':
                    w_s[head, kl, ql] = jnp.zeros((key_sub, qry_sub), BF16)
                    continue
                after = run_s[head:head + 1, ql]
                for blk in range(ks // per_suf + 1, n_suf):
                    after = after + block_sum(head, blk, ql)
                w = jnp.exp(z_s[head, kl, ql] + btw_s[head, kl, ql] + after)
                if kind == 'partial':
                    w = jnp.where(_tile_allowed(ks, qs, key_sub, qry_sub, True), w, 0.0)
                w_s[head, kl, ql] = w.astype(BF16)
        for head in range(2):
            rows = slice(head * HEAD_DIM, (head + 1) * HEAD_DIM)
            acc_s[rows, :] += _dot(vt_ref[rows, :], w_s[head])
            total = run_s[head:head + 1, :]
            for blk in range(n_suf):
                total = total + block_sum(head, blk, slice(None))
            run_s[head:head + 1, :] = total

    masked = (flags & FLAG_MASKED) != 0

    @pl.when(masked)
    def _():
        step(True)

    @pl.when(jnp.logical_not(masked))
    def _():
        step(False)

    @pl.when((flags & FLAG_LAST) != 0)
    def _():
        o_ref[...] = acc_s[...].astype(o_ref.dtype)


def stick_breaking_attention(qkv, batch, seq):
    tq = tk = min(512, seq)
    key_sub, qry_sub = min(KEY_SUB, tk), min(QRY_SUB, tq)
    sched = _causal_schedule(seq, tq, tk, strict=True, reverse=True)
    n_steps = int(sched[0].shape[0])
    n_pairs = W_SB // LANES
    qkv3 = qkv.reshape(batch, seq, qkv.shape[-1])
    qc, kc, vc = COL_SB
    v_t = _pair_transpose_in(qkv3, vc, W_SB)
    suf = min(SUFFIX_SUB, tk)
    assert 2 * (tk // suf) <= SUBLANES and suf % key_sub == 0
    later = jnp.asarray((np.arange(suf)[None, :] > np.arange(suf)[:, None]).astype(np.float32), BF16)
    block_f32 = pltpu.VMEM((2, tk, tq), F32)
    block_bf16 = pltpu.VMEM((2, tk, tq), BF16)
    row_stats = pltpu.VMEM((SUBLANES, tq), F32)
    grid_spec = pltpu.PrefetchScalarGridSpec(
        num_scalar_prefetch=3,
        grid=(batch, n_pairs, n_steps),
        in_specs=[pl.BlockSpec((None, tq, LANES), lambda b, p, s, qi, kb, fl: (b, qi[s], qc + p)),
                  pl.BlockSpec((None, tk, LANES), lambda b, p, s, qi, kb, fl: (b, kb[s], kc + p)),
                  pl.BlockSpec((None, LANES, tk), lambda b, p, s, qi, kb, fl: (b, p, kb[s])),
                  pl.BlockSpec((suf, suf), lambda b, p, s, qi, kb, fl: (0, 0))],
        out_specs=pl.BlockSpec((None, LANES, tq), lambda b, p, s, qi, kb, fl: (b, p, qi[s])),
        scratch_shapes=[pltpu.VMEM((2, tq, LANES), BF16), pltpu.VMEM((LANES, tq), F32),
                        row_stats, row_stats, block_f32, block_bf16, block_bf16, block_f32, block_bf16])
    out_t = pl.pallas_call(
        functools.partial(_sb_kernel, tq=tq, tk=tk, key_sub=key_sub, qry_sub=qry_sub),
        grid_spec=grid_spec,
        out_shape=jax.ShapeDtypeStruct((batch, W_SB, seq), BF16),
        compiler_params=_cparams(("arbitrary", "arbitrary", "arbitrary"), 32),
        name="stick_breaking_attention",
    )(*sched, qkv3, qkv3, v_t, later)
    return out_t.transpose(0, 2, 1).reshape(batch * seq, W_SB)


def _fox_kernel(qi_ref, kb_ref, fl_ref, q_ref, qf_ref, k_ref, kf_ref, vt_ref, o_ref,
                qx_s, acc_s, m_s, l_s, alpha_s, z_s, p_s, *, tq, tk, key_sub, qry_sub):
    s = pl.program_id(2)
    flags = fl_ref[s]

    @pl.when((flags & FLAG_FIRST) != 0)
    def _():
        lane = lax.broadcasted_iota(jnp.int32, (tq, LANES), 1)
        q = q_ref[...] * QK_SCALE
        qf = qf_ref[...]
        for head in range(2):
            in_head = (lane >= head * HEAD_DIM) & (lane < (head + 1) * HEAD_DIM)
            in_feat = (lane >= head * FEAT_PER_HEAD) & (lane < (head + 1) * FEAT_PER_HEAD)
            qx_s[head] = jnp.concatenate([jnp.where(in_head, q, 0).astype(BF16),
                                          jnp.where(in_feat, qf, 0).astype(BF16)], axis=1)
        acc_s[...] = jnp.zeros_like(acc_s)
        l_s[...] = jnp.zeros_like(l_s)
        m_s[...] = jnp.full_like(m_s, NEG_BIG)

    def step(diag):
        kind_of = lambda ks, qs: _diag_tile_kind(ks, qs, key_sub, qry_sub, False) if diag else 'full'
        kx = jnp.concatenate([k_ref[...], kf_ref[...]], axis=1)
        for head in range(2):
            z_s[head] = _dot_t(kx, qx_s[head])
        for head in range(2):
            for qs in range(tq // qry_sub):
                ql = slice(qs * qry_sub, (qs + 1) * qry_sub)

                def score(ks):
                    z = z_s[head, ks * key_sub:(ks + 1) * key_sub, ql]
                    if kind_of(ks, qs) == 'partial':
                        z = jnp.where(_tile_allowed(ks, qs, key_sub, qry_sub, False), z, NEG_BIG)
                    return z

                live = [ks for ks in range(tk // key_sub) if kind_of(ks, qs) != 'dead']
                m_old = m_s[head:head + 1, ql]
                m_new = m_old
                for ks in live:
                    m_new = jnp.maximum(m_new, jnp.max(score(ks), axis=0, keepdims=True))
                alpha = jnp.exp(m_old - m_new)
                total = alpha * l_s[head:head + 1, ql]
                for ks in range(tk // key_sub):
                    kl = slice(ks * key_sub, (ks + 1) * key_sub)
                    if ks not in live:
                        p_s[head, kl, ql] = jnp.zeros((key_sub, qry_sub), BF16)
                        continue
                    prob = jnp.exp(score(ks) - m_new)
                    total = total + jnp.sum(prob, axis=0, keepdims=True)
                    p_s[head, kl, ql] = prob.astype(BF16)
                m_s[head:head + 1, ql] = m_new
                l_s[head:head + 1, ql] = total
                alpha_s[head:head + 1, ql] = alpha
        for head in range(2):
            rows = slice(head * HEAD_DIM, (head + 1) * HEAD_DIM)
            acc_s[rows, :] = alpha_s[head:head + 1, :] * acc_s[rows, :] + _dot(vt_ref[rows, :], p_s[head])

    masked = (flags & FLAG_MASKED) != 0

    @pl.when(masked)
    def _():
        step(True)

    @pl.when(jnp.logical_not(masked))
    def _():
        step(False)

    @pl.when((flags & FLAG_LAST) != 0)
    def _():
        for head in range(2):
            rows = slice(head * HEAD_DIM, (head + 1) * HEAD_DIM)
            o_ref[rows, :] = (acc_s[rows, :] / l_s[head:head + 1, :]).astype(o_ref.dtype)


def forgetting_attention(qkv, q_feat, k_feat, batch, seq):
    tq = tk = min(512, seq)
    key_sub, qry_sub = min(KEY_SUB, tk), min(QRY_SUB, tq)
    sched = _causal_schedule(seq, tq, tk, strict=False, reverse=False)
    n_steps = int(sched[0].shape[0])
    n_pairs = W_FOX // LANES
    qkv3 = qkv.reshape(batch, seq, qkv.shape[-1])
    qf3 = q_feat.reshape(batch, seq, n_pairs * LANES)
    kf3 = k_feat.reshape(batch, seq, n_pairs * LANES)
    qc, kc, vc = COL_FX
    v_t = _pair_transpose_in(qkv3, vc, W_FOX)
    grid_spec = pltpu.PrefetchScalarGridSpec(
        num_scalar_prefetch=3,
        grid=(batch, n_pairs, n_steps),
        in_specs=[pl.BlockSpec((None, tq, LANES), lambda b, p, s, qi, kb, fl: (b, qi[s], qc + p)),
                  pl.BlockSpec((None, tq, LANES), lambda b, p, s, qi, kb, fl: (b, qi[s], p)),
                  pl.BlockSpec((None, tk, LANES), lambda b, p, s, qi, kb, fl: (b, kb[s], kc + p)),
                  pl.BlockSpec((None, tk, LANES), lambda b, p, s, qi, kb, fl: (b, kb[s], p)),
                  pl.BlockSpec((None, LANES, tk), lambda b, p, s, qi, kb, fl: (b, p, kb[s]))],
        out_specs=pl.BlockSpec((None, LANES, tq), lambda b, p, s, qi, kb, fl: (b, p, qi[s])),
        scratch_shapes=[pltpu.VMEM((2, tq, 2 * LANES), BF16), pltpu.VMEM((LANES, tq), F32),
                        pltpu.VMEM((SUBLANES, tq), F32), pltpu.VMEM((SUBLANES, tq), F32),
                        pltpu.VMEM((SUBLANES, tq), F32), pltpu.VMEM((2, tk, tq), F32),
                        pltpu.VMEM((2, tk, tq), BF16)])
    out_t = pl.pallas_call(
        functools.partial(_fox_kernel, tq=tq, tk=tk, key_sub=key_sub, qry_sub=qry_sub),
        grid_spec=grid_spec,
        out_shape=jax.ShapeDtypeStruct((batch, W_FOX, seq), BF16),
        compiler_params=_cparams(("arbitrary", "arbitrary", "arbitrary"), 32),
        name="forgetting_attention",
    )(*sched, qkv3, qf3, qkv3, kf3, v_t)
    return out_t.transpose(0, 2, 1).reshape(batch * seq, W_FOX)


def _band_kernel(q_ref, kc_ref, vc_ref, kp_ref, vp_ref, o_ref, lse_ref, *, tq):
    i = pl.program_id(2)
    head_a = _pair_masks(tq)
    q = q_ref[...] * QK_SCALE
    kc, vc, kp, vp = kc_ref[...], vc_ref[...], kp_ref[...], vp_ref[...]
    diff = lax.broadcasted_iota(jnp.int32, (tq, tq), 0) - lax.broadcasted_iota(jnp.int32, (tq, tq), 1)
    mask_c = jnp.logical_and(diff >= 0, diff <= BAND)
    rp = lax.broadcasted_iota(jnp.int32, (tq, BAND), 0)
    cp = lax.broadcasted_iota(jnp.int32, (tq, BAND), 1)
    mask_p = jnp.logical_and(cp >= rp, i > 0)
    outs, lses = [], []
    for sel in (head_a, jnp.logical_not(head_a)):
        qh = jnp.where(sel, q, 0).astype(BF16)
        sc = jnp.where(mask_c, _dot_t(qh, kc), NEG_BIG)
        sp = jnp.where(mask_p, _dot_t(qh, kp), NEG_BIG)
        m = jnp.maximum(jnp.max(sc, axis=1, keepdims=True), jnp.max(sp, axis=1, keepdims=True))
        ec = jnp.exp(sc - m)
        ep = jnp.exp(sp - m)
        den = jnp.sum(ec, axis=1, keepdims=True) + jnp.sum(ep, axis=1, keepdims=True)
        outs.append((_dot(ec.astype(BF16), vc) + _dot(ep.astype(BF16), vp)) / den)
        lses.append(m + jnp.log(den))
    o_ref[...] = jnp.where(head_a, outs[0], outs[1])
    lse_ref[...] = jnp.where(head_a, lses[0], lses[1])


def _band_attention(q, k, v):
    nb, u, w = q.shape
    tq = min(256, u)
    assert u % tq == 0 and tq % BAND == 0
    sub = tq // BAND
    cur = pl.BlockSpec((None, tq, LANES), lambda n, p, i: (n, i, p))
    prev = pl.BlockSpec((None, BAND, LANES), lambda n, p, i: (n, jnp.maximum(i * sub - 1, 0), p))
    shp = jax.ShapeDtypeStruct((nb, u, w), F32)
    return pl.pallas_call(
        functools.partial(_band_kernel, tq=tq),
        grid=(nb, w // LANES, u // tq),
        in_specs=[cur, cur, cur, prev, prev],
        out_specs=[cur, cur],
        out_shape=[shp, shp],
        compiler_params=_cparams(("arbitrary", "arbitrary", "arbitrary"), 32),
        name="band_attention",
    )(q, k, v, k, v)


def _dil_mix_kernel(o0, o1, o2, l0, l1, l2, out_ref):
    a, b, c = l0[...], l1[...], l2[...]
    m = jnp.maximum(jnp.maximum(a, b), c)
    ea, eb, ec = jnp.exp(a - m), jnp.exp(b - m), jnp.exp(c - m)
    tot = ea + eb + ec
    out_ref[...] = ((o0[...] * ea + o1[...] * eb + o2[...] * ec) / tot).astype(out_ref.dtype)


def dilated_window_attention(qkv, batch, seq):
    t = batch * seq
    qkv3 = qkv.reshape(batch, seq, qkv.shape[-1])
    gw = W_DIL_OUT
    outs, lses = [], []
    for g, (window, dil) in enumerate(DIL_CONFIGS):
        assert window // dil == BAND and seq % (dil * BAND) == 0
        u = seq // dil

        def split(col):
            a = lax.slice_in_dim(qkv3, col * LANES + g * gw, col * LANES + (g + 1) * gw, axis=2)
            return a.reshape(batch, u, dil, gw).transpose(0, 2, 1, 3).reshape(batch * dil, u, gw)

        o, lse = _band_attention(split(COL_DL[0]), split(COL_DL[1]), split(COL_DL[2]))
        merge = lambda a: a.reshape(batch, dil, u, gw).transpose(0, 2, 1, 3).reshape(t, gw)
        outs.append(merge(o))
        lses.append(merge(lse))
    tm = min(1024, t)
    spec = pl.BlockSpec((tm, gw), lambda i: (i, 0))
    return pl.pallas_call(
        _dil_mix_kernel,
        grid=(t // tm,),
        in_specs=[spec] * 6,
        out_specs=spec,
        out_shape=jax.ShapeDtypeStruct((t, gw), BF16),
        compiler_params=_cparams(("arbitrary",), 32),
        name="dilated_mix",
    )(*outs, *lses)


def _merge_kernel(h_ref, wg0, wg1, wg2, bg0, bg1, bg2, o0, o1, o2, wp0, wp1, wp2, out_ref):
    h = h_ref[...]

    def branch(wg, bg, o, wp):
        return jax.nn.sigmoid(_dot(h, wg[...]) + bg[...]) * _dot(o[...], wp[...])

    merged = branch(wg0, bg0, o0, wp0) + branch(wg1, bg1, o1, wp1) + branch(wg2, bg2, o2, wp2)
    out_ref[...] = merged.astype(out_ref.dtype)


def gated_merge(h, w_gate, b_gate, branch_outs, branch_projs):
    t, d = h.shape
    tm, tn = min(512, t), min(512, d)
    nj = d // tn
    b_gate2 = b_gate.reshape(1, -1)
    gate_w = [pl.BlockSpec((d, tn), lambda j, i, br=br: (0, br * nj + j)) for br in range(3)]
    gate_b = [pl.BlockSpec((1, tn), lambda j, i, br=br: (0, br * nj + j)) for br in range(3)]
    o_specs = [pl.BlockSpec((tm, o.shape[1]), lambda j, i: (i, 0)) for o in branch_outs]
    p_specs = [pl.BlockSpec((w.shape[0], tn), lambda j, i: (0, j)) for w in branch_projs]
    return pl.pallas_call(
        _merge_kernel,
        grid=(nj, t // tm),
        in_specs=[pl.BlockSpec((tm, d), lambda j, i: (i, 0))] + gate_w + gate_b + o_specs + p_specs,
        out_specs=pl.BlockSpec((tm, tn), lambda j, i: (i, j)),
        out_shape=jax.ShapeDtypeStruct((t, d), BF16),
        compiler_params=_cparams(("arbitrary", "arbitrary"), 48),
        name="gated_merge",
    )(h, w_gate, w_gate, w_gate, b_gate2, b_gate2, b_gate2, *branch_outs, *branch_projs)


def _outproj_kernel(m_ref, w_ref, x_ref, gate_ref, g_ref, b_ref, sc_ref, sh_ref, wrh_ref, wrl_ref, br_ref,
                    x1_ref, h2_ref, lg_ref, *, alpha):
    y = _dot(m_ref[...], w_ref[...])
    x1 = _ln(alpha * x_ref[...] + gate_ref[...] * y) * g_ref[...] + b_ref[...]
    x1_ref[...] = x1
    h2 = _ln(x1) * (1.0 + sc_ref[...]) + sh_ref[...]
    h2_ref[...] = h2
    hi = h2.astype(BF16)
    lo = (h2 - hi.astype(F32)).astype(BF16)
    wrh = wrh_ref[...]
    lg_ref[...] = _dot(hi, wrh) + _dot(lo, wrh) + _dot(hi, wrl_ref[...]) + br_ref[...]


def out_projection(merged, w_out, x2d, mod_l, ln_g, ln_b, w_router, b_router, seq, alpha):
    t, d = x2d.shape
    tm = min(256, seq)
    bpb = seq // tm
    n_exp = w_router.shape[1]
    wr = jnp.zeros((d, LANES), F32).at[:, :n_exp].set(w_router)
    wr_hi = wr.astype(BF16)
    wr_lo = (wr - wr_hi.astype(F32)).astype(BF16)
    br = jnp.zeros((1, LANES), F32).at[0, :n_exp].set(b_router)
    row = pl.BlockSpec((tm, d), lambda i: (i, 0))
    vec = pl.BlockSpec((1, d), lambda i: (0, 0))
    wr_spec = pl.BlockSpec((d, LANES), lambda i: (0, 0))
    return pl.pallas_call(
        functools.partial(_outproj_kernel, alpha=alpha),
        grid=(t // tm,),
        in_specs=[row, pl.BlockSpec((d, d), lambda i: (0, 0)), row, _mod_spec(2, d, bpb), vec, vec,
                  _mod_spec(4, d, bpb), _mod_spec(3, d, bpb), wr_spec, wr_spec,
                  pl.BlockSpec((1, LANES), lambda i: (0, 0))],
        out_specs=[row, row, pl.BlockSpec((tm, LANES), lambda i: (i, 0))],
        out_shape=[jax.ShapeDtypeStruct((t, d), F32), jax.ShapeDtypeStruct((t, d), F32),
                   jax.ShapeDtypeStruct((t, LANES), F32)],
        compiler_params=_cparams(("arbitrary",), 48),
        name="out_projection",
    )(merged, w_out, x2d, mod_l, ln_g.reshape(1, d), ln_b.reshape(1, d), mod_l, mod_l, wr_hi, wr_lo, br)


def _route_kernel(lg_ref, tri_ref, ir_ref, w_ref, cnt_ref, carry_ref, *, n_exp):
    @pl.when(pl.program_id(0) == 0)
    def _():
        carry_ref[...] = jnp.zeros_like(carry_ref)

    tr = lg_ref.shape[0]
    lane = lax.broadcasted_iota(jnp.int32, (tr, LANES), 1)
    logits = jnp.where(lane < n_exp, lg_ref[...], -jnp.inf)
    vals, sels, idxs = [], [], []
    for _ in range(TOP_K):
        m = jnp.max(logits, axis=1, keepdims=True)
        idx = jnp.min(jnp.where(logits == m, lane, LANES), axis=1, keepdims=True)
        sel = lane == idx
        vals.append(m)
        idxs.append(idx)
        sels.append(sel)
        logits = jnp.where(sel, -jnp.inf, logits)
    exps = [jnp.exp(v - vals[0]) for v in vals]
    tot = exps[0] + exps[1] + exps[2] + exps[3]
    onehot = jnp.zeros((tr, LANES), F32)
    for sel in sels:
        onehot = onehot + jnp.where(sel, 1.0, 0.0)
    before = _dot(tri_ref[...], onehot.astype(BF16)) + carry_ref[0:1, :]
    ir = jnp.zeros((tr, LANES), jnp.int32)
    wt = jnp.zeros((tr, LANES), F32)
    for k in range(TOP_K):
        rank = jnp.sum(jnp.where(sels[k], before, 0.0), axis=1, keepdims=True).astype(jnp.int32)
        ir = jnp.where(lane == k, idxs[k], ir)
        ir = jnp.where(lane == TOP_K + k, rank, ir)
        wt = jnp.where(lane == k, exps[k] / tot, wt)
    ir_ref[...] = ir
    w_ref[...] = wt
    total = before[-1:, :] + onehot[-1:, :]
    carry_ref[...] = jnp.broadcast_to(total, carry_ref.shape)
    cnt_ref[...] = jnp.broadcast_to(total, cnt_ref.shape)


def route(logits, n_exp):
    t = logits.shape[0]
    tr = min(512, t)
    tri = jnp.asarray((np.arange(tr)[:, None] > np.arange(tr)[None, :]).astype(np.float32), BF16)
    row = pl.BlockSpec((tr, LANES), lambda i: (i, 0))
    return pl.pallas_call(
        functools.partial(_route_kernel, n_exp=n_exp),
        grid=(t // tr,),
        in_specs=[row, pl.BlockSpec((tr, tr), lambda i: (0, 0))],
        out_specs=[row, row, pl.BlockSpec((SUBLANES, LANES), lambda i: (0, 0))],
        out_shape=[jax.ShapeDtypeStruct((t, LANES), jnp.int32), jax.ShapeDtypeStruct((t, LANES), F32),
                   jax.ShapeDtypeStruct((SUBLANES, LANES), F32)],
        scratch_shapes=[pltpu.VMEM((SUBLANES, LANES), F32)],
        compiler_params=_cparams(("arbitrary",), 32),
        name="route",
    )(logits, tri)


def _prep_up_kernel(w_ref, even_ref, odd_ref, g_ref, l_ref):
    w = w_ref[...].astype(BF16)
    g_ref[...] = _dot(w, even_ref[...]).astype(g_ref.dtype)
    l_ref[...] = _dot(w, odd_ref[...]).astype(l_ref.dtype)


def prepare_up_weights(w_up):
    d, f2 = w_up.shape[-2:]
    w3 = w_up.reshape(-1, d, f2)
    n = w3.shape[0]
    tn = min(512, f2)
    half = tn // 2
    even = np.zeros((tn, half), np.float32)
    odd = np.zeros((tn, half), np.float32)
    even[2 * np.arange(half), np.arange(half)] = 1.0
    odd[2 * np.arange(half) + 1, np.arange(half)] = 1.0
    sel = pl.BlockSpec((tn, half), lambda e, j: (0, 0))
    out_spec = pl.BlockSpec((None, d, half), lambda e, j: (e, 0, j))
    out_shape = jax.ShapeDtypeStruct((n, d, f2 // 2), BF16)
    return pl.pallas_call(
        _prep_up_kernel,
        grid=(n, f2 // tn),
        in_specs=[pl.BlockSpec((None, d, tn), lambda e, j: (e, 0, j)), sel, sel],
        out_specs=[out_spec, out_spec],
        out_shape=[out_shape, out_shape],
        compiler_params=_cparams(("arbitrary", "arbitrary"), 32),
        name="prepare_up_weights",
    )(w3, jnp.asarray(even, BF16), jnp.asarray(odd, BF16))


def _expert_kernel(be_ref, tok_cur_ref, tok_next_ref, h_ref, wg_ref, wl_ref, bg_ref, bl_ref, wd_ref, bd_ref,
                   y_ref, xbuf, sem, *, row_block):
    i = pl.program_id(0)
    n_blocks = pl.num_programs(0)
    slot = i % 2

    def start_gather(tok_ref, dst_slot):
        def issue(r, carry):
            pltpu.make_async_copy(h_ref.at[pl.ds(tok_ref[0, r], 1), :], xbuf.at[dst_slot, pl.ds(r, 1), :],
                                  sem.at[dst_slot]).start()
            return carry
        lax.fori_loop(0, row_block, issue, 0)

    @pl.when(i == 0)
    def _():
        start_gather(tok_cur_ref, 0)

    @pl.when(i + 1 < n_blocks)
    def _():
        start_gather(tok_next_ref, 1 - slot)

    pltpu.make_async_copy(h_ref.at[pl.ds(0, row_block), :], xbuf.at[slot], sem.at[slot]).wait()
    x = xbuf[slot].astype(BF16)
    glu = jnp.minimum(_dot(x, wg_ref[...]) + bg_ref[...], SWIGLU_LIMIT)
    lin = jnp.clip(_dot(x, wl_ref[...]) + bl_ref[...], -SWIGLU_LIMIT, SWIGLU_LIMIT)
    act = glu * jax.nn.sigmoid(SWIGLU_ALPHA * glu) * (lin + 1.0)
    y_ref[...] = _dot(act.astype(BF16), wd_ref[...]) + bd_ref[...]


def expert_ffn(h2, row_tok, block_e, w_glu, w_lin, up_offset, b_glu, b_lin, w_down, b_down, row_block):
    d = h2.shape[1]
    n_rows = row_tok.shape[0]
    n_blocks = n_rows // row_block
    n_exp = w_down.shape[0]
    f = w_glu.shape[2]
    once = pl.Buffered(1)
    tok3 = row_tok.reshape(n_blocks, 1, row_block)
    grid_spec = pltpu.PrefetchScalarGridSpec(
        num_scalar_prefetch=1,
        grid=(n_blocks,),
        in_specs=[pl.BlockSpec((None, 1, row_block), lambda i, be: (i, 0, 0), memory_space=pltpu.SMEM),
                  pl.BlockSpec((None, 1, row_block), lambda i, be: (jnp.minimum(i + 1, n_blocks - 1), 0, 0),
                               memory_space=pltpu.SMEM),
                  pl.BlockSpec(memory_space=pl.ANY),
                  pl.BlockSpec((None, d, f), lambda i, be: (be[i] + up_offset, 0, 0), pipeline_mode=once),
                  pl.BlockSpec((None, d, f), lambda i, be: (be[i] + up_offset, 0, 0), pipeline_mode=once),
                  pl.BlockSpec((None, 1, f), lambda i, be: (be[i], 0, 0)),
                  pl.BlockSpec((None, 1, f), lambda i, be: (be[i], 0, 0)),
                  pl.BlockSpec((None, f, d), lambda i, be: (be[i], 0, 0), pipeline_mode=once),
                  pl.BlockSpec((None, 1, d), lambda i, be: (be[i], 0, 0))],
        out_specs=pl.BlockSpec((row_block, d), lambda i, be: (i, 0)),
        scratch_shapes=[pltpu.VMEM((2, row_block, d), F32), pltpu.SemaphoreType.DMA((2,))])
    return pl.pallas_call(
        functools.partial(_expert_kernel, row_block=row_block),
        grid_spec=grid_spec,
        out_shape=jax.ShapeDtypeStruct((n_rows, d), F32),
        compiler_params=_cparams(("arbitrary",)),
        name="expert_ffn",
    )(block_e, tok3, tok3, h2, w_glu, w_lin, b_glu.reshape(n_exp, 1, f), b_lin.reshape(n_exp, 1, f),
      w_down, b_down.reshape(n_exp, 1, d))


def _combine_kernel(dest_ref, ys_ref, w_ref, x_ref, gate_ref, g_ref, b_ref, sc_ref, sh_ref,
                    x2_ref, h_ref, buf, sem, *, tc, alpha):
    def row_copy(tok, k, dest):
        return pltpu.make_async_copy(ys_ref.at[pl.ds(dest, 1), :], buf.at[k, pl.ds(tok, 1), :], sem)

    def issue(tok, carry):
        for k in range(TOP_K):
            row_copy(tok, k, dest_ref[0, tok * TOP_K + k]).start()
        return carry

    lax.fori_loop(0, tc, issue, 0)
    for k in range(TOP_K):
        pltpu.make_async_copy(ys_ref.at[pl.ds(0, tc), :], buf.at[k], sem).wait()
    w = w_ref[...]
    y = w[:, 0:1] * buf[0]
    for k in range(1, TOP_K):
        y = y + w[:, k:k + 1] * buf[k]
    x2 = _ln(alpha * x_ref[...] + gate_ref[...] * y) * g_ref[...] + b_ref[...]
    x2_ref[...] = x2
    h_ref[...] = (_ln(x2) * (1.0 + sc_ref[...]) + sh_ref[...]).astype(h_ref.dtype)


def moe_combine(ys, dest, top_w, x1, mod_l, mod_next, ln_g, ln_b, seq, alpha):
    t, d = x1.shape
    tc = min(128, seq)
    bpb = seq // tc
    row = pl.BlockSpec((tc, d), lambda i: (i, 0))
    vec = pl.BlockSpec((1, d), lambda i: (0, 0))
    return pl.pallas_call(
        functools.partial(_combine_kernel, tc=tc, alpha=alpha),
        grid=(t // tc,),
        in_specs=[pl.BlockSpec((None, 1, tc * TOP_K), lambda i: (i, 0, 0), memory_space=pltpu.SMEM),
                  pl.BlockSpec(memory_space=pl.ANY),
                  pl.BlockSpec((tc, LANES), lambda i: (i, 0)),
                  row, _mod_spec(5, d, bpb), vec, vec, _mod_spec(1, d, bpb), _mod_spec(0, d, bpb)],
        out_specs=[row, row],
        out_shape=[jax.ShapeDtypeStruct((t, d), F32), jax.ShapeDtypeStruct((t, d), BF16)],
        scratch_shapes=[pltpu.VMEM((TOP_K, tc, d), F32), pltpu.SemaphoreType.DMA(())],
        compiler_params=_cparams(("arbitrary",), 32),
        name="moe_combine",
    )(dest.reshape(t // tc, 1, tc * TOP_K), ys, top_w, x1, mod_l, ln_g.reshape(1, d), ln_b.reshape(1, d),
      mod_next, mod_next)


def moe_ffn(h2, logits, w_glu, w_lin, up_offset, b_glu, b_lin, w_down, b_down):
    t, d = h2.shape
    n_exp = w_down.shape[0]
    row_block = 256
    idx_rank, top_w, counts = route(logits, n_exp)
    top_idx = idx_rank[:, :TOP_K]
    rank = idx_rank[:, TOP_K:2 * TOP_K]
    counts = counts[0, :n_exp].astype(jnp.int32)
    padded = (counts + row_block - 1) // row_block * row_block
    padded_end = jnp.cumsum(padded)
    padded_start = padded_end - padded
    dest = padded_start[top_idx] + rank
    n_rows = t * TOP_K + n_exp * row_block
    tok_ids = jnp.broadcast_to(jnp.arange(t, dtype=jnp.int32)[:, None], (t, TOP_K))
    row_tok = jnp.zeros((n_rows,), jnp.int32).at[dest.reshape(-1)].set(tok_ids.reshape(-1))
    n_blocks = n_rows // row_block
    block_e = jnp.minimum(
        jnp.searchsorted(padded_end, jnp.arange(n_blocks, dtype=jnp.int32) * row_block, side='right'),
        n_exp - 1).astype(jnp.int32)
    ys = expert_ffn(h2, row_tok, block_e, w_glu, w_lin, up_offset, b_glu, b_lin, w_down, b_down, row_block)
    return ys, dest.astype(jnp.int32), top_w


def kernel(x, c, w_ada, b_ada, w_in, b_forget, w_gate, b_gate, w_proj_sb, w_proj_dil, w_proj_fox, w_out,
           ln1_g, ln1_b, w_router, b_router, w_up, b_up, w_down, b_down, ln2_g, ln2_b):
    batch, seq, d = x.shape
    depth = w_ada.shape[0]
    t = batch * seq
    alpha = (2.0 * depth) ** 0.25
    rope_tabs = rope_pair_tables(seq)
    mod = adaln_mod(c, w_ada, b_ada)
    x2d = x.reshape(t, d)
    h = ln_mod(x2d, mod[0], seq, comp_shift=0, comp_scale=1)
    n_exp = w_up.shape[1]
    w_glu, w_lin = prepare_up_weights(w_up)
    for l in range(depth):
        w_in_l = w_in[l]
        qkv = in_projection(h, w_in_l[:, :N_QKV].astype(BF16), rope_tabs, seq)
        q_feat, k_feat = forget_features(h, w_in_l[:, N_QKV:], b_forget[l], batch, seq)
        o_sb = stick_breaking_attention(qkv, batch, seq)
        o_dl = dilated_window_attention(qkv, batch, seq)
        o_fx = forgetting_attention(qkv, q_feat, k_feat, batch, seq)
        merged = gated_merge(h, w_gate[l].astype(BF16), b_gate[l], (o_sb, o_dl, o_fx),
                             (w_proj_sb[l].astype(BF16), w_proj_dil[l].astype(BF16),
                              w_proj_fox[l].astype(BF16)))
        x1, h2, logits = out_projection(merged, w_out[l].astype(BF16), x2d, mod[l], ln1_g[l], ln1_b[l],
                                        w_router[l], b_router[l], seq, alpha)
        ys, dest, top_w = moe_ffn(h2, logits, w_glu, w_lin, l * n_exp,
                                  b_up[l][:, 0::2], b_up[l][:, 1::2], w_down[l].astype(BF16), b_down[l])
        mod_next = mod[min(l + 1, depth - 1)]
        x2d, h = moe_combine(ys, dest, top_w, x1, mod[l], mod_next, ln2_g[l], ln2_b[l], seq, alpha)
    return x2d.reshape(batch, seq, d)
```

```python
import functools

import numpy as np
import jax
import jax.numpy as jnp
from jax import lax
from jax.experimental import pallas as pl
from jax.experimental.pallas import tpu as pltpu

F32 = jnp.float32
BF16 = jnp.bfloat16

HEAD_DIM = 64
H_SB = 8
DIL_CONFIGS = ((128, 1), (512, 4), (2048, 16))
H_DIL_PER_GROUP = 4
H_FOX = 12
ROPE_THETA = 500000.0
ROPE_DIMS = HEAD_DIM // 4
TOP_K = 4
SWIGLU_ALPHA = 1.702
SWIGLU_LIMIT = 7.0
LN_EPS = 1e-5
QK_SCALE = HEAD_DIM ** -0.5

LANES = 128
SUBLANES = 8
VMEM_BUDGET_MB = 56

N_DIL_GROUPS = len(DIL_CONFIGS)
W_SB = H_SB * HEAD_DIM
W_DIL = N_DIL_GROUPS * H_DIL_PER_GROUP * HEAD_DIM
W_DIL_OUT = H_DIL_PER_GROUP * HEAD_DIM
W_FOX = H_FOX * HEAD_DIM
N_QKV = 3 * (W_SB + W_DIL + W_FOX)
COL_SB = (0, W_SB // LANES, 2 * W_SB // LANES)
_DL0 = 3 * W_SB // LANES
COL_DL = (_DL0, _DL0 + W_DIL // LANES, _DL0 + 2 * W_DIL // LANES)
_FX0 = _DL0 + 3 * W_DIL // LANES
COL_FX = (_FX0, _FX0 + W_FOX // LANES, _FX0 + 2 * W_FOX // LANES)
BAND = 128
NEG_BIG = -1e30


def _cparams(semantics, vmem_mb=VMEM_BUDGET_MB):
    return pltpu.CompilerParams(dimension_semantics=semantics, vmem_limit_bytes=vmem_mb * 2 ** 20)


def _dot(a, b):
    return jnp.dot(a, b, preferred_element_type=F32)


def _dot_t(a, b):
    return lax.dot_general(a, b, (((1,), (1,)), ((), ())), preferred_element_type=F32)


def _ln(x):
    mu = jnp.mean(x, axis=-1, keepdims=True)
    xc = x - mu
    var = jnp.mean(xc * xc, axis=-1, keepdims=True)
    return xc * lax.rsqrt(var + LN_EPS)


def _softplus_neg_abs(z):
    return jnp.log1p(jnp.exp(-jnp.abs(z)))


def _split3(x):
    hi = x.astype(BF16)
    r1 = x - hi.astype(F32)
    mid = r1.astype(BF16)
    lo = (r1 - mid.astype(F32)).astype(BF16)
    return hi, mid, lo


def _mod_kernel(c_ref, w_ref, b_ref, o_ref):
    c = c_ref[...]
    act = (c * jax.nn.sigmoid(c)).astype(BF16)
    o_ref[...] = _dot(act, w_ref[...].astype(BF16)) + b_ref[...]


def adaln_mod(c, w_ada, b_ada):
    n_layers, d, n = w_ada.shape
    b = c.shape[0]
    assert b <= SUBLANES
    tn = 512
    c_pad = jnp.zeros((SUBLANES, d), F32).at[:b].set(c)
    out = pl.pallas_call(
        _mod_kernel,
        grid=(n_layers, n // tn),
        in_specs=[pl.BlockSpec((SUBLANES, d), lambda l, j: (0, 0)),
                  pl.BlockSpec((None, d, tn), lambda l, j: (l, 0, j)),
                  pl.BlockSpec((None, 1, tn), lambda l, j: (l, 0, j))],
        out_specs=pl.BlockSpec((None, SUBLANES, tn), lambda l, j: (l, 0, j)),
        out_shape=jax.ShapeDtypeStruct((n_layers, SUBLANES, n), F32),
        compiler_params=_cparams(("arbitrary", "arbitrary"), 32),
        name="adaln_mod",
    )(c_pad, w_ada, b_ada.reshape(n_layers, 1, n))
    return out[:, :b].reshape(n_layers, b, 6, d).transpose(0, 2, 1, 3)[:, :, :, None, :]


def _mod_spec(comp, d, rows_per_batch_blocks):
    return pl.BlockSpec((None, None, 1, d), lambda i, *_: (comp, i // rows_per_batch_blocks, 0, 0))


def _ln_mod_kernel(x_ref, sc_ref, sh_ref, h_ref):
    h_ref[...] = (_ln(x_ref[...]) * (1.0 + sc_ref[...]) + sh_ref[...]).astype(h_ref.dtype)


def ln_mod(x2d, mod_l, seq, comp_shift, comp_scale):
    t, d = x2d.shape
    ts = min(512, seq)
    bpb = seq // ts
    return pl.pallas_call(
        _ln_mod_kernel,
        grid=(t // ts,),
        in_specs=[pl.BlockSpec((ts, d), lambda i: (i, 0)),
                  _mod_spec(comp_scale, d, bpb), _mod_spec(comp_shift, d, bpb)],
        out_specs=pl.BlockSpec((ts, d), lambda i: (i, 0)),
        out_shape=jax.ShapeDtypeStruct((t, d), BF16),
        compiler_params=_cparams(("arbitrary",), 32),
        name="ln_mod",
    )(x2d, mod_l, mod_l)


def _inproj_kernel(h_ref, w_ref, cos_ref, s1_ref, s2_ref, o_ref, *, rope_lo, rope_hi, tn):
    j = pl.program_id(1)
    acc = _dot(h_ref[...], w_ref[...])
    is_rope = jnp.logical_and(j >= rope_lo, j < rope_hi)

    @pl.when(is_rope)
    def _():
        c, s1, s2 = cos_ref[...], s1_ref[...], s2_ref[...]
        for blk in range(tn // LANES):
            a = acc[:, blk * LANES:(blk + 1) * LANES]
            r = a * c + pltpu.roll(a, LANES - ROPE_DIMS // 2, 1) * s1 + pltpu.roll(a, ROPE_DIMS // 2, 1) * s2
            o_ref[:, blk * LANES:(blk + 1) * LANES] = r.astype(o_ref.dtype)

    @pl.when(jnp.logical_not(is_rope))
    def _():
        o_ref[...] = acc.astype(o_ref.dtype)


def rope_pair_tables(seq):
    pos = jnp.arange(seq, dtype=F32)
    inv = ROPE_THETA ** (-jnp.arange(0, ROPE_DIMS, 2, dtype=F32) / ROPE_DIMS)
    ang = pos[:, None] * inv[None, :]
    cos, sin = jnp.cos(ang), jnp.sin(ang)
    half = ROPE_DIMS // 2
    head_c = jnp.concatenate([cos, cos, jnp.ones((seq, HEAD_DIM - ROPE_DIMS), F32)], axis=1)
    head_s1 = jnp.concatenate([-sin, jnp.zeros((seq, HEAD_DIM - half), F32)], axis=1)
    head_s2 = jnp.concatenate([jnp.zeros((seq, half), F32), sin,
                               jnp.zeros((seq, HEAD_DIM - ROPE_DIMS), F32)], axis=1)
    two = lambda a: jnp.concatenate([a, a], axis=1)
    return two(head_c), two(head_s1), two(head_s2)


def in_projection(h, w_qkv, rope_tabs, seq):
    t, d = h.shape
    n = w_qkv.shape[1]
    tm = min(1024, seq)
    tn = 512
    assert seq % tm == 0 and n % tn == 0
    rope_lo, rope_hi = COL_DL[0] * LANES, COL_DL[2] * LANES
    assert rope_lo % tn == 0 and rope_hi % tn == 0
    spb = seq // tm
    tab_spec = pl.BlockSpec((tm, LANES), lambda i, j: (i % spb, 0))
    return pl.pallas_call(
        functools.partial(_inproj_kernel, rope_lo=rope_lo // tn, rope_hi=rope_hi // tn, tn=tn),
        grid=(t // tm, n // tn),
        in_specs=[pl.BlockSpec((tm, d), lambda i, j: (i, 0)),
                  pl.BlockSpec((d, tn), lambda i, j: (0, j)),
                  tab_spec, tab_spec, tab_spec],
        out_specs=pl.BlockSpec((tm, tn), lambda i, j: (i, j)),
        out_shape=jax.ShapeDtypeStruct((t, n), BF16),
        compiler_params=_cparams(("arbitrary", "arbitrary"), 48),
        name="in_projection",
    )(h, w_qkv, *rope_tabs)


FEAT_PER_HEAD = 6


def _forget_feature_maps():
    width = (H_FOX // 2) * LANES
    pq = np.zeros((3 * LANES, width), np.float32)
    pk = np.zeros((3 * LANES, width), np.float32)
    cq = np.zeros((1, width), np.float32)
    ck = np.zeros((1, width), np.float32)
    for head in range(H_FOX):
        base = (head // 2) * LANES + (head % 2) * FEAT_PER_HEAD
        for piece in range(3):
            pq[piece * LANES + head, base + piece] = 1.0
            pk[piece * LANES + head, base + 3 + piece] = -1.0
        cq[0, base + 3:base + 6] = 1.0
        ck[0, base:base + 3] = 1.0
    return pq, pk, cq, ck


def _forget_kernel(h_ref, w_ref, b_ref, tri_ref, pq_ref, pk_ref, cq_ref, ck_ref, qf_ref, kf_ref, carry_ref):
    @pl.when(pl.program_id(1) == 0)
    def _():
        carry_ref[...] = jnp.zeros_like(carry_ref)

    f = _dot(h_ref[...], w_ref[...]) + b_ref[...]
    log_f = jnp.minimum(f, 0.0) - _softplus_neg_abs(f)
    tri = tri_ref[...]
    hi, mid, lo = _split3(log_f)
    cum = _dot(tri, hi) + _dot(tri, mid) + _dot(tri, lo) + carry_ref[0:1, :]
    carry_ref[...] = jnp.broadcast_to(cum[-1:, :], carry_ref.shape)
    pieces = jnp.concatenate(_split3(cum), axis=1)
    qf_ref[...] = (_dot(pieces, pq_ref[...]) + cq_ref[...]).astype(qf_ref.dtype)
    kf_ref[...] = (_dot(pieces, pk_ref[...]) + ck_ref[...]).astype(kf_ref.dtype)


def forget_features(h, w_f, b_f, batch, seq):
    t, d = h.shape
    ts = min(256, seq)
    nsb = seq // ts
    w_pad = jnp.zeros((d, LANES), BF16).at[:, :H_FOX].set(w_f.astype(BF16))
    b_pad = jnp.zeros((1, LANES), F32).at[0, :H_FOX].set(b_f)
    tri = (np.arange(ts)[:, None] >= np.arange(ts)[None, :]).astype(np.float32)
    pq, pk, cq, ck = _forget_feature_maps()
    width = pq.shape[1]
    const = lambda shape: pl.BlockSpec(shape, lambda b, s: (0, 0))
    out_spec = pl.BlockSpec((ts, width), lambda b, s: (b * nsb + s, 0))
    out_shape = jax.ShapeDtypeStruct((t, width), BF16)
    return pl.pallas_call(
        _forget_kernel,
        grid=(batch, nsb),
        in_specs=[pl.BlockSpec((ts, d), lambda b, s: (b * nsb + s, 0)),
                  const((d, LANES)), const((1, LANES)), const((ts, ts)),
                  const(pq.shape), const(pk.shape), const(cq.shape), const(ck.shape)],
        out_specs=[out_spec, out_spec],
        out_shape=[out_shape, out_shape],
        scratch_shapes=[pltpu.VMEM((SUBLANES, LANES), F32)],
        compiler_params=_cparams(("arbitrary", "arbitrary"), 32),
        name="forget_features",
    )(h, w_pad, b_pad, jnp.asarray(tri, BF16), jnp.asarray(pq, BF16), jnp.asarray(pk, BF16),
      jnp.asarray(cq), jnp.asarray(ck))


FLAG_FIRST, FLAG_LAST, FLAG_MASKED = 1, 2, 4
FLAG_OFFSET_SHIFT = 3


def _causal_schedule(seq, tq, tk, strict, reverse):
    qi, kb, fl = [], [], []
    for i in range(seq // tq):
        q_lo, q_hi = i * tq, (i + 1) * tq - 1
        k_max = q_hi - 1 if strict else q_hi
        blocks = list(range(max(k_max, 0) // tk + 1))
        if reverse:
            blocks = blocks[::-1]
        for n, kblk in enumerate(blocks):
            k_hi = kblk * tk + tk - 1
            masked = (k_hi >= q_lo) if strict else (k_hi > q_lo)
            qi.append(i)
            kb.append(kblk)
            offset = (kblk * tk - q_lo) // tk if masked else 0
            assert offset >= 0
            fl.append((FLAG_FIRST if n == 0 else 0) | (FLAG_LAST if n == len(blocks) - 1 else 0)
                      | (FLAG_MASKED if masked else 0) | (offset << FLAG_OFFSET_SHIFT))
    as_i32 = lambda a: jnp.asarray(np.asarray(a, np.int32))
    return as_i32(qi), as_i32(kb), as_i32(fl)


def _pair_masks(tq):
    lane = lax.broadcasted_iota(jnp.int32, (tq, LANES), 1)
    return lane < HEAD_DIM


KEY_SUB = 128
QRY_SUB = 128
SUFFIX_SUB = 256


def _diag_tile_kind(ks, qs, key_sub, qry_sub, strict, key_off):
    if key_off is None:
        return 'full'
    k_lo, k_hi = key_off + ks * key_sub, key_off + ks * key_sub + key_sub - 1
    q_lo, q_hi = qs * qry_sub, qs * qry_sub + qry_sub - 1
    if strict:
        if k_lo >= q_hi:
            return 'dead'
        return 'full' if k_hi < q_lo else 'partial'
    if k_lo > q_hi:
        return 'dead'
    return 'full' if k_hi <= q_lo else 'partial'


def _tile_allowed(ks, qs, key_sub, qry_sub, strict, key_off):
    kpos = key_off + ks * key_sub + lax.broadcasted_iota(jnp.int32, (key_sub, qry_sub), 0)
    qpos = qs * qry_sub + lax.broadcasted_iota(jnp.int32, (key_sub, qry_sub), 1)
    return kpos < qpos if strict else kpos <= qpos


def _pair_transpose_in(qkv3, col, width):
    return lax.slice_in_dim(qkv3, col * LANES, col * LANES + width, axis=2).transpose(0, 2, 1)


def _sb_kernel(qi_ref, kb_ref, fl_ref, q_ref, k_ref, vt_ref, later_ref, o_ref,
               qx_s, acc_s, run_s, first_s, z_s, hi_s, lo_s, btw_s, w_s, *, tq, tk, key_sub, qry_sub):
    s = pl.program_id(2)
    flags = fl_ref[s]

    @pl.when((flags & FLAG_FIRST) != 0)
    def _():
        head_a = _pair_masks(tq)
        q = q_ref[...] * QK_SCALE
        qx_s[0] = jnp.where(head_a, q, 0).astype(BF16)
        qx_s[1] = jnp.where(head_a, 0, q).astype(BF16)
        acc_s[...] = jnp.zeros_like(acc_s)
        run_s[...] = jnp.zeros_like(run_s)

    def step(key_off):
        suf = later_ref.shape[0]
        n_suf = tk // suf
        per_suf = suf // key_sub
        tiles = [(ks, qs) for ks in range(tk // key_sub) for qs in range(tq // qry_sub)]
        kind_of = lambda ks, qs: _diag_tile_kind(ks, qs, key_sub, qry_sub, True, key_off)
        k = k_ref[...]
        for head in range(2):
            z_s[head] = _dot_t(k, qx_s[head])
        for head in range(2):
            for ks, qs in tiles:
                kl = slice(ks * key_sub, (ks + 1) * key_sub)
                ql = slice(qs * qry_sub, (qs + 1) * qry_sub)
                first_row = head * n_suf + ks // per_suf
                kind = kind_of(ks, qs)
                if kind == 'dead':
                    hi_s[head, kl, ql] = jnp.zeros((key_sub, qry_sub), BF16)
                    lo_s[head, kl, ql] = jnp.zeros((key_sub, qry_sub), BF16)
                    if ks % per_suf == 0:
                        first_s[first_row:first_row + 1, ql] = jnp.zeros((1, qry_sub), F32)
                    continue
                z = z_s[head, kl, ql]
                log_stop = jnp.minimum(z, 0.0) - jnp.log(1.0 + jnp.exp(-jnp.abs(z)))
                log_cont = log_stop - z
                if kind == 'partial':
                    log_cont = jnp.where(_tile_allowed(ks, qs, key_sub, qry_sub, True, key_off), log_cont, 0.0)
                z_s[head, kl, ql] = log_stop
                hi = log_cont.astype(BF16)
                hi_s[head, kl, ql] = hi
                lo_s[head, kl, ql] = (log_cont - hi.astype(F32)).astype(BF16)
                if ks % per_suf == 0:
                    first_s[first_row:first_row + 1, ql] = log_cont[0:1, :]
        later = later_ref[...]
        for head in range(2):
            for blk in range(n_suf):
                bl = slice(blk * suf, (blk + 1) * suf)
                btw_s[head, bl, :] = _dot(later, hi_s[head, bl, :]) + _dot(later, lo_s[head, bl, :])

        def block_sum(head, blk, ql):
            row = head * n_suf + blk
            return btw_s[head, blk * suf:blk * suf + 1, ql] + first_s[row:row + 1, ql]

        for head in range(2):
            for ks, qs in tiles:
                kl = slice(ks * key_sub, (ks + 1) * key_sub)
                ql = slice(qs * qry_sub, (qs + 1) * qry_sub)
                kind = kind_of(ks, qs)
                if kind == 'dead':
                    w_s[head, kl, ql] = jnp.zeros((key_sub, qry_sub), BF16)
                    continue
                after = run_s[head:head + 1, ql]
                for blk in range(ks // per_suf + 1, n_suf):
                    after = after + block_sum(head, blk, ql)
                w = jnp.exp(z_s[head, kl, ql] + btw_s[head, kl, ql] + after)
                if kind == 'partial':
                    w = jnp.where(_tile_allowed(ks, qs, key_sub, qry_sub, True, key_off), w, 0.0)
                w_s[head, kl, ql] = w.astype(BF16)
        for head in range(2):
            rows = slice(head * HEAD_DIM, (head + 1) * HEAD_DIM)
            acc_s[rows, :] += _dot(vt_ref[rows, :], w_s[head])
            total = run_s[head:head + 1, :]
            for blk in range(n_suf):
                total = total + block_sum(head, blk, slice(None))
            run_s[head:head + 1, :] = total

    masked = (flags & FLAG_MASKED) != 0
    for variant in range(tq // tk):
        @pl.when(jnp.logical_and(masked, (flags >> FLAG_OFFSET_SHIFT) == variant))
        def _(variant=variant):
            step(variant * tk)

    @pl.when(jnp.logical_not(masked))
    def _():
        step(None)

    @pl.when((flags & FLAG_LAST) != 0)
    def _():
        o_ref[...] = acc_s[...].astype(o_ref.dtype)


def stick_breaking_attention(qkv, batch, seq):
    tk = min(512, seq)
    tq = min(1024, seq)
    key_sub, qry_sub = min(KEY_SUB, tk), min(QRY_SUB, tq)
    sched = _causal_schedule(seq, tq, tk, strict=True, reverse=True)
    n_steps = int(sched[0].shape[0])
    n_pairs = W_SB // LANES
    qkv3 = qkv.reshape(batch, seq, qkv.shape[-1])
    qc, kc, vc = COL_SB
    v_t = _pair_transpose_in(qkv3, vc, W_SB)
    suf = min(SUFFIX_SUB, tk)
    assert 2 * (tk // suf) <= SUBLANES and suf % key_sub == 0
    later = jnp.asarray((np.arange(suf)[None, :] > np.arange(suf)[:, None]).astype(np.float32), BF16)
    block_f32 = pltpu.VMEM((2, tk, tq), F32)
    block_bf16 = pltpu.VMEM((2, tk, tq), BF16)
    row_stats = pltpu.VMEM((SUBLANES, tq), F32)
    grid_spec = pltpu.PrefetchScalarGridSpec(
        num_scalar_prefetch=3,
        grid=(batch, n_pairs, n_steps),
        in_specs=[pl.BlockSpec((None, tq, LANES), lambda b, p, s, qi, kb, fl: (b, qi[s], qc + p)),
                  pl.BlockSpec((None, tk, LANES), lambda b, p, s, qi, kb, fl: (b, kb[s], kc + p)),
                  pl.BlockSpec((None, LANES, tk), lambda b, p, s, qi, kb, fl: (b, p, kb[s])),
                  pl.BlockSpec((suf, suf), lambda b, p, s, qi, kb, fl: (0, 0))],
        out_specs=pl.BlockSpec((None, LANES, tq), lambda b, p, s, qi, kb, fl: (b, p, qi[s])),
        scratch_shapes=[pltpu.VMEM((2, tq, LANES), BF16), pltpu.VMEM((LANES, tq), F32),
                        row_stats, row_stats, block_f32, block_bf16, block_bf16, block_f32, block_bf16])
    out_t = pl.pallas_call(
        functools.partial(_sb_kernel, tq=tq, tk=tk, key_sub=key_sub, qry_sub=qry_sub),
        grid_spec=grid_spec,
        out_shape=jax.ShapeDtypeStruct((batch, W_SB, seq), BF16),
        compiler_params=_cparams(("arbitrary", "arbitrary", "arbitrary"), 32),
        name="stick_breaking_attention",
    )(*sched, qkv3, qkv3, v_t, later)
    return out_t.transpose(0, 2, 1).reshape(batch * seq, W_SB)


def _fox_kernel(qi_ref, kb_ref, fl_ref, q_ref, qf_ref, k_ref, kf_ref, vt_ref, o_ref,
                qx_s, acc_s, m_s, l_s, alpha_s, z_s, p_s, *, tq, tk, key_sub, qry_sub):
    s = pl.program_id(2)
    flags = fl_ref[s]

    @pl.when((flags & FLAG_FIRST) != 0)
    def _():
        lane = lax.broadcasted_iota(jnp.int32, (tq, LANES), 1)
        q = q_ref[...] * QK_SCALE
        qf = qf_ref[...]
        for head in range(2):
            in_head = (lane >= head * HEAD_DIM) & (lane < (head + 1) * HEAD_DIM)
            in_feat = (lane >= head * FEAT_PER_HEAD) & (lane < (head + 1) * FEAT_PER_HEAD)
            qx_s[head] = jnp.concatenate([jnp.where(in_head, q, 0).astype(BF16),
                                          jnp.where(in_feat, qf, 0).astype(BF16)], axis=1)
        acc_s[...] = jnp.zeros_like(acc_s)
        l_s[...] = jnp.zeros_like(l_s)
        m_s[...] = jnp.full_like(m_s, NEG_BIG)

    def step(key_off):
        kind_of = lambda ks, qs: _diag_tile_kind(ks, qs, key_sub, qry_sub, False, key_off)
        kx = jnp.concatenate([k_ref[...], kf_ref[...]], axis=1)
        for head in range(2):
            z_s[head] = _dot_t(kx, qx_s[head])
        for head in range(2):
            for qs in range(tq // qry_sub):
                ql = slice(qs * qry_sub, (qs + 1) * qry_sub)

                def score(ks):
                    z = z_s[head, ks * key_sub:(ks + 1) * key_sub, ql]
                    if kind_of(ks, qs) == 'partial':
                        z = jnp.where(_tile_allowed(ks, qs, key_sub, qry_sub, False, key_off), z, NEG_BIG)
                    return z

                live = [ks for ks in range(tk // key_sub) if kind_of(ks, qs) != 'dead']
                m_old = m_s[head:head + 1, ql]
                m_new = m_old
                for ks in live:
                    m_new = jnp.maximum(m_new, jnp.max(score(ks), axis=0, keepdims=True))
                alpha = jnp.exp(m_old - m_new)
                total = alpha * l_s[head:head + 1, ql]
                for ks in range(tk // key_sub):
                    kl = slice(ks * key_sub, (ks + 1) * key_sub)
                    if ks not in live:
                        p_s[head, kl, ql] = jnp.zeros((key_sub, qry_sub), BF16)
                        continue
                    prob = jnp.exp(score(ks) - m_new)
                    total = total + jnp.sum(prob, axis=0, keepdims=True)
                    p_s[head, kl, ql] = prob.astype(BF16)
                m_s[head:head + 1, ql] = m_new
                l_s[head:head + 1, ql] = total
                alpha_s[head:head + 1, ql] = alpha
        for head in range(2):
            rows = slice(head * HEAD_DIM, (head + 1) * HEAD_DIM)
            acc_s[rows, :] = alpha_s[head:head + 1, :] * acc_s[rows, :] + _dot(vt_ref[rows, :], p_s[head])

    masked = (flags & FLAG_MASKED) != 0
    for variant in range(tq // tk):
        @pl.when(jnp.logical_and(masked, (flags >> FLAG_OFFSET_SHIFT) == variant))
        def _(variant=variant):
            step(variant * tk)

    @pl.when(jnp.logical_not(masked))
    def _():
        step(None)

    @pl.when((flags & FLAG_LAST) != 0)
    def _():
        for head in range(2):
            rows = slice(head * HEAD_DIM, (head + 1) * HEAD_DIM)
            o_ref[rows, :] = (acc_s[rows, :] / l_s[head:head + 1, :]).astype(o_ref.dtype)


def forgetting_attention(qkv, q_feat, k_feat, batch, seq):
    tk = min(512, seq)
    tq = min(1024, seq)
    key_sub, qry_sub = min(KEY_SUB, tk), min(QRY_SUB, tq)
    sched = _causal_schedule(seq, tq, tk, strict=False, reverse=False)
    n_steps = int(sched[0].shape[0])
    n_pairs = W_FOX // LANES
    qkv3 = qkv.reshape(batch, seq, qkv.shape[-1])
    qf3 = q_feat.reshape(batch, seq, n_pairs * LANES)
    kf3 = k_feat.reshape(batch, seq, n_pairs * LANES)
    qc, kc, vc = COL_FX
    v_t = _pair_transpose_in(qkv3, vc, W_FOX)
    grid_spec = pltpu.PrefetchScalarGridSpec(
        num_scalar_prefetch=3,
        grid=(batch, n_pairs, n_steps),
        in_specs=[pl.BlockSpec((None, tq, LANES), lambda b, p, s, qi, kb, fl: (b, qi[s], qc + p)),
                  pl.BlockSpec((None, tq, LANES), lambda b, p, s, qi, kb, fl: (b, qi[s], p)),
                  pl.BlockSpec((None, tk, LANES), lambda b, p, s, qi, kb, fl: (b, kb[s], kc + p)),
                  pl.BlockSpec((None, tk, LANES), lambda b, p, s, qi, kb, fl: (b, kb[s], p)),
                  pl.BlockSpec((None, LANES, tk), lambda b, p, s, qi, kb, fl: (b, p, kb[s]))],
        out_specs=pl.BlockSpec((None, LANES, tq), lambda b, p, s, qi, kb, fl: (b, p, qi[s])),
        scratch_shapes=[pltpu.VMEM((2, tq, 2 * LANES), BF16), pltpu.VMEM((LANES, tq), F32),
                        pltpu.VMEM((SUBLANES, tq), F32), pltpu.VMEM((SUBLANES, tq), F32),
                        pltpu.VMEM((SUBLANES, tq), F32), pltpu.VMEM((2, tk, tq), F32),
                        pltpu.VMEM((2, tk, tq), BF16)])
    out_t = pl.pallas_call(
        functools.partial(_fox_kernel, tq=tq, tk=tk, key_sub=key_sub, qry_sub=qry_sub),
        grid_spec=grid_spec,
        out_shape=jax.ShapeDtypeStruct((batch, W_FOX, seq), BF16),
        compiler_params=_cparams(("arbitrary", "arbitrary", "arbitrary"), 32),
        name="forgetting_attention",
    )(*sched, qkv3, qf3, qkv3, kf3, v_t)
    return out_t.transpose(0, 2, 1).reshape(batch * seq, W_FOX)


def _band_kernel(q_ref, kc_ref, vc_ref, kp_ref, vp_ref, o_ref, lse_ref, *, tq):
    i = pl.program_id(2)
    head_a = _pair_masks(tq)
    q = q_ref[...] * QK_SCALE
    kc, vc, kp, vp = kc_ref[...], vc_ref[...], kp_ref[...], vp_ref[...]
    diff = lax.broadcasted_iota(jnp.int32, (tq, tq), 0) - lax.broadcasted_iota(jnp.int32, (tq, tq), 1)
    mask_c = jnp.logical_and(diff >= 0, diff <= BAND)
    rp = lax.broadcasted_iota(jnp.int32, (tq, BAND), 0)
    cp = lax.broadcasted_iota(jnp.int32, (tq, BAND), 1)
    mask_p = jnp.logical_and(cp >= rp, i > 0)
    outs, lses = [], []
    for sel in (head_a, jnp.logical_not(head_a)):
        qh = jnp.where(sel, q, 0).astype(BF16)
        sc = jnp.where(mask_c, _dot_t(qh, kc), NEG_BIG)
        sp = jnp.where(mask_p, _dot_t(qh, kp), NEG_BIG)
        m = jnp.maximum(jnp.max(sc, axis=1, keepdims=True), jnp.max(sp, axis=1, keepdims=True))
        ec = jnp.exp(sc - m)
        ep = jnp.exp(sp - m)
        den = jnp.sum(ec, axis=1, keepdims=True) + jnp.sum(ep, axis=1, keepdims=True)
        outs.append((_dot(ec.astype(BF16), vc) + _dot(ep.astype(BF16), vp)) / den)
        lses.append(m + jnp.log(den))
    o_ref[...] = jnp.where(head_a, outs[0], outs[1])
    lse_ref[...] = jnp.where(head_a, lses[0], lses[1])


def _band_attention(q, k, v):
    nb, u, w = q.shape
    tq = min(256, u)
    assert u % tq == 0 and tq % BAND == 0
    sub = tq // BAND
    cur = pl.BlockSpec((None, tq, LANES), lambda n, p, i: (n, i, p))
    prev = pl.BlockSpec((None, BAND, LANES), lambda n, p, i: (n, jnp.maximum(i * sub - 1, 0), p))
    shp = jax.ShapeDtypeStruct((nb, u, w), F32)
    return pl.pallas_call(
        functools.partial(_band_kernel, tq=tq),
        grid=(nb, w // LANES, u // tq),
        in_specs=[cur, cur, cur, prev, prev],
        out_specs=[cur, cur],
        out_shape=[shp, shp],
        compiler_params=_cparams(("arbitrary", "arbitrary", "arbitrary"), 32),
        name="band_attention",
    )(q, k, v, k, v)


def _dil_mix_kernel(o0, o1, o2, l0, l1, l2, out_ref):
    a, b, c = l0[...], l1[...], l2[...]
    m = jnp.maximum(jnp.maximum(a, b), c)
    ea, eb, ec = jnp.exp(a - m), jnp.exp(b - m), jnp.exp(c - m)
    tot = ea + eb + ec
    out_ref[...] = ((o0[...] * ea + o1[...] * eb + o2[...] * ec) / tot).astype(out_ref.dtype)


def dilated_window_attention(qkv, batch, seq):
    t = batch * seq
    qkv3 = qkv.reshape(batch, seq, qkv.shape[-1])
    gw = W_DIL_OUT
    outs, lses = [], []
    for g, (window, dil) in enumerate(DIL_CONFIGS):
        assert window // dil == BAND and seq % (dil * BAND) == 0
        u = seq // dil

        def split(col):
            a = lax.slice_in_dim(qkv3, col * LANES + g * gw, col * LANES + (g + 1) * gw, axis=2)
            return a.reshape(batch, u, dil, gw).transpose(0, 2, 1, 3).reshape(batch * dil, u, gw)

        o, lse = _band_attention(split(COL_DL[0]), split(COL_DL[1]), split(COL_DL[2]))
        merge = lambda a: a.reshape(batch, dil, u, gw).transpose(0, 2, 1, 3).reshape(t, gw)
        outs.append(merge(o))
        lses.append(merge(lse))
    tm = min(1024, t)
    spec = pl.BlockSpec((tm, gw), lambda i: (i, 0))
    return pl.pallas_call(
        _dil_mix_kernel,
        grid=(t // tm,),
        in_specs=[spec] * 6,
        out_specs=spec,
        out_shape=jax.ShapeDtypeStruct((t, gw), BF16),
        compiler_params=_cparams(("arbitrary",), 32),
        name="dilated_mix",
    )(*outs, *lses)


def _merge_kernel(h_ref, wg0, wg1, wg2, bg0, bg1, bg2, o0, o1, o2, wp0, wp1, wp2, out_ref):
    h = h_ref[...]

    def branch(wg, bg, o, wp):
        return jax.nn.sigmoid(_dot(h, wg[...]) + bg[...]) * _dot(o[...], wp[...])

    merged = branch(wg0, bg0, o0, wp0) + branch(wg1, bg1, o1, wp1) + branch(wg2, bg2, o2, wp2)
    out_ref[...] = merged.astype(out_ref.dtype)


def gated_merge(h, w_gate, b_gate, branch_outs, branch_projs):
    t, d = h.shape
    tm, tn = min(512, t), min(512, d)
    nj = d // tn
    b_gate2 = b_gate.reshape(1, -1)
    gate_w = [pl.BlockSpec((d, tn), lambda j, i, br=br: (0, br * nj + j)) for br in range(3)]
    gate_b = [pl.BlockSpec((1, tn), lambda j, i, br=br: (0, br * nj + j)) for br in range(3)]
    o_specs = [pl.BlockSpec((tm, o.shape[1]), lambda j, i: (i, 0)) for o in branch_outs]
    p_specs = [pl.BlockSpec((w.shape[0], tn), lambda j, i: (0, j)) for w in branch_projs]
    return pl.pallas_call(
        _merge_kernel,
        grid=(nj, t // tm),
        in_specs=[pl.BlockSpec((tm, d), lambda j, i: (i, 0))] + gate_w + gate_b + o_specs + p_specs,
        out_specs=pl.BlockSpec((tm, tn), lambda j, i: (i, j)),
        out_shape=jax.ShapeDtypeStruct((t, d), BF16),
        compiler_params=_cparams(("arbitrary", "arbitrary"), 48),
        name="gated_merge",
    )(h, w_gate, w_gate, w_gate, b_gate2, b_gate2, b_gate2, *branch_outs, *branch_projs)


def _outproj_kernel(m_ref, w_ref, x_ref, gate_ref, g_ref, b_ref, sc_ref, sh_ref, wrh_ref, wrl_ref, br_ref,
                    x1_ref, h2_ref, lg_ref, *, alpha):
    y = _dot(m_ref[...], w_ref[...])
    x1 = _ln(alpha * x_ref[...] + gate_ref[...] * y) * g_ref[...] + b_ref[...]
    x1_ref[...] = x1
    h2 = _ln(x1) * (1.0 + sc_ref[...]) + sh_ref[...]
    h2_ref[...] = h2
    hi = h2.astype(BF16)
    lo = (h2 - hi.astype(F32)).astype(BF16)
    wrh = wrh_ref[...]
    lg_ref[...] = _dot(hi, wrh) + _dot(lo, wrh) + _dot(hi, wrl_ref[...]) + br_ref[...]


def out_projection(merged, w_out, x2d, mod_l, ln_g, ln_b, w_router, b_router, seq, alpha):
    t, d = x2d.shape
    tm = min(256, seq)
    bpb = seq // tm
    n_exp = w_router.shape[1]
    wr = jnp.zeros((d, LANES), F32).at[:, :n_exp].set(w_router)
    wr_hi = wr.astype(BF16)
    wr_lo = (wr - wr_hi.astype(F32)).astype(BF16)
    br = jnp.zeros((1, LANES), F32).at[0, :n_exp].set(b_router)
    row = pl.BlockSpec((tm, d), lambda i: (i, 0))
    vec = pl.BlockSpec((1, d), lambda i: (0, 0))
    wr_spec = pl.BlockSpec((d, LANES), lambda i: (0, 0))
    return pl.pallas_call(
        functools.partial(_outproj_kernel, alpha=alpha),
        grid=(t // tm,),
        in_specs=[row, pl.BlockSpec((d, d), lambda i: (0, 0)), row, _mod_spec(2, d, bpb), vec, vec,
                  _mod_spec(4, d, bpb), _mod_spec(3, d, bpb), wr_spec, wr_spec,
                  pl.BlockSpec((1, LANES), lambda i: (0, 0))],
        out_specs=[row, row, pl.BlockSpec((tm, LANES), lambda i: (i, 0))],
        out_shape=[jax.ShapeDtypeStruct((t, d), F32), jax.ShapeDtypeStruct((t, d), F32),
                   jax.ShapeDtypeStruct((t, LANES), F32)],
        compiler_params=_cparams(("arbitrary",), 48),
        name="out_projection",
    )(merged, w_out, x2d, mod_l, ln_g.reshape(1, d), ln_b.reshape(1, d), mod_l, mod_l, wr_hi, wr_lo, br)


def _route_kernel(lg_ref, tri_ref, ir_ref, w_ref, cnt_ref, carry_ref, *, n_exp):
    @pl.when(pl.program_id(0) == 0)
    def _():
        carry_ref[...] = jnp.zeros_like(carry_ref)

    tr = lg_ref.shape[0]
    lane = lax.broadcasted_iota(jnp.int32, (tr, LANES), 1)
    logits = jnp.where(lane < n_exp, lg_ref[...], -jnp.inf)
    vals, sels, idxs = [], [], []
    for _ in range(TOP_K):
        m = jnp.max(logits, axis=1, keepdims=True)
        idx = jnp.min(jnp.where(logits == m, lane, LANES), axis=1, keepdims=True)
        sel = lane == idx
        vals.append(m)
        idxs.append(idx)
        sels.append(sel)
        logits = jnp.where(sel, -jnp.inf, logits)
    exps = [jnp.exp(v - vals[0]) for v in vals]
    tot = exps[0] + exps[1] + exps[2] + exps[3]
    onehot = jnp.zeros((tr, LANES), F32)
    for sel in sels:
        onehot = onehot + jnp.where(sel, 1.0, 0.0)
    before = _dot(tri_ref[...], onehot.astype(BF16)) + carry_ref[0:1, :]
    ir = jnp.zeros((tr, LANES), jnp.int32)
    wt = jnp.zeros((tr, LANES), F32)
    for k in range(TOP_K):
        rank = jnp.sum(jnp.where(sels[k], before, 0.0), axis=1, keepdims=True).astype(jnp.int32)
        ir = jnp.where(lane == k, idxs[k], ir)
        ir = jnp.where(lane == TOP_K + k, rank, ir)
        wt = jnp.where(lane == k, exps[k] / tot, wt)
    ir_ref[...] = ir
    w_ref[...] = wt
    total = before[-1:, :] + onehot[-1:, :]
    carry_ref[...] = jnp.broadcast_to(total, carry_ref.shape)
    cnt_ref[...] = jnp.broadcast_to(total, cnt_ref.shape)


def route(logits, n_exp):
    t = logits.shape[0]
    tr = min(512, t)
    tri = jnp.asarray((np.arange(tr)[:, None] > np.arange(tr)[None, :]).astype(np.float32), BF16)
    row = pl.BlockSpec((tr, LANES), lambda i: (i, 0))
    return pl.pallas_call(
        functools.partial(_route_kernel, n_exp=n_exp),
        grid=(t // tr,),
        in_specs=[row, pl.BlockSpec((tr, tr), lambda i: (0, 0))],
        out_specs=[row, row, pl.BlockSpec((SUBLANES, LANES), lambda i: (0, 0))],
        out_shape=[jax.ShapeDtypeStruct((t, LANES), jnp.int32), jax.ShapeDtypeStruct((t, LANES), F32),
                   jax.ShapeDtypeStruct((SUBLANES, LANES), F32)],
        scratch_shapes=[pltpu.VMEM((SUBLANES, LANES), F32)],
        compiler_params=_cparams(("arbitrary",), 32),
        name="route",
    )(logits, tri)


def _prep_up_kernel(w_ref, even_ref, odd_ref, g_ref, l_ref):
    w = w_ref[...].astype(BF16)
    g_ref[...] = _dot(w, even_ref[...]).astype(g_ref.dtype)
    l_ref[...] = _dot(w, odd_ref[...]).astype(l_ref.dtype)


def prepare_up_weights(w_up):
    d, f2 = w_up.shape[-2:]
    w3 = w_up.reshape(-1, d, f2)
    n = w3.shape[0]
    tn = min(512, f2)
    half = tn // 2
    even = np.zeros((tn, half), np.float32)
    odd = np.zeros((tn, half), np.float32)
    even[2 * np.arange(half), np.arange(half)] = 1.0
    odd[2 * np.arange(half) + 1, np.arange(half)] = 1.0
    sel = pl.BlockSpec((tn, half), lambda e, j: (0, 0))
    out_spec = pl.BlockSpec((None, d, half), lambda e, j: (e, 0, j))
    out_shape = jax.ShapeDtypeStruct((n, d, f2 // 2), BF16)
    return pl.pallas_call(
        _prep_up_kernel,
        grid=(n, f2 // tn),
        in_specs=[pl.BlockSpec((None, d, tn), lambda e, j: (e, 0, j)), sel, sel],
        out_specs=[out_spec, out_spec],
        out_shape=[out_shape, out_shape],
        compiler_params=_cparams(("arbitrary", "arbitrary"), 32),
        name="prepare_up_weights",
    )(w3, jnp.asarray(even, BF16), jnp.asarray(odd, BF16))


def _expert_kernel(be_ref, tok_cur_ref, tok_next_ref, h_ref, wg_ref, wl_ref, bg_ref, bl_ref, wd_ref, bd_ref,
                   y_ref, xbuf, sem, *, row_block):
    i = pl.program_id(0)
    n_blocks = pl.num_programs(0)
    slot = i % 2

    def block_wait(dst_slot):
        pltpu.make_async_copy(h_ref.at[pl.ds(0, row_block), :], xbuf.at[dst_slot], sem.at[dst_slot]).wait()

    def start_gather(tok_ref, dst_slot):
        for r in range(row_block):
            pltpu.make_async_copy(h_ref.at[pl.ds(tok_ref[0, r], 1), :], xbuf.at[dst_slot, pl.ds(r, 1), :],
                                  sem.at[dst_slot]).start()

    @pl.when(i == 0)
    def _():
        start_gather(tok_cur_ref, 0)

    block_wait(slot)
    x = xbuf[slot].astype(BF16)
    start_gather(tok_next_ref, 1 - slot)
    glu = jnp.minimum(_dot(x, wg_ref[...]) + bg_ref[...], SWIGLU_LIMIT)
    lin = jnp.clip(_dot(x, wl_ref[...]) + bl_ref[...], -SWIGLU_LIMIT, SWIGLU_LIMIT)
    act = glu * jax.nn.sigmoid(SWIGLU_ALPHA * glu) * (lin + 1.0)
    y_ref[...] = _dot(act.astype(BF16), wd_ref[...]) + bd_ref[...]

    @pl.when(i == n_blocks - 1)
    def _():
        block_wait(1 - slot)


def expert_ffn(h2, row_tok, block_e, w_glu, w_lin, up_offset, b_glu, b_lin, w_down, b_down, row_block):
    d = h2.shape[1]
    n_rows = row_tok.shape[0]
    n_blocks = n_rows // row_block
    n_exp = w_down.shape[0]
    f = w_glu.shape[2]
    once = pl.Buffered(1)
    tok3 = row_tok.reshape(n_blocks, 1, row_block)
    grid_spec = pltpu.PrefetchScalarGridSpec(
        num_scalar_prefetch=1,
        grid=(n_blocks,),
        in_specs=[pl.BlockSpec((None, 1, row_block), lambda i, be: (i, 0, 0), memory_space=pltpu.SMEM),
                  pl.BlockSpec((None, 1, row_block), lambda i, be: (jnp.minimum(i + 1, n_blocks - 1), 0, 0),
                               memory_space=pltpu.SMEM),
                  pl.BlockSpec(memory_space=pl.ANY),
                  pl.BlockSpec((None, d, f), lambda i, be: (be[i] + up_offset, 0, 0), pipeline_mode=once),
                  pl.BlockSpec((None, d, f), lambda i, be: (be[i] + up_offset, 0, 0), pipeline_mode=once),
                  pl.BlockSpec((None, 1, f), lambda i, be: (be[i], 0, 0)),
                  pl.BlockSpec((None, 1, f), lambda i, be: (be[i], 0, 0)),
                  pl.BlockSpec((None, f, d), lambda i, be: (be[i], 0, 0), pipeline_mode=once),
                  pl.BlockSpec((None, 1, d), lambda i, be: (be[i], 0, 0))],
        out_specs=pl.BlockSpec((row_block, d), lambda i, be: (i, 0)),
        scratch_shapes=[pltpu.VMEM((2, row_block, d), F32), pltpu.SemaphoreType.DMA((2,))])
    return pl.pallas_call(
        functools.partial(_expert_kernel, row_block=row_block),
        grid_spec=grid_spec,
        out_shape=jax.ShapeDtypeStruct((n_rows, d), F32),
        compiler_params=_cparams(("arbitrary",)),
        name="expert_ffn",
    )(block_e, tok3, tok3, h2, w_glu, w_lin, b_glu.reshape(n_exp, 1, f), b_lin.reshape(n_exp, 1, f),
      w_down, b_down.reshape(n_exp, 1, d))


def _combine_kernel(dest_cur_ref, dest_next_ref, ys_ref, w_ref, x_ref, gate_ref, g_ref, b_ref, sc_ref, sh_ref,
                    x2_ref, h_ref, buf, sem, *, tc, alpha):
    i = pl.program_id(0)
    n_tiles = pl.num_programs(0)
    slot = i % 2

    def start_gather(dest_ref, dst_slot):
        for tok in range(tc):
            for k in range(TOP_K):
                pltpu.make_async_copy(ys_ref.at[pl.ds(dest_ref[0, tok * TOP_K + k], 1), :],
                                      buf.at[dst_slot, k, pl.ds(tok, 1), :], sem.at[dst_slot]).start()

    def tile_wait(dst_slot):
        for k in range(TOP_K):
            pltpu.make_async_copy(ys_ref.at[pl.ds(0, tc), :], buf.at[dst_slot, k], sem.at[dst_slot]).wait()

    @pl.when(i == 0)
    def _():
        start_gather(dest_cur_ref, 0)

    tile_wait(slot)
    start_gather(dest_next_ref, 1 - slot)
    w = w_ref[...]
    y = w[:, 0:1] * buf[slot, 0]
    for k in range(1, TOP_K):
        y = y + w[:, k:k + 1] * buf[slot, k]
    x2 = _ln(alpha * x_ref[...] + gate_ref[...] * y) * g_ref[...] + b_ref[...]
    x2_ref[...] = x2
    h_ref[...] = (_ln(x2) * (1.0 + sc_ref[...]) + sh_ref[...]).astype(h_ref.dtype)

    @pl.when(i == n_tiles - 1)
    def _():
        tile_wait(1 - slot)


def moe_combine(ys, dest, top_w, x1, mod_l, mod_next, ln_g, ln_b, seq, alpha):
    t, d = x1.shape
    tc = min(128, seq)
    bpb = seq // tc
    row = pl.BlockSpec((tc, d), lambda i: (i, 0))
    vec = pl.BlockSpec((1, d), lambda i: (0, 0))
    n_tiles = t // tc
    dest3 = dest.reshape(n_tiles, 1, tc * TOP_K)
    return pl.pallas_call(
        functools.partial(_combine_kernel, tc=tc, alpha=alpha),
        grid=(n_tiles,),
        in_specs=[pl.BlockSpec((None, 1, tc * TOP_K), lambda i: (i, 0, 0), memory_space=pltpu.SMEM),
                  pl.BlockSpec((None, 1, tc * TOP_K), lambda i: (jnp.minimum(i + 1, n_tiles - 1), 0, 0),
                               memory_space=pltpu.SMEM),
                  pl.BlockSpec(memory_space=pl.ANY),
                  pl.BlockSpec((tc, LANES), lambda i: (i, 0)),
                  row, _mod_spec(5, d, bpb), vec, vec, _mod_spec(1, d, bpb), _mod_spec(0, d, bpb)],
        out_specs=[row, row],
        out_shape=[jax.ShapeDtypeStruct((t, d), F32), jax.ShapeDtypeStruct((t, d), BF16)],
        scratch_shapes=[pltpu.VMEM((2, TOP_K, tc, d), F32), pltpu.SemaphoreType.DMA((2,))],
        compiler_params=_cparams(("arbitrary",), 32),
        name="moe_combine",
    )(dest3, dest3, ys, top_w, x1, mod_l, ln_g.reshape(1, d), ln_b.reshape(1, d), mod_next, mod_next)


def moe_ffn(h2, logits, w_glu, w_lin, up_offset, b_glu, b_lin, w_down, b_down):
    t, d = h2.shape
    n_exp = w_down.shape[0]
    row_block = 256
    idx_rank, top_w, counts = route(logits, n_exp)
    top_idx = idx_rank[:, :TOP_K]
    rank = idx_rank[:, TOP_K:2 * TOP_K]
    counts = counts[0, :n_exp].astype(jnp.int32)
    padded = (counts + row_block - 1) // row_block * row_block
    padded_end = jnp.cumsum(padded)
    padded_start = padded_end - padded
    dest = padded_start[top_idx] + rank
    n_rows = t * TOP_K + n_exp * row_block
    tok_ids = jnp.broadcast_to(jnp.arange(t, dtype=jnp.int32)[:, None], (t, TOP_K))
    row_tok = jnp.zeros((n_rows,), jnp.int32).at[dest.reshape(-1)].set(tok_ids.reshape(-1))
    n_blocks = n_rows // row_block
    block_e = jnp.minimum(
        jnp.searchsorted(padded_end, jnp.arange(n_blocks, dtype=jnp.int32) * row_block, side='right'),
        n_exp - 1).astype(jnp.int32)
    ys = expert_ffn(h2, row_tok, block_e, w_glu, w_lin, up_offset, b_glu, b_lin, w_down, b_down, row_block)
    return ys, dest.astype(jnp.int32), top_w


def kernel(x, c, w_ada, b_ada, w_in, b_forget, w_gate, b_gate, w_proj_sb, w_proj_dil, w_proj_fox, w_out,
           ln1_g, ln1_b, w_router, b_router, w_up, b_up, w_down, b_down, ln2_g, ln2_b):
    batch, seq, d = x.shape
    depth = w_ada.shape[0]
    t = batch * seq
    alpha = (2.0 * depth) ** 0.25
    rope_tabs = rope_pair_tables(seq)
    mod = adaln_mod(c, w_ada, b_ada)
    x2d = x.reshape(t, d)
    h = ln_mod(x2d, mod[0], seq, comp_shift=0, comp_scale=1)
    n_exp = w_up.shape[1]
    w_glu, w_lin = prepare_up_weights(w_up)
    for l in range(depth):
        w_in_l = w_in[l]
        qkv = in_projection(h, w_in_l[:, :N_QKV].astype(BF16), rope_tabs, seq)
        q_feat, k_feat = forget_features(h, w_in_l[:, N_QKV:], b_forget[l], batch, seq)
        o_sb = stick_breaking_attention(qkv, batch, seq)
        o_dl = dilated_window_attention(qkv, batch, seq)
        o_fx = forgetting_attention(qkv, q_feat, k_feat, batch, seq)
        merged = gated_merge(h, w_gate[l].astype(BF16), b_gate[l], (o_sb, o_dl, o_fx),
                             (w_proj_sb[l].astype(BF16), w_proj_dil[l].astype(BF16),
                              w_proj_fox[l].astype(BF16)))
        x1, h2, logits = out_projection(merged, w_out[l].astype(BF16), x2d, mod[l], ln1_g[l], ln1_b[l],
                                        w_router[l], b_router[l], seq, alpha)
        ys, dest, top_w = moe_ffn(h2, logits, w_glu, w_lin, l * n_exp,
                                  b_up[l][:, 0::2], b_up[l][:, 1::2], w_down[l].astype(BF16), b_down[l])
        mod_next = mod[min(l + 1, depth - 1)]
        x2d, h = moe_combine(ys, dest, top_w, x1, mod[l], mod_next, ln2_g[l], ln2_b[l], seq, alpha)
    return x2d.reshape(batch, seq, d)
```

```python
import functools

import numpy as np
import jax
import jax.numpy as jnp
from jax import lax
from jax.experimental import pallas as pl
from jax.experimental.pallas import tpu as pltpu

F32 = jnp.float32
BF16 = jnp.bfloat16

HEAD_DIM = 64
H_SB = 8
DIL_CONFIGS = ((128, 1), (512, 4), (2048, 16))
H_DIL_PER_GROUP = 4
H_FOX = 12
ROPE_THETA = 500000.0
ROPE_DIMS = HEAD_DIM // 4
TOP_K = 4
SWIGLU_ALPHA = 1.702
SWIGLU_LIMIT = 7.0
LN_EPS = 1e-5
QK_SCALE = HEAD_DIM ** -0.5

LANES = 128
SUBLANES = 8
VMEM_BUDGET_MB = 56

N_DIL_GROUPS = len(DIL_CONFIGS)
W_SB = H_SB * HEAD_DIM
W_DIL = N_DIL_GROUPS * H_DIL_PER_GROUP * HEAD_DIM
W_DIL_OUT = H_DIL_PER_GROUP * HEAD_DIM
W_FOX = H_FOX * HEAD_DIM
N_QKV = 3 * (W_SB + W_DIL + W_FOX)
COL_SB = (0, W_SB // LANES, 2 * W_SB // LANES)
_DL0 = 3 * W_SB // LANES
COL_DL = (_DL0, _DL0 + W_DIL // LANES, _DL0 + 2 * W_DIL // LANES)
_FX0 = _DL0 + 3 * W_DIL // LANES
COL_FX = (_FX0, _FX0 + W_FOX // LANES, _FX0 + 2 * W_FOX // LANES)
BAND = 128
NEG_BIG = -1e30
EXP_UNDERFLOW = -104.0


def _cparams(semantics, vmem_mb=VMEM_BUDGET_MB):
    return pltpu.CompilerParams(dimension_semantics=semantics, vmem_limit_bytes=vmem_mb * 2 ** 20)


def _dot(a, b):
    return jnp.dot(a, b, preferred_element_type=F32)


def _dot_t(a, b):
    return lax.dot_general(a, b, (((1,), (1,)), ((), ())), preferred_element_type=F32)


def _ln(x):
    mu = jnp.mean(x, axis=-1, keepdims=True)
    xc = x - mu
    var = jnp.mean(xc * xc, axis=-1, keepdims=True)
    return xc * lax.rsqrt(var + LN_EPS)


def _softplus_neg_abs(z):
    return jnp.log1p(jnp.exp(-jnp.abs(z)))


def _split3(x):
    hi = x.astype(BF16)
    r1 = x - hi.astype(F32)
    mid = r1.astype(BF16)
    lo = (r1 - mid.astype(F32)).astype(BF16)
    return hi, mid, lo


def _mod_kernel(c_ref, w_ref, b_ref, o_ref):
    c = c_ref[...]
    act = (c * jax.nn.sigmoid(c)).astype(BF16)
    o_ref[...] = _dot(act, w_ref[...].astype(BF16)) + b_ref[...]


def adaln_mod(c, w_ada, b_ada):
    n_layers, d, n = w_ada.shape
    b = c.shape[0]
    assert b <= SUBLANES
    tn = 512
    c_pad = jnp.zeros((SUBLANES, d), F32).at[:b].set(c)
    out = pl.pallas_call(
        _mod_kernel,
        grid=(n_layers, n // tn),
        in_specs=[pl.BlockSpec((SUBLANES, d), lambda l, j: (0, 0)),
                  pl.BlockSpec((None, d, tn), lambda l, j: (l, 0, j)),
                  pl.BlockSpec((None, 1, tn), lambda l, j: (l, 0, j))],
        out_specs=pl.BlockSpec((None, SUBLANES, tn), lambda l, j: (l, 0, j)),
        out_shape=jax.ShapeDtypeStruct((n_layers, SUBLANES, n), F32),
        compiler_params=_cparams(("arbitrary", "arbitrary"), 32),
        name="adaln_mod",
    )(c_pad, w_ada, b_ada.reshape(n_layers, 1, n))
    return out[:, :b].reshape(n_layers, b, 6, d).transpose(0, 2, 1, 3)[:, :, :, None, :]


def _mod_spec(comp, d, rows_per_batch_blocks):
    return pl.BlockSpec((None, None, 1, d), lambda i, *_: (comp, i // rows_per_batch_blocks, 0, 0))


def _ln_mod_kernel(x_ref, sc_ref, sh_ref, h_ref):
    h_ref[...] = (_ln(x_ref[...]) * (1.0 + sc_ref[...]) + sh_ref[...]).astype(h_ref.dtype)


def ln_mod(x2d, mod_l, seq, comp_shift, comp_scale):
    t, d = x2d.shape
    ts = min(512, seq)
    bpb = seq // ts
    return pl.pallas_call(
        _ln_mod_kernel,
        grid=(t // ts,),
        in_specs=[pl.BlockSpec((ts, d), lambda i: (i, 0)),
                  _mod_spec(comp_scale, d, bpb), _mod_spec(comp_shift, d, bpb)],
        out_specs=pl.BlockSpec((ts, d), lambda i: (i, 0)),
        out_shape=jax.ShapeDtypeStruct((t, d), BF16),
        compiler_params=_cparams(("arbitrary",), 32),
        name="ln_mod",
    )(x2d, mod_l, mod_l)


def _inproj_kernel(h_ref, w_ref, cos_ref, s1_ref, s2_ref, o_ref, *, rope_lo, rope_hi, tn):
    j = pl.program_id(1)
    acc = _dot(h_ref[...], w_ref[...])
    is_rope = jnp.logical_and(j >= rope_lo, j < rope_hi)

    @pl.when(is_rope)
    def _():
        c, s1, s2 = cos_ref[...], s1_ref[...], s2_ref[...]
        for blk in range(tn // LANES):
            a = acc[:, blk * LANES:(blk + 1) * LANES]
            r = a * c + pltpu.roll(a, LANES - ROPE_DIMS // 2, 1) * s1 + pltpu.roll(a, ROPE_DIMS // 2, 1) * s2
            o_ref[:, blk * LANES:(blk + 1) * LANES] = r.astype(o_ref.dtype)

    @pl.when(jnp.logical_not(is_rope))
    def _():
        o_ref[...] = acc.astype(o_ref.dtype)


def rope_pair_tables(seq):
    pos = jnp.arange(seq, dtype=F32)
    inv = ROPE_THETA ** (-jnp.arange(0, ROPE_DIMS, 2, dtype=F32) / ROPE_DIMS)
    ang = pos[:, None] * inv[None, :]
    cos, sin = jnp.cos(ang), jnp.sin(ang)
    half = ROPE_DIMS // 2
    head_c = jnp.concatenate([cos, cos, jnp.ones((seq, HEAD_DIM - ROPE_DIMS), F32)], axis=1)
    head_s1 = jnp.concatenate([-sin, jnp.zeros((seq, HEAD_DIM - half), F32)], axis=1)
    head_s2 = jnp.concatenate([jnp.zeros((seq, half), F32), sin,
                               jnp.zeros((seq, HEAD_DIM - ROPE_DIMS), F32)], axis=1)
    two = lambda a: jnp.concatenate([a, a], axis=1)
    return two(head_c), two(head_s1), two(head_s2)


def in_projection(h, w_qkv, rope_tabs, seq):
    t, d = h.shape
    n = w_qkv.shape[1]
    tm = min(1024, seq)
    tn = 512
    assert seq % tm == 0 and n % tn == 0
    rope_lo, rope_hi = COL_DL[0] * LANES, COL_DL[2] * LANES
    assert rope_lo % tn == 0 and rope_hi % tn == 0
    spb = seq // tm
    tab_spec = pl.BlockSpec((tm, LANES), lambda i, j: (i % spb, 0))
    return pl.pallas_call(
        functools.partial(_inproj_kernel, rope_lo=rope_lo // tn, rope_hi=rope_hi // tn, tn=tn),
        grid=(t // tm, n // tn),
        in_specs=[pl.BlockSpec((tm, d), lambda i, j: (i, 0)),
                  pl.BlockSpec((d, tn), lambda i, j: (0, j)),
                  tab_spec, tab_spec, tab_spec],
        out_specs=pl.BlockSpec((tm, tn), lambda i, j: (i, j)),
        out_shape=jax.ShapeDtypeStruct((t, n), BF16),
        compiler_params=_cparams(("arbitrary", "arbitrary"), 48),
        name="in_projection",
    )(h, w_qkv, *rope_tabs)


FEAT_PER_HEAD = 6


def _forget_feature_maps():
    width = (H_FOX // 2) * LANES
    pq = np.zeros((3 * LANES, width), np.float32)
    pk = np.zeros((3 * LANES, width), np.float32)
    cq = np.zeros((1, width), np.float32)
    ck = np.zeros((1, width), np.float32)
    for head in range(H_FOX):
        base = (head // 2) * LANES + (head % 2) * FEAT_PER_HEAD
        for piece in range(3):
            pq[piece * LANES + head, base + piece] = 1.0
            pk[piece * LANES + head, base + 3 + piece] = -1.0
        cq[0, base + 3:base + 6] = 1.0
        ck[0, base:base + 3] = 1.0
    return pq, pk, cq, ck


def _forget_kernel(h_ref, w_ref, b_ref, tri_ref, pq_ref, pk_ref, cq_ref, ck_ref, qf_ref, kf_ref, carry_ref):
    @pl.when(pl.program_id(1) == 0)
    def _():
        carry_ref[...] = jnp.zeros_like(carry_ref)

    f = _dot(h_ref[...], w_ref[...]) + b_ref[...]
    log_f = jnp.minimum(f, 0.0) - _softplus_neg_abs(f)
    tri = tri_ref[...]
    hi, mid, lo = _split3(log_f)
    cum = _dot(tri, hi) + _dot(tri, mid) + _dot(tri, lo) + carry_ref[0:1, :]
    carry_ref[...] = jnp.broadcast_to(cum[-1:, :], carry_ref.shape)
    pieces = jnp.concatenate(_split3(cum), axis=1)
    qf_ref[...] = (_dot(pieces, pq_ref[...]) + cq_ref[...]).astype(qf_ref.dtype)
    kf_ref[...] = (_dot(pieces, pk_ref[...]) + ck_ref[...]).astype(kf_ref.dtype)


def forget_features(h, w_f, b_f, batch, seq):
    t, d = h.shape
    ts = min(256, seq)
    nsb = seq // ts
    w_pad = jnp.zeros((d, LANES), BF16).at[:, :H_FOX].set(w_f.astype(BF16))
    b_pad = jnp.zeros((1, LANES), F32).at[0, :H_FOX].set(b_f)
    tri = (np.arange(ts)[:, None] >= np.arange(ts)[None, :]).astype(np.float32)
    pq, pk, cq, ck = _forget_feature_maps()
    width = pq.shape[1]
    const = lambda shape: pl.BlockSpec(shape, lambda b, s: (0, 0))
    out_spec = pl.BlockSpec((ts, width), lambda b, s: (b * nsb + s, 0))
    out_shape = jax.ShapeDtypeStruct((t, width), BF16)
    return pl.pallas_call(
        _forget_kernel,
        grid=(batch, nsb),
        in_specs=[pl.BlockSpec((ts, d), lambda b, s: (b * nsb + s, 0)),
                  const((d, LANES)), const((1, LANES)), const((ts, ts)),
                  const(pq.shape), const(pk.shape), const(cq.shape), const(ck.shape)],
        out_specs=[out_spec, out_spec],
        out_shape=[out_shape, out_shape],
        scratch_shapes=[pltpu.VMEM((SUBLANES, LANES), F32)],
        compiler_params=_cparams(("arbitrary", "arbitrary"), 32),
        name="forget_features",
    )(h, w_pad, b_pad, jnp.asarray(tri, BF16), jnp.asarray(pq, BF16), jnp.asarray(pk, BF16),
      jnp.asarray(cq), jnp.asarray(ck))


def _pair_masks(tq):
    lane = lax.broadcasted_iota(jnp.int32, (tq, LANES), 1)
    return lane < HEAD_DIM


KEY_SUB = 128
QRY_SUB = 128
SUFFIX_SUB = 256


def _diag_tile_kind(ks, qs, key_sub, qry_sub, strict, key_off):
    if key_off is None:
        return 'full'
    k_lo, k_hi = key_off + ks * key_sub, key_off + ks * key_sub + key_sub - 1
    q_lo, q_hi = qs * qry_sub, qs * qry_sub + qry_sub - 1
    if strict:
        if k_lo >= q_hi:
            return 'dead'
        return 'full' if k_hi < q_lo else 'partial'
    if k_lo > q_hi:
        return 'dead'
    return 'full' if k_hi <= q_lo else 'partial'


def _tile_allowed(ks, qs, key_sub, qry_sub, strict, key_off):
    kpos = key_off + ks * key_sub + lax.broadcasted_iota(jnp.int32, (key_sub, qry_sub), 0)
    qpos = qs * qry_sub + lax.broadcasted_iota(jnp.int32, (key_sub, qry_sub), 1)
    return kpos < qpos if strict else kpos <= qpos


def _pair_transpose_in(qkv3, col, width):
    return lax.slice_in_dim(qkv3, col * LANES, col * LANES + width, axis=2).transpose(0, 2, 1)


def _sb_kernel(q_ref, k_ref, vt_ref, later_ref, o_ref,
               qx_s, acc_s, run_s, first_s, z_s, hi_s, lo_s, btw_s, w_s, *, tq, tk, key_sub, qry_sub):
    q_blk = pl.program_id(2)
    head_a = _pair_masks(tq)
    q = q_ref[...] * QK_SCALE
    qx_s[0] = jnp.where(head_a, q, 0).astype(BF16)
    qx_s[1] = jnp.where(head_a, 0, q).astype(BF16)
    acc_s[...] = jnp.zeros_like(acc_s)
    run_s[...] = jnp.zeros_like(run_s)

    def step(key_off, kv_blk):
        k_ref_blk = k_ref.at[pl.ds(pl.multiple_of(kv_blk * tk, tk), tk), :]
        vt_blk = vt_ref.at[kv_blk]
        _sb_block(key_off, k_ref_blk, vt_blk)

    def _sb_block(key_off, k_ref, vt_ref):
        suf = later_ref.shape[0]
        n_suf = tk // suf
        per_suf = suf // key_sub
        tiles = [(ks, qs) for ks in range(tk // key_sub) for qs in range(tq // qry_sub)]
        kind_of = lambda ks, qs: _diag_tile_kind(ks, qs, key_sub, qry_sub, True, key_off)
        k = k_ref[...]
        for head in range(2):
            z_s[head] = _dot_t(k, qx_s[head])
        for head in range(2):
            for ks, qs in tiles:
                kl = slice(ks * key_sub, (ks + 1) * key_sub)
                ql = slice(qs * qry_sub, (qs + 1) * qry_sub)
                first_row = head * n_suf + ks // per_suf
                kind = kind_of(ks, qs)
                if kind == 'dead':
                    hi_s[head, kl, ql] = jnp.zeros((key_sub, qry_sub), BF16)
                    lo_s[head, kl, ql] = jnp.zeros((key_sub, qry_sub), BF16)
                    if ks % per_suf == 0:
                        first_s[first_row:first_row + 1, ql] = jnp.zeros((1, qry_sub), F32)
                    continue
                z = z_s[head, kl, ql]
                log_stop = jnp.minimum(z, 0.0) - jnp.log(1.0 + jnp.exp(-jnp.abs(z)))
                log_cont = log_stop - z
                if kind == 'partial':
                    log_cont = jnp.where(_tile_allowed(ks, qs, key_sub, qry_sub, True, key_off), log_cont, 0.0)
                z_s[head, kl, ql] = log_stop
                hi = log_cont.astype(BF16)
                hi_s[head, kl, ql] = hi
                lo_s[head, kl, ql] = (log_cont - hi.astype(F32)).astype(BF16)
                if ks % per_suf == 0:
                    first_s[first_row:first_row + 1, ql] = log_cont[0:1, :]
        later = later_ref[...]
        for head in range(2):
            for blk in range(n_suf):
                bl = slice(blk * suf, (blk + 1) * suf)
                btw_s[head, bl, :] = _dot(later, hi_s[head, bl, :]) + _dot(later, lo_s[head, bl, :])

        def block_sum(head, blk, ql):
            row = head * n_suf + blk
            return btw_s[head, blk * suf:blk * suf + 1, ql] + first_s[row:row + 1, ql]

        for head in range(2):
            for ks, qs in tiles:
                kl = slice(ks * key_sub, (ks + 1) * key_sub)
                ql = slice(qs * qry_sub, (qs + 1) * qry_sub)
                kind = kind_of(ks, qs)
                if kind == 'dead':
                    w_s[head, kl, ql] = jnp.zeros((key_sub, qry_sub), BF16)
                    continue
                after = run_s[head:head + 1, ql]
                for blk in range(ks // per_suf + 1, n_suf):
                    after = after + block_sum(head, blk, ql)
                w = jnp.exp(z_s[head, kl, ql] + btw_s[head, kl, ql] + after)
                if kind == 'partial':
                    w = jnp.where(_tile_allowed(ks, qs, key_sub, qry_sub, True, key_off), w, 0.0)
                w_s[head, kl, ql] = w.astype(BF16)
        for head in range(2):
            rows = slice(head * HEAD_DIM, (head + 1) * HEAD_DIM)
            acc_s[rows, :] += _dot(vt_ref[rows, :], w_s[head])
            total = run_s[head:head + 1, :]
            for blk in range(n_suf):
                total = total + block_sum(head, blk, slice(None))
            run_s[head:head + 1, :] = total

    def all_weights_zero():
        return jnp.max(run_s[0:2, :]) < EXP_UNDERFLOW

    step(0, q_blk)

    def more(carry):
        kv_blk, dead = carry
        return jnp.logical_and(kv_blk >= 0, jnp.logical_not(dead))

    def visit(carry):
        kv_blk, _ = carry
        step(None, kv_blk)
        return kv_blk - 1, all_weights_zero()

    lax.while_loop(more, visit, (q_blk - 1, all_weights_zero()))
    o_ref[...] = acc_s[...].astype(o_ref.dtype)


def stick_breaking_attention(qkv, batch, seq):
    tq = tk = min(512, seq)
    key_sub, qry_sub = min(KEY_SUB, tk), min(QRY_SUB, tq)
    n_pairs = W_SB // LANES
    n_kv = seq // tk
    qkv3 = qkv.reshape(batch, seq, qkv.shape[-1])
    qc, kc, vc = COL_SB
    v_t = _pair_transpose_in(qkv3, vc, W_SB).reshape(batch, n_pairs, LANES, n_kv, tk).transpose(0, 1, 3, 2, 4)
    suf = min(SUFFIX_SUB, tk)
    assert 2 * (tk // suf) <= SUBLANES and suf % key_sub == 0
    later = jnp.asarray((np.arange(suf)[None, :] > np.arange(suf)[:, None]).astype(np.float32), BF16)
    block_f32 = pltpu.VMEM((2, tk, tq), F32)
    block_bf16 = pltpu.VMEM((2, tk, tq), BF16)
    row_stats = pltpu.VMEM((SUBLANES, tq), F32)
    out_t = pl.pallas_call(
        functools.partial(_sb_kernel, tq=tq, tk=tk, key_sub=key_sub, qry_sub=qry_sub),
        grid=(batch, n_pairs, seq // tq),
        in_specs=[pl.BlockSpec((None, tq, LANES), lambda b, p, i: (b, i, qc + p)),
                  pl.BlockSpec((None, seq, LANES), lambda b, p, i: (b, 0, kc + p)),
                  pl.BlockSpec((None, None, n_kv, LANES, tk), lambda b, p, i: (b, p, 0, 0, 0)),
                  pl.BlockSpec((suf, suf), lambda b, p, i: (0, 0))],
        out_specs=pl.BlockSpec((None, LANES, tq), lambda b, p, i: (b, p, i)),
        scratch_shapes=[pltpu.VMEM((2, tq, LANES), BF16), pltpu.VMEM((LANES, tq), F32),
                        row_stats, row_stats, block_f32, block_bf16, block_bf16, block_f32, block_bf16],
        out_shape=jax.ShapeDtypeStruct((batch, W_SB, seq), BF16),
        compiler_params=_cparams(("arbitrary", "arbitrary", "arbitrary"), 32),
        name="stick_breaking_attention",
    )(qkv3, qkv3, v_t, later)
    return out_t.transpose(0, 2, 1).reshape(batch * seq, W_SB)


def _fox_kernel(q_ref, qf_ref, k_ref, kf_ref, vt_ref, o_ref,
                qx_s, acc_s, m_s, l_s, alpha_s, z_s, p_s, *, tq, tk, key_sub, qry_sub):
    q_blk = pl.program_id(2)
    lane = lax.broadcasted_iota(jnp.int32, (tq, LANES), 1)
    q = q_ref[...] * QK_SCALE
    qf = qf_ref[...]
    for head in range(2):
        in_head = (lane >= head * HEAD_DIM) & (lane < (head + 1) * HEAD_DIM)
        in_feat = (lane >= head * FEAT_PER_HEAD) & (lane < (head + 1) * FEAT_PER_HEAD)
        qx_s[head] = jnp.concatenate([jnp.where(in_head, q, 0).astype(BF16),
                                      jnp.where(in_feat, qf, 0).astype(BF16)], axis=1)
    acc_s[...] = jnp.zeros_like(acc_s)
    l_s[...] = jnp.zeros_like(l_s)
    m_s[...] = jnp.full_like(m_s, NEG_BIG)

    def step(key_off, kv_blk):
        rows = pl.ds(pl.multiple_of(kv_blk * tk, tk), tk)
        _fox_block(key_off, k_ref.at[rows, :], kf_ref.at[rows, :], vt_ref.at[kv_blk])

    def _fox_block(key_off, k_ref, kf_ref, vt_ref):
        kind_of = lambda ks, qs: _diag_tile_kind(ks, qs, key_sub, qry_sub, False, key_off)
        kx = jnp.concatenate([k_ref[...], kf_ref[...]], axis=1)
        for head in range(2):
            z_s[head] = _dot_t(kx, qx_s[head])
        for head in range(2):
            for qs in range(tq // qry_sub):
                ql = slice(qs * qry_sub, (qs + 1) * qry_sub)

                def score(ks):
                    z = z_s[head, ks * key_sub:(ks + 1) * key_sub, ql]
                    if kind_of(ks, qs) == 'partial':
                        z = jnp.where(_tile_allowed(ks, qs, key_sub, qry_sub, False, key_off), z, NEG_BIG)
                    return z

                live = [ks for ks in range(tk // key_sub) if kind_of(ks, qs) != 'dead']
                m_old = m_s[head:head + 1, ql]
                m_new = m_old
                for ks in live:
                    m_new = jnp.maximum(m_new, jnp.max(score(ks), axis=0, keepdims=True))
                alpha = jnp.exp(m_old - m_new)
                total = alpha * l_s[head:head + 1, ql]
                for ks in range(tk // key_sub):
                    kl = slice(ks * key_sub, (ks + 1) * key_sub)
                    if ks not in live:
                        p_s[head, kl, ql] = jnp.zeros((key_sub, qry_sub), BF16)
                        continue
                    prob = jnp.exp(score(ks) - m_new)
                    total = total + jnp.sum(prob, axis=0, keepdims=True)
                    p_s[head, kl, ql] = prob.astype(BF16)
                m_s[head:head + 1, ql] = m_new
                l_s[head:head + 1, ql] = total
                alpha_s[head:head + 1, ql] = alpha
        for head in range(2):
            rows = slice(head * HEAD_DIM, (head + 1) * HEAD_DIM)
            acc_s[rows, :] = alpha_s[head:head + 1, :] * acc_s[rows, :] + _dot(vt_ref[rows, :], p_s[head])

    per_q = tq // tk
    first_diag = q_blk * per_q

    def visit(kv_blk, carry):
        step(None, kv_blk)
        return carry

    lax.fori_loop(0, first_diag, visit, 0)
    for variant in range(per_q):
        step(variant * tk, first_diag + variant)
    for head in range(2):
        rows = slice(head * HEAD_DIM, (head + 1) * HEAD_DIM)
        o_ref[rows, :] = (acc_s[rows, :] / l_s[head:head + 1, :]).astype(o_ref.dtype)


def forgetting_attention(qkv, q_feat, k_feat, batch, seq):
    tk = min(512, seq)
    tq = min(1024, seq)
    key_sub, qry_sub = min(KEY_SUB, tk), min(QRY_SUB, tq)
    assert tq % tk == 0
    n_pairs = W_FOX // LANES
    n_kv = seq // tk
    qkv3 = qkv.reshape(batch, seq, qkv.shape[-1])
    qf3 = q_feat.reshape(batch, seq, n_pairs * LANES)
    kf3 = k_feat.reshape(batch, seq, n_pairs * LANES)
    qc, kc, vc = COL_FX
    v_t = _pair_transpose_in(qkv3, vc, W_FOX).reshape(batch, n_pairs, LANES, n_kv, tk).transpose(0, 1, 3, 2, 4)
    out_t = pl.pallas_call(
        functools.partial(_fox_kernel, tq=tq, tk=tk, key_sub=key_sub, qry_sub=qry_sub),
        grid=(batch, n_pairs, seq // tq),
        in_specs=[pl.BlockSpec((None, tq, LANES), lambda b, p, i: (b, i, qc + p)),
                  pl.BlockSpec((None, tq, LANES), lambda b, p, i: (b, i, p)),
                  pl.BlockSpec((None, seq, LANES), lambda b, p, i: (b, 0, kc + p)),
                  pl.BlockSpec((None, seq, LANES), lambda b, p, i: (b, 0, p)),
                  pl.BlockSpec((None, None, n_kv, LANES, tk), lambda b, p, i: (b, p, 0, 0, 0))],
        out_specs=pl.BlockSpec((None, LANES, tq), lambda b, p, i: (b, p, i)),
        scratch_shapes=[pltpu.VMEM((2, tq, 2 * LANES), BF16), pltpu.VMEM((LANES, tq), F32),
                        pltpu.VMEM((SUBLANES, tq), F32), pltpu.VMEM((SUBLANES, tq), F32),
                        pltpu.VMEM((SUBLANES, tq), F32), pltpu.VMEM((2, tk, tq), F32),
                        pltpu.VMEM((2, tk, tq), BF16)],
        out_shape=jax.ShapeDtypeStruct((batch, W_FOX, seq), BF16),
        compiler_params=_cparams(("arbitrary", "arbitrary", "arbitrary"), 40),
        name="forgetting_attention",
    )(qkv3, qf3, qkv3, kf3, v_t)
    return out_t.transpose(0, 2, 1).reshape(batch * seq, W_FOX)


def _band_kernel(q_ref, kc_ref, vc_ref, kp_ref, vp_ref, o_ref, lse_ref, *, tq):
    i = pl.program_id(2)
    head_a = _pair_masks(tq)
    q = q_ref[...] * QK_SCALE
    kc, vc, kp, vp = kc_ref[...], vc_ref[...], kp_ref[...], vp_ref[...]
    diff = lax.broadcasted_iota(jnp.int32, (tq, tq), 0) - lax.broadcasted_iota(jnp.int32, (tq, tq), 1)
    mask_c = jnp.logical_and(diff >= 0, diff <= BAND)
    rp = lax.broadcasted_iota(jnp.int32, (tq, BAND), 0)
    cp = lax.broadcasted_iota(jnp.int32, (tq, BAND), 1)
    mask_p = jnp.logical_and(cp >= rp, i > 0)
    outs, lses = [], []
    for sel in (head_a, jnp.logical_not(head_a)):
        qh = jnp.where(sel, q, 0).astype(BF16)
        sc = jnp.where(mask_c, _dot_t(qh, kc), NEG_BIG)
        sp = jnp.where(mask_p, _dot_t(qh, kp), NEG_BIG)
        m = jnp.maximum(jnp.max(sc, axis=1, keepdims=True), jnp.max(sp, axis=1, keepdims=True))
        ec = jnp.exp(sc - m)
        ep = jnp.exp(sp - m)
        den = jnp.sum(ec, axis=1, keepdims=True) + jnp.sum(ep, axis=1, keepdims=True)
        outs.append((_dot(ec.astype(BF16), vc) + _dot(ep.astype(BF16), vp)) / den)
        lses.append(m + jnp.log(den))
    o_ref[...] = jnp.where(head_a, outs[0], outs[1])
    lse_ref[...] = jnp.where(head_a, lses[0], lses[1])


def _band_attention(q, k, v):
    nb, u, w = q.shape
    tq = min(256, u)
    assert u % tq == 0 and tq % BAND == 0
    sub = tq // BAND
    cur = pl.BlockSpec((None, tq, LANES), lambda n, p, i: (n, i, p))
    prev = pl.BlockSpec((None, BAND, LANES), lambda n, p, i: (n, jnp.maximum(i * sub - 1, 0), p))
    shp = jax.ShapeDtypeStruct((nb, u, w), F32)
    return pl.pallas_call(
        functools.partial(_band_kernel, tq=tq),
        grid=(nb, w // LANES, u // tq),
        in_specs=[cur, cur, cur, prev, prev],
        out_specs=[cur, cur],
        out_shape=[shp, shp],
        compiler_params=_cparams(("arbitrary", "arbitrary", "arbitrary"), 32),
        name="band_attention",
    )(q, k, v, k, v)


def _dil_mix_kernel(o0, o1, o2, l0, l1, l2, out_ref):
    a, b, c = l0[...], l1[...], l2[...]
    m = jnp.maximum(jnp.maximum(a, b), c)
    ea, eb, ec = jnp.exp(a - m), jnp.exp(b - m), jnp.exp(c - m)
    tot = ea + eb + ec
    out_ref[...] = ((o0[...] * ea + o1[...] * eb + o2[...] * ec) / tot).astype(out_ref.dtype)


def dilated_window_attention(qkv, batch, seq):
    t = batch * seq
    qkv3 = qkv.reshape(batch, seq, qkv.shape[-1])
    gw = W_DIL_OUT
    outs, lses = [], []
    for g, (window, dil) in enumerate(DIL_CONFIGS):
        assert window // dil == BAND and seq % (dil * BAND) == 0
        u = seq // dil

        def split(col):
            a = lax.slice_in_dim(qkv3, col * LANES + g * gw, col * LANES + (g + 1) * gw, axis=2)
            return a.reshape(batch, u, dil, gw).transpose(0, 2, 1, 3).reshape(batch * dil, u, gw)

        o, lse = _band_attention(split(COL_DL[0]), split(COL_DL[1]), split(COL_DL[2]))
        merge = lambda a: a.reshape(batch, dil, u, gw).transpose(0, 2, 1, 3).reshape(t, gw)
        outs.append(merge(o))
        lses.append(merge(lse))
    tm = min(1024, t)
    spec = pl.BlockSpec((tm, gw), lambda i: (i, 0))
    return pl.pallas_call(
        _dil_mix_kernel,
        grid=(t // tm,),
        in_specs=[spec] * 6,
        out_specs=spec,
        out_shape=jax.ShapeDtypeStruct((t, gw), BF16),
        compiler_params=_cparams(("arbitrary",), 32),
        name="dilated_mix",
    )(*outs, *lses)


def _merge_kernel(h_ref, wg0, wg1, wg2, bg0, bg1, bg2, o0, o1, o2, wp0, wp1, wp2, out_ref):
    h = h_ref[...]

    def branch(wg, bg, o, wp):
        return jax.nn.sigmoid(_dot(h, wg[...]) + bg[...]) * _dot(o[...], wp[...])

    merged = branch(wg0, bg0, o0, wp0) + branch(wg1, bg1, o1, wp1) + branch(wg2, bg2, o2, wp2)
    out_ref[...] = merged.astype(out_ref.dtype)


def gated_merge(h, w_gate, b_gate, branch_outs, branch_projs):
    t, d = h.shape
    tm, tn = min(512, t), min(512, d)
    nj = d // tn
    b_gate2 = b_gate.reshape(1, -1)
    gate_w = [pl.BlockSpec((d, tn), lambda j, i, br=br: (0, br * nj + j)) for br in range(3)]
    gate_b = [pl.BlockSpec((1, tn), lambda j, i, br=br: (0, br * nj + j)) for br in range(3)]
    o_specs = [pl.BlockSpec((tm, o.shape[1]), lambda j, i: (i, 0)) for o in branch_outs]
    p_specs = [pl.BlockSpec((w.shape[0], tn), lambda j, i: (0, j)) for w in branch_projs]
    return pl.pallas_call(
        _merge_kernel,
        grid=(nj, t // tm),
        in_specs=[pl.BlockSpec((tm, d), lambda j, i: (i, 0))] + gate_w + gate_b + o_specs + p_specs,
        out_specs=pl.BlockSpec((tm, tn), lambda j, i: (i, j)),
        out_shape=jax.ShapeDtypeStruct((t, d), BF16),
        compiler_params=_cparams(("arbitrary", "arbitrary"), 48),
        name="gated_merge",
    )(h, w_gate, w_gate, w_gate, b_gate2, b_gate2, b_gate2, *branch_outs, *branch_projs)


def _outproj_kernel(m_ref, w_ref, x_ref, gate_ref, g_ref, b_ref, sc_ref, sh_ref, wrh_ref, wrl_ref, br_ref,
                    x1_ref, h2_ref, lg_ref, *, alpha):
    y = _dot(m_ref[...], w_ref[...])
    x1 = _ln(alpha * x_ref[...] + gate_ref[...] * y) * g_ref[...] + b_ref[...]
    x1_ref[...] = x1
    h2 = _ln(x1) * (1.0 + sc_ref[...]) + sh_ref[...]
    h2_ref[...] = h2
    hi = h2.astype(BF16)
    lo = (h2 - hi.astype(F32)).astype(BF16)
    wrh = wrh_ref[...]
    lg_ref[...] = _dot(hi, wrh) + _dot(lo, wrh) + _dot(hi, wrl_ref[...]) + br_ref[...]


def out_projection(merged, w_out, x2d, mod_l, ln_g, ln_b, w_router, b_router, seq, alpha):
    t, d = x2d.shape
    tm = min(256, seq)
    bpb = seq // tm
    n_exp = w_router.shape[1]
    wr = jnp.zeros((d, LANES), F32).at[:, :n_exp].set(w_router)
    wr_hi = wr.astype(BF16)
    wr_lo = (wr - wr_hi.astype(F32)).astype(BF16)
    br = jnp.zeros((1, LANES), F32).at[0, :n_exp].set(b_router)
    row = pl.BlockSpec((tm, d), lambda i: (i, 0))
    vec = pl.BlockSpec((1, d), lambda i: (0, 0))
    wr_spec = pl.BlockSpec((d, LANES), lambda i: (0, 0))
    return pl.pallas_call(
        functools.partial(_outproj_kernel, alpha=alpha),
        grid=(t // tm,),
        in_specs=[row, pl.BlockSpec((d, d), lambda i: (0, 0)), row, _mod_spec(2, d, bpb), vec, vec,
                  _mod_spec(4, d, bpb), _mod_spec(3, d, bpb), wr_spec, wr_spec,
                  pl.BlockSpec((1, LANES), lambda i: (0, 0))],
        out_specs=[row, row, pl.BlockSpec((tm, LANES), lambda i: (i, 0))],
        out_shape=[jax.ShapeDtypeStruct((t, d), F32), jax.ShapeDtypeStruct((t, d), F32),
                   jax.ShapeDtypeStruct((t, LANES), F32)],
        compiler_params=_cparams(("arbitrary",), 48),
        name="out_projection",
    )(merged, w_out, x2d, mod_l, ln_g.reshape(1, d), ln_b.reshape(1, d), mod_l, mod_l, wr_hi, wr_lo, br)


def _route_kernel(lg_ref, tri_ref, ir_ref, w_ref, cnt_ref, carry_ref, *, n_exp):
    @pl.when(pl.program_id(0) == 0)
    def _():
        carry_ref[...] = jnp.zeros_like(carry_ref)

    tr = lg_ref.shape[0]
    lane = lax.broadcasted_iota(jnp.int32, (tr, LANES), 1)
    logits = jnp.where(lane < n_exp, lg_ref[...], -jnp.inf)
    vals, sels, idxs = [], [], []
    for _ in range(TOP_K):
        m = jnp.max(logits, axis=1, keepdims=True)
        idx = jnp.min(jnp.where(logits == m, lane, LANES), axis=1, keepdims=True)
        sel = lane == idx
        vals.append(m)
        idxs.append(idx)
        sels.append(sel)
        logits = jnp.where(sel, -jnp.inf, logits)
    exps = [jnp.exp(v - vals[0]) for v in vals]
    tot = exps[0] + exps[1] + exps[2] + exps[3]
    onehot = jnp.zeros((tr, LANES), F32)
    for sel in sels:
        onehot = onehot + jnp.where(sel, 1.0, 0.0)
    before = _dot(tri_ref[...], onehot.astype(BF16)) + carry_ref[0:1, :]
    ir = jnp.zeros((tr, LANES), jnp.int32)
    wt = jnp.zeros((tr, LANES), F32)
    for k in range(TOP_K):
        rank = jnp.sum(jnp.where(sels[k], before, 0.0), axis=1, keepdims=True).astype(jnp.int32)
        ir = jnp.where(lane == k, idxs[k], ir)
        ir = jnp.where(lane == TOP_K + k, rank, ir)
        wt = jnp.where(lane == k, exps[k] / tot, wt)
    ir_ref[...] = ir
    w_ref[...] = wt
    total = before[-1:, :] + onehot[-1:, :]
    carry_ref[...] = jnp.broadcast_to(total, carry_ref.shape)
    cnt_ref[...] = jnp.broadcast_to(total, cnt_ref.shape)


def route(logits, n_exp):
    t = logits.shape[0]
    tr = min(512, t)
    tri = jnp.asarray((np.arange(tr)[:, None] > np.arange(tr)[None, :]).astype(np.float32), BF16)
    row = pl.BlockSpec((tr, LANES), lambda i: (i, 0))
    return pl.pallas_call(
        functools.partial(_route_kernel, n_exp=n_exp),
        grid=(t // tr,),
        in_specs=[row, pl.BlockSpec((tr, tr), lambda i: (0, 0))],
        out_specs=[row, row, pl.BlockSpec((SUBLANES, LANES), lambda i: (0, 0))],
        out_shape=[jax.ShapeDtypeStruct((t, LANES), jnp.int32), jax.ShapeDtypeStruct((t, LANES), F32),
                   jax.ShapeDtypeStruct((SUBLANES, LANES), F32)],
        scratch_shapes=[pltpu.VMEM((SUBLANES, LANES), F32)],
        compiler_params=_cparams(("arbitrary",), 32),
        name="route",
    )(logits, tri)


def _prep_up_kernel(w_ref, even_ref, odd_ref, g_ref, l_ref):
    w = w_ref[...].astype(BF16)
    g_ref[...] = _dot(w, even_ref[...]).astype(g_ref.dtype)
    l_ref[...] = _dot(w, odd_ref[...]).astype(l_ref.dtype)


def prepare_up_weights(w_up):
    d, f2 = w_up.shape[-2:]
    w3 = w_up.reshape(-1, d, f2)
    n = w3.shape[0]
    tn = min(512, f2)
    half = tn // 2
    even = np.zeros((tn, half), np.float32)
    odd = np.zeros((tn, half), np.float32)
    even[2 * np.arange(half), np.arange(half)] = 1.0
    odd[2 * np.arange(half) + 1, np.arange(half)] = 1.0
    sel = pl.BlockSpec((tn, half), lambda e, j: (0, 0))
    out_spec = pl.BlockSpec((None, d, half), lambda e, j: (e, 0, j))
    out_shape = jax.ShapeDtypeStruct((n, d, f2 // 2), BF16)
    return pl.pallas_call(
        _prep_up_kernel,
        grid=(n, f2 // tn),
        in_specs=[pl.BlockSpec((None, d, tn), lambda e, j: (e, 0, j)), sel, sel],
        out_specs=[out_spec, out_spec],
        out_shape=[out_shape, out_shape],
        compiler_params=_cparams(("arbitrary", "arbitrary"), 32),
        name="prepare_up_weights",
    )(w3, jnp.asarray(even, BF16), jnp.asarray(odd, BF16))


def _expert_kernel(be_ref, tok_cur_ref, tok_next_ref, h_ref, wg_ref, wl_ref, bg_ref, bl_ref, wd_ref, bd_ref,
                   y_ref, xbuf, sem, *, row_block):
    i = pl.program_id(0)
    n_blocks = pl.num_programs(0)
    slot = i % 2

    def block_wait(dst_slot):
        pltpu.make_async_copy(h_ref.at[pl.ds(0, row_block), :], xbuf.at[dst_slot], sem.at[dst_slot]).wait()

    def start_gather(tok_ref, dst_slot):
        for r in range(row_block):
            pltpu.make_async_copy(h_ref.at[pl.ds(tok_ref[0, r], 1), :], xbuf.at[dst_slot, pl.ds(r, 1), :],
                                  sem.at[dst_slot]).start(priority=r % 2)

    @pl.when(i == 0)
    def _():
        start_gather(tok_cur_ref, 0)

    block_wait(slot)
    x = xbuf[slot].astype(BF16)
    start_gather(tok_next_ref, 1 - slot)
    glu = jnp.minimum(_dot(x, wg_ref[...]) + bg_ref[...], SWIGLU_LIMIT)
    lin = jnp.clip(_dot(x, wl_ref[...]) + bl_ref[...], -SWIGLU_LIMIT, SWIGLU_LIMIT)
    act = glu * jax.nn.sigmoid(SWIGLU_ALPHA * glu) * (lin + 1.0)
    y_ref[...] = _dot(act.astype(BF16), wd_ref[...]) + bd_ref[...]

    @pl.when(i == n_blocks - 1)
    def _():
        block_wait(1 - slot)


def expert_ffn(h2, row_tok, block_e, w_glu, w_lin, up_offset, b_glu, b_lin, w_down, b_down, row_block):
    d = h2.shape[1]
    n_rows = row_tok.shape[0]
    n_blocks = n_rows // row_block
    n_exp = w_down.shape[0]
    f = w_glu.shape[2]
    once = pl.Buffered(1)
    tok3 = row_tok.reshape(n_blocks, 1, row_block)
    grid_spec = pltpu.PrefetchScalarGridSpec(
        num_scalar_prefetch=1,
        grid=(n_blocks,),
        in_specs=[pl.BlockSpec((None, 1, row_block), lambda i, be: (i, 0, 0), memory_space=pltpu.SMEM),
                  pl.BlockSpec((None, 1, row_block), lambda i, be: (jnp.minimum(i + 1, n_blocks - 1), 0, 0),
                               memory_space=pltpu.SMEM),
                  pl.BlockSpec(memory_space=pl.ANY),
                  pl.BlockSpec((None, d, f), lambda i, be: (be[i] + up_offset, 0, 0), pipeline_mode=once),
                  pl.BlockSpec((None, d, f), lambda i, be: (be[i] + up_offset, 0, 0), pipeline_mode=once),
                  pl.BlockSpec((None, 1, f), lambda i, be: (be[i], 0, 0)),
                  pl.BlockSpec((None, 1, f), lambda i, be: (be[i], 0, 0)),
                  pl.BlockSpec((None, f, d), lambda i, be: (be[i], 0, 0), pipeline_mode=once),
                  pl.BlockSpec((None, 1, d), lambda i, be: (be[i], 0, 0))],
        out_specs=pl.BlockSpec((row_block, d), lambda i, be: (i, 0)),
        scratch_shapes=[pltpu.VMEM((2, row_block, d), F32), pltpu.SemaphoreType.DMA((2,))])
    return pl.pallas_call(
        functools.partial(_expert_kernel, row_block=row_block),
        grid_spec=grid_spec,
        out_shape=jax.ShapeDtypeStruct((n_rows, d), F32),
        compiler_params=_cparams(("arbitrary",)),
        name="expert_ffn",
    )(block_e, tok3, tok3, h2, w_glu, w_lin, b_glu.reshape(n_exp, 1, f), b_lin.reshape(n_exp, 1, f),
      w_down, b_down.reshape(n_exp, 1, d))


def _combine_kernel(dest_cur_ref, dest_next_ref, ys_ref, w_ref, x_ref, gate_ref, g_ref, b_ref, sc_ref, sh_ref,
                    x2_ref, h_ref, buf, sem, *, tc, alpha):
    i = pl.program_id(0)
    n_tiles = pl.num_programs(0)
    slot = i % 2

    def start_gather(dest_ref, dst_slot):
        for tok in range(tc):
            for k in range(TOP_K):
                pltpu.make_async_copy(ys_ref.at[pl.ds(dest_ref[0, tok * TOP_K + k], 1), :],
                                      buf.at[dst_slot, k, pl.ds(tok, 1), :], sem.at[dst_slot]).start(priority=k % 2)

    def tile_wait(dst_slot):
        for k in range(TOP_K):
            pltpu.make_async_copy(ys_ref.at[pl.ds(0, tc), :], buf.at[dst_slot, k], sem.at[dst_slot]).wait()

    @pl.when(i == 0)
    def _():
        start_gather(dest_cur_ref, 0)

    tile_wait(slot)
    start_gather(dest_next_ref, 1 - slot)
    w = w_ref[...]
    y = w[:, 0:1] * buf[slot, 0]
    for k in range(1, TOP_K):
        y = y + w[:, k:k + 1] * buf[slot, k]
    x2 = _ln(alpha * x_ref[...] + gate_ref[...] * y) * g_ref[...] + b_ref[...]
    x2_ref[...] = x2
    h_ref[...] = (_ln(x2) * (1.0 + sc_ref[...]) + sh_ref[...]).astype(h_ref.dtype)

    @pl.when(i == n_tiles - 1)
    def _():
        tile_wait(1 - slot)


def moe_combine(ys, dest, top_w, x1, mod_l, mod_next, ln_g, ln_b, seq, alpha):
    t, d = x1.shape
    tc = min(128, seq)
    bpb = seq // tc
    row = pl.BlockSpec((tc, d), lambda i: (i, 0))
    vec = pl.BlockSpec((1, d), lambda i: (0, 0))
    n_tiles = t // tc
    dest3 = dest.reshape(n_tiles, 1, tc * TOP_K)
    return pl.pallas_call(
        functools.partial(_combine_kernel, tc=tc, alpha=alpha),
        grid=(n_tiles,),
        in_specs=[pl.BlockSpec((None, 1, tc * TOP_K), lambda i: (i, 0, 0), memory_space=pltpu.SMEM),
                  pl.BlockSpec((None, 1, tc * TOP_K), lambda i: (jnp.minimum(i + 1, n_tiles - 1), 0, 0),
                               memory_space=pltpu.SMEM),
                  pl.BlockSpec(memory_space=pl.ANY),
                  pl.BlockSpec((tc, LANES), lambda i: (i, 0)),
                  row, _mod_spec(5, d, bpb), vec, vec, _mod_spec(1, d, bpb), _mod_spec(0, d, bpb)],
        out_specs=[row, row],
        out_shape=[jax.ShapeDtypeStruct((t, d), F32), jax.ShapeDtypeStruct((t, d), BF16)],
        scratch_shapes=[pltpu.VMEM((2, TOP_K, tc, d), F32), pltpu.SemaphoreType.DMA((2,))],
        compiler_params=_cparams(("arbitrary",), 32),
        name="moe_combine",
    )(dest3, dest3, ys, top_w, x1, mod_l, ln_g.reshape(1, d), ln_b.reshape(1, d), mod_next, mod_next)


def moe_ffn(h2, logits, w_glu, w_lin, up_offset, b_glu, b_lin, w_down, b_down):
    t, d = h2.shape
    n_exp = w_down.shape[0]
    row_block = 256
    idx_rank, top_w, counts = route(logits, n_exp)
    top_idx = idx_rank[:, :TOP_K]
    rank = idx_rank[:, TOP_K:2 * TOP_K]
    counts = counts[0, :n_exp].astype(jnp.int32)
    padded = (counts + row_block - 1) // row_block * row_block
    padded_end = jnp.cumsum(padded)
    padded_start = padded_end - padded
    dest = padded_start[top_idx] + rank
    n_rows = t * TOP_K + n_exp * row_block
    tok_ids = jnp.broadcast_to(jnp.arange(t, dtype=jnp.int32)[:, None], (t, TOP_K))
    row_tok = jnp.zeros((n_rows,), jnp.int32).at[dest.reshape(-1)].set(tok_ids.reshape(-1), unique_indices=True)
    n_blocks = n_rows // row_block
    block_start = jnp.arange(n_blocks, dtype=jnp.int32) * row_block
    block_e = jnp.minimum(jnp.sum(padded_end[None, :] <= block_start[:, None], axis=1), n_exp - 1).astype(jnp.int32)
    ys = expert_ffn(h2, row_tok, block_e, w_glu, w_lin, up_offset, b_glu, b_lin, w_down, b_down, row_block)
    return ys, dest.astype(jnp.int32), top_w


def kernel(x, c, w_ada, b_ada, w_in, b_forget, w_gate, b_gate, w_proj_sb, w_proj_dil, w_proj_fox, w_out,
           ln1_g, ln1_b, w_router, b_router, w_up, b_up, w_down, b_down, ln2_g, ln2_b):
    batch, seq, d = x.shape
    depth = w_ada.shape[0]
    t = batch * seq
    alpha = (2.0 * depth) ** 0.25
    rope_tabs = rope_pair_tables(seq)
    mod = adaln_mod(c, w_ada, b_ada)
    x2d = x.reshape(t, d)
    h = ln_mod(x2d, mod[0], seq, comp_shift=0, comp_scale=1)
    n_exp = w_up.shape[1]
    w_glu, w_lin = prepare_up_weights(w_up)
    for l in range(depth):
        w_in_l = w_in[l]
        qkv = in_projection(h, w_in_l[:, :N_QKV].astype(BF16), rope_tabs, seq)
        q_feat, k_feat = forget_features(h, w_in_l[:, N_QKV:], b_forget[l], batch, seq)
        o_sb = stick_breaking_attention(qkv, batch, seq)
        o_dl = dilated_window_attention(qkv, batch, seq)
        o_fx = forgetting_attention(qkv, q_feat, k_feat, batch, seq)
        merged = gated_merge(h, w_gate[l].astype(BF16), b_gate[l], (o_sb, o_dl, o_fx),
                             (w_proj_sb[l].astype(BF16), w_proj_dil[l].astype(BF16),
                              w_proj_fox[l].astype(BF16)))
        x1, h2, logits = out_projection(merged, w_out[l].astype(BF16), x2d, mod[l], ln1_g[l], ln1_b[l],
                                        w_router[l], b_router[l], seq, alpha)
        ys, dest, top_w = moe_ffn(h2, logits, w_glu, w_lin, l * n_exp,
                                  b_up[l][:, 0::2], b_up[l][:, 1::2], w_down[l].astype(BF16), b_down[l])
        mod_next = mod[min(l + 1, depth - 1)]
        x2d, h = moe_combine(ys, dest, top_w, x1, mod[l], mod_next, ln2_g[l], ln2_b[l], seq, alpha)
    return x2d.reshape(batch, seq, d)
```

```python
import functools

import numpy as np
import jax
import jax.numpy as jnp
from jax import lax
from jax.experimental import pallas as pl
from jax.experimental.pallas import tpu as pltpu

F32 = jnp.float32
BF16 = jnp.bfloat16

HEAD_DIM = 64
H_SB = 8
DIL_CONFIGS = ((128, 1), (512, 4), (2048, 16))
H_DIL_PER_GROUP = 4
H_FOX = 12
ROPE_THETA = 500000.0
ROPE_DIMS = HEAD_DIM // 4
TOP_K = 4
SWIGLU_ALPHA = 1.702
SWIGLU_LIMIT = 7.0
LN_EPS = 1e-5
QK_SCALE = HEAD_DIM ** -0.5
LOG2E = 1.4426950408889634

LANES = 128
SUBLANES = 8
VMEM_BUDGET_MB = 56

N_DIL_GROUPS = len(DIL_CONFIGS)
W_SB = H_SB * HEAD_DIM
W_DIL = N_DIL_GROUPS * H_DIL_PER_GROUP * HEAD_DIM
W_DIL_OUT = H_DIL_PER_GROUP * HEAD_DIM
W_FOX = H_FOX * HEAD_DIM
N_QKV = 3 * (W_SB + W_DIL + W_FOX)
COL_SB = (0, W_SB // LANES, 2 * W_SB // LANES)
_DL0 = 3 * W_SB // LANES
COL_DL = (_DL0, _DL0 + W_DIL // LANES, _DL0 + 2 * W_DIL // LANES)
_FX0 = _DL0 + 3 * W_DIL // LANES
COL_FX = (_FX0, _FX0 + W_FOX // LANES, _FX0 + 2 * W_FOX // LANES)
BAND = 128
NEG_BIG = -1e30
EXP_UNDERFLOW = -104.0


def _cparams(semantics, vmem_mb=VMEM_BUDGET_MB):
    return pltpu.CompilerParams(dimension_semantics=semantics, vmem_limit_bytes=vmem_mb * 2 ** 20)


def _dot(a, b):
    return jnp.dot(a, b, preferred_element_type=F32)


def _dot_t(a, b):
    return lax.dot_general(a, b, (((1,), (1,)), ((), ())), preferred_element_type=F32)


def _ln(x):
    mu = jnp.mean(x, axis=-1, keepdims=True)
    xc = x - mu
    var = jnp.mean(xc * xc, axis=-1, keepdims=True)
    return xc * lax.rsqrt(var + LN_EPS)


def _softplus_neg_abs(z):
    return jnp.log1p(jnp.exp(-jnp.abs(z)))


def _split3(x):
    hi = x.astype(BF16)
    r1 = x - hi.astype(F32)
    mid = r1.astype(BF16)
    lo = (r1 - mid.astype(F32)).astype(BF16)
    return hi, mid, lo


def _mod_kernel(c_ref, w_ref, b_ref, o_ref):
    c = c_ref[...]
    act = (c * jax.nn.sigmoid(c)).astype(BF16)
    o_ref[...] = _dot(act, w_ref[...].astype(BF16)) + b_ref[...]


def adaln_mod(c, w_ada, b_ada):
    n_layers, d, n = w_ada.shape
    b = c.shape[0]
    assert b <= SUBLANES
    tn = 512
    c_pad = jnp.zeros((SUBLANES, d), F32).at[:b].set(c)
    out = pl.pallas_call(
        _mod_kernel,
        grid=(n_layers, n // tn),
        in_specs=[pl.BlockSpec((SUBLANES, d), lambda l, j: (0, 0)),
                  pl.BlockSpec((None, d, tn), lambda l, j: (l, 0, j)),
                  pl.BlockSpec((None, 1, tn), lambda l, j: (l, 0, j))],
        out_specs=pl.BlockSpec((None, SUBLANES, tn), lambda l, j: (l, 0, j)),
        out_shape=jax.ShapeDtypeStruct((n_layers, SUBLANES, n), F32),
        compiler_params=_cparams(("arbitrary", "arbitrary"), 32),
        name="adaln_mod",
    )(c_pad, w_ada, b_ada.reshape(n_layers, 1, n))
    return out[:, :b].reshape(n_layers, b, 6, d).transpose(0, 2, 1, 3)[:, :, :, None, :]


def _mod_spec(comp, d, rows_per_batch_blocks):
    return pl.BlockSpec((None, None, 1, d), lambda i, *_: (comp, i // rows_per_batch_blocks, 0, 0))


def _ln_mod_kernel(x_ref, sc_ref, sh_ref, h_ref):
    h_ref[...] = (_ln(x_ref[...]) * (1.0 + sc_ref[...]) + sh_ref[...]).astype(h_ref.dtype)


def ln_mod(x2d, mod_l, seq, comp_shift, comp_scale):
    t, d = x2d.shape
    ts = min(512, seq)
    bpb = seq // ts
    return pl.pallas_call(
        _ln_mod_kernel,
        grid=(t // ts,),
        in_specs=[pl.BlockSpec((ts, d), lambda i: (i, 0)),
                  _mod_spec(comp_scale, d, bpb), _mod_spec(comp_shift, d, bpb)],
        out_specs=pl.BlockSpec((ts, d), lambda i: (i, 0)),
        out_shape=jax.ShapeDtypeStruct((t, d), BF16),
        compiler_params=_cparams(("arbitrary",), 32),
        name="ln_mod",
    )(x2d, mod_l, mod_l)


def _inproj_kernel(h_ref, w_ref, scale_ref, cos_ref, s1_ref, s2_ref, o_ref, *, rope_lo, rope_hi, tn):
    j = pl.program_id(1)
    acc = _dot(h_ref[...], w_ref[...])
    is_rope = jnp.logical_and(j >= rope_lo, j < rope_hi)

    @pl.when(is_rope)
    def _():
        c, s1, s2 = cos_ref[...], s1_ref[...], s2_ref[...]
        for blk in range(tn // LANES):
            a = acc[:, blk * LANES:(blk + 1) * LANES]
            r = a * c + pltpu.roll(a, LANES - ROPE_DIMS // 2, 1) * s1 + pltpu.roll(a, ROPE_DIMS // 2, 1) * s2
            o_ref[:, blk * LANES:(blk + 1) * LANES] = r.astype(o_ref.dtype)

    @pl.when(jnp.logical_not(is_rope))
    def _():
        o_ref[...] = (acc * scale_ref[...]).astype(o_ref.dtype)


def rope_pair_tables(seq):
    pos = jnp.arange(seq, dtype=F32)
    inv = ROPE_THETA ** (-jnp.arange(0, ROPE_DIMS, 2, dtype=F32) / ROPE_DIMS)
    ang = pos[:, None] * inv[None, :]
    cos, sin = jnp.cos(ang), jnp.sin(ang)
    half = ROPE_DIMS // 2
    head_c = jnp.concatenate([cos, cos, jnp.ones((seq, HEAD_DIM - ROPE_DIMS), F32)], axis=1)
    head_s1 = jnp.concatenate([-sin, jnp.zeros((seq, HEAD_DIM - half), F32)], axis=1)
    head_s2 = jnp.concatenate([jnp.zeros((seq, half), F32), sin,
                               jnp.zeros((seq, HEAD_DIM - ROPE_DIMS), F32)], axis=1)
    two = lambda a: jnp.concatenate([a, a], axis=1)
    return two(head_c), two(head_s1), two(head_s2)


def in_projection(h, w_qkv, col_scale, rope_tabs, seq):
    t, d = h.shape
    n = w_qkv.shape[1]
    tm = min(1024, seq)
    tn = 512
    assert seq % tm == 0 and n % tn == 0
    rope_lo, rope_hi = COL_DL[0] * LANES, COL_DL[2] * LANES
    assert rope_lo % tn == 0 and rope_hi % tn == 0
    spb = seq // tm
    tab_spec = pl.BlockSpec((tm, LANES), lambda i, j: (i % spb, 0))
    return pl.pallas_call(
        functools.partial(_inproj_kernel, rope_lo=rope_lo // tn, rope_hi=rope_hi // tn, tn=tn),
        grid=(t // tm, n // tn),
        in_specs=[pl.BlockSpec((tm, d), lambda i, j: (i, 0)),
                  pl.BlockSpec((d, tn), lambda i, j: (0, j)),
                  pl.BlockSpec((1, tn), lambda i, j: (0, j)),
                  tab_spec, tab_spec, tab_spec],
        out_specs=pl.BlockSpec((tm, tn), lambda i, j: (i, j)),
        out_shape=jax.ShapeDtypeStruct((t, n), BF16),
        compiler_params=_cparams(("arbitrary", "arbitrary"), 48),
        name="in_projection",
    )(h, w_qkv, col_scale, *rope_tabs)


FEAT_PER_HEAD = 6


def _forget_feature_maps():
    width = (H_FOX // 2) * LANES
    pq = np.zeros((3 * LANES, width), np.float32)
    pk = np.zeros((3 * LANES, width), np.float32)
    cq = np.zeros((1, width), np.float32)
    ck = np.zeros((1, width), np.float32)
    for head in range(H_FOX):
        base = (head // 2) * LANES + (head % 2) * FEAT_PER_HEAD
        for piece in range(3):
            pq[piece * LANES + head, base + piece] = 1.0
            pk[piece * LANES + head, base + 3 + piece] = -1.0
        cq[0, base + 3:base + 6] = 1.0
        ck[0, base:base + 3] = 1.0
    return pq, pk, cq, ck


def _forget_kernel(h_ref, w_ref, b_ref, tri_ref, pq_ref, pk_ref, cq_ref, ck_ref, qf_ref, kf_ref, carry_ref):
    @pl.when(pl.program_id(1) == 0)
    def _():
        carry_ref[...] = jnp.zeros_like(carry_ref)

    f = _dot(h_ref[...], w_ref[...]) + b_ref[...]
    log_f = jnp.minimum(f, 0.0) - _softplus_neg_abs(f)
    tri = tri_ref[...]
    hi, mid, lo = _split3(log_f)
    cum = _dot(tri, hi) + _dot(tri, mid) + _dot(tri, lo) + carry_ref[0:1, :]
    carry_ref[...] = jnp.broadcast_to(cum[-1:, :], carry_ref.shape)
    pieces = jnp.concatenate(_split3(cum * LOG2E), axis=1)
    qf_ref[...] = (_dot(pieces, pq_ref[...]) + cq_ref[...]).astype(qf_ref.dtype)
    kf_ref[...] = (_dot(pieces, pk_ref[...]) + ck_ref[...]).astype(kf_ref.dtype)


def forget_features(h, w_f, b_f, batch, seq):
    t, d = h.shape
    ts = min(256, seq)
    nsb = seq // ts
    w_pad = jnp.zeros((d, LANES), BF16).at[:, :H_FOX].set(w_f.astype(BF16))
    b_pad = jnp.zeros((1, LANES), F32).at[0, :H_FOX].set(b_f)
    tri = (np.arange(ts)[:, None] >= np.arange(ts)[None, :]).astype(np.float32)
    pq, pk, cq, ck = _forget_feature_maps()
    width = pq.shape[1]
    const = lambda shape: pl.BlockSpec(shape, lambda b, s: (0, 0))
    out_spec = pl.BlockSpec((ts, width), lambda b, s: (b * nsb + s, 0))
    out_shape = jax.ShapeDtypeStruct((t, width), BF16)
    return pl.pallas_call(
        _forget_kernel,
        grid=(batch, nsb),
        in_specs=[pl.BlockSpec((ts, d), lambda b, s: (b * nsb + s, 0)),
                  const((d, LANES)), const((1, LANES)), const((ts, ts)),
                  const(pq.shape), const(pk.shape), const(cq.shape), const(ck.shape)],
        out_specs=[out_spec, out_spec],
        out_shape=[out_shape, out_shape],
        scratch_shapes=[pltpu.VMEM((SUBLANES, LANES), F32)],
        compiler_params=_cparams(("arbitrary", "arbitrary"), 32),
        name="forget_features",
    )(h, w_pad, b_pad, jnp.asarray(tri, BF16), jnp.asarray(pq, BF16), jnp.asarray(pk, BF16),
      jnp.asarray(cq), jnp.asarray(ck))


def _pair_masks(tq):
    lane = lax.broadcasted_iota(jnp.int32, (tq, LANES), 1)
    return lane < HEAD_DIM


KEY_SUB = 128
QRY_SUB = 128
UP_CHUNK = 256
SUFFIX_SUB = 256


def _diag_tile_kind(ks, qs, key_sub, qry_sub, strict, key_off):
    if key_off is None:
        return 'full'
    k_lo, k_hi = key_off + ks * key_sub, key_off + ks * key_sub + key_sub - 1
    q_lo, q_hi = qs * qry_sub, qs * qry_sub + qry_sub - 1
    if strict:
        if k_lo >= q_hi:
            return 'dead'
        return 'full' if k_hi < q_lo else 'partial'
    if k_lo > q_hi:
        return 'dead'
    return 'full' if k_hi <= q_lo else 'partial'


def _tile_allowed(ks, qs, key_sub, qry_sub, strict, key_off):
    kpos = key_off + ks * key_sub + lax.broadcasted_iota(jnp.int32, (key_sub, qry_sub), 0)
    qpos = qs * qry_sub + lax.broadcasted_iota(jnp.int32, (key_sub, qry_sub), 1)
    return kpos < qpos if strict else kpos <= qpos


def _pair_transpose_in(qkv3, col, width):
    return lax.slice_in_dim(qkv3, col * LANES, col * LANES + width, axis=2).transpose(0, 2, 1)


def _sb_kernel(q_ref, k_ref, vt_ref, later_ref, o_ref,
               qx_s, acc_s, run_s, first_s, z_s, hi_s, lo_s, btw_s, w_s, *, tq, tk, key_sub, qry_sub):
    q_blk = pl.program_id(2)
    head_a = _pair_masks(tq)
    q = q_ref[...] * QK_SCALE
    qx_s[0] = jnp.where(head_a, q, 0).astype(BF16)
    qx_s[1] = jnp.where(head_a, 0, q).astype(BF16)
    acc_s[...] = jnp.zeros_like(acc_s)
    run_s[...] = jnp.zeros_like(run_s)

    def step(key_off, kv_blk):
        k_ref_blk = k_ref.at[pl.ds(pl.multiple_of(kv_blk * tk, tk), tk), :]
        vt_blk = vt_ref.at[kv_blk]
        _sb_block(key_off, k_ref_blk, vt_blk)

    def _sb_block(key_off, k_ref, vt_ref):
        suf = later_ref.shape[0]
        n_suf = tk // suf
        per_suf = suf // key_sub
        tiles = [(ks, qs) for ks in range(tk // key_sub) for qs in range(tq // qry_sub)]
        kind_of = lambda ks, qs: _diag_tile_kind(ks, qs, key_sub, qry_sub, True, key_off)
        k = k_ref[...]
        for head in range(2):
            z_s[head] = _dot_t(k, qx_s[head])
        for head in range(2):
            for ks, qs in tiles:
                kl = slice(ks * key_sub, (ks + 1) * key_sub)
                ql = slice(qs * qry_sub, (qs + 1) * qry_sub)
                first_row = head * n_suf + ks // per_suf
                kind = kind_of(ks, qs)
                if kind == 'dead':
                    hi_s[head, kl, ql] = jnp.zeros((key_sub, qry_sub), BF16)
                    lo_s[head, kl, ql] = jnp.zeros((key_sub, qry_sub), BF16)
                    if ks % per_suf == 0:
                        first_s[first_row:first_row + 1, ql] = jnp.zeros((1, qry_sub), F32)
                    continue
                z = z_s[head, kl, ql]
                log_stop = jnp.minimum(z, 0.0) - jnp.log(1.0 + jnp.exp(-jnp.abs(z)))
                log_cont = log_stop - z
                if kind == 'partial':
                    log_cont = jnp.where(_tile_allowed(ks, qs, key_sub, qry_sub, True, key_off), log_cont, 0.0)
                z_s[head, kl, ql] = log_stop
                hi = log_cont.astype(BF16)
                hi_s[head, kl, ql] = hi
                lo_s[head, kl, ql] = (log_cont - hi.astype(F32)).astype(BF16)
                if ks % per_suf == 0:
                    first_s[first_row:first_row + 1, ql] = log_cont[0:1, :]
        later = later_ref[...]
        for head in range(2):
            for blk in range(n_suf):
                bl = slice(blk * suf, (blk + 1) * suf)
                btw_s[head, bl, :] = _dot(later, hi_s[head, bl, :]) + _dot(later, lo_s[head, bl, :])

        def block_sum(head, blk, ql):
            row = head * n_suf + blk
            return btw_s[head, blk * suf:blk * suf + 1, ql] + first_s[row:row + 1, ql]

        for head in range(2):
            for ks, qs in tiles:
                kl = slice(ks * key_sub, (ks + 1) * key_sub)
                ql = slice(qs * qry_sub, (qs + 1) * qry_sub)
                kind = kind_of(ks, qs)
                if kind == 'dead':
                    w_s[head, kl, ql] = jnp.zeros((key_sub, qry_sub), BF16)
                    continue
                after = run_s[head:head + 1, ql]
                for blk in range(ks // per_suf + 1, n_suf):
                    after = after + block_sum(head, blk, ql)
                w = jnp.exp(z_s[head, kl, ql] + btw_s[head, kl, ql] + after)
                if kind == 'partial':
                    w = jnp.where(_tile_allowed(ks, qs, key_sub, qry_sub, True, key_off), w, 0.0)
                w_s[head, kl, ql] = w.astype(BF16)
        for head in range(2):
            rows = slice(head * HEAD_DIM, (head + 1) * HEAD_DIM)
            acc_s[rows, :] += _dot(vt_ref[rows, :], w_s[head])
            total = run_s[head:head + 1, :]
            for blk in range(n_suf):
                total = total + block_sum(head, blk, slice(None))
            run_s[head:head + 1, :] = total

    def all_weights_zero():
        return jnp.max(run_s[0:2, :]) < EXP_UNDERFLOW

    step(0, q_blk)

    def more(carry):
        kv_blk, dead = carry
        return jnp.logical_and(kv_blk >= 0, jnp.logical_not(dead))

    def visit(carry):
        kv_blk, _ = carry
        step(None, kv_blk)
        return kv_blk - 1, all_weights_zero()

    lax.while_loop(more, visit, (q_blk - 1, all_weights_zero()))
    o_ref[...] = acc_s[...].astype(o_ref.dtype)


def stick_breaking_attention(qkv, batch, seq):
    tq = tk = min(256, seq)
    key_sub, qry_sub = min(KEY_SUB, tk), min(QRY_SUB, tq)
    n_pairs = W_SB // LANES
    n_kv = seq // tk
    qkv3 = qkv.reshape(batch, seq, qkv.shape[-1])
    qc, kc, vc = COL_SB
    v_t = _pair_transpose_in(qkv3, vc, W_SB).reshape(batch, n_pairs, LANES, n_kv, tk).transpose(0, 1, 3, 2, 4)
    suf = min(SUFFIX_SUB, tk)
    assert 2 * (tk // suf) <= SUBLANES and suf % key_sub == 0
    later = jnp.asarray((np.arange(suf)[None, :] > np.arange(suf)[:, None]).astype(np.float32), BF16)
    block_f32 = pltpu.VMEM((2, tk, tq), F32)
    block_bf16 = pltpu.VMEM((2, tk, tq), BF16)
    row_stats = pltpu.VMEM((SUBLANES, tq), F32)
    out_t = pl.pallas_call(
        functools.partial(_sb_kernel, tq=tq, tk=tk, key_sub=key_sub, qry_sub=qry_sub),
        grid=(batch, n_pairs, seq // tq),
        in_specs=[pl.BlockSpec((None, tq, LANES), lambda b, p, i: (b, i, qc + p)),
                  pl.BlockSpec((None, seq, LANES), lambda b, p, i: (b, 0, kc + p)),
                  pl.BlockSpec((None, None, n_kv, LANES, tk), lambda b, p, i: (b, p, 0, 0, 0)),
                  pl.BlockSpec((suf, suf), lambda b, p, i: (0, 0))],
        out_specs=pl.BlockSpec((None, LANES, tq), lambda b, p, i: (b, p, i)),
        scratch_shapes=[pltpu.VMEM((2, tq, LANES), BF16), pltpu.VMEM((LANES, tq), F32),
                        row_stats, row_stats, block_f32, block_bf16, block_bf16, block_f32, block_bf16],
        out_shape=jax.ShapeDtypeStruct((batch, W_SB, seq), BF16),
        compiler_params=_cparams(("arbitrary", "arbitrary", "arbitrary"), 32),
        name="stick_breaking_attention",
    )(qkv3, qkv3, v_t, later)
    return out_t.transpose(0, 2, 1).reshape(batch * seq, W_SB)


def _fox_kernel(q_ref, qf_ref, k_ref, kf_ref, vt_ref, o_ref,
                qx_s, acc_s, m_s, l_s, alpha_s, z_s, p_s, *, tq, tk, key_sub, qry_sub):
    q_blk = pl.program_id(2)
    lane = lax.broadcasted_iota(jnp.int32, (tq, LANES), 1)
    q = q_ref[...]
    qf = qf_ref[...]
    for head in range(2):
        in_head = (lane >= head * HEAD_DIM) & (lane < (head + 1) * HEAD_DIM)
        in_feat = (lane >= head * FEAT_PER_HEAD) & (lane < (head + 1) * FEAT_PER_HEAD)
        qx_s[head] = jnp.concatenate([jnp.where(in_head, q, 0).astype(BF16),
                                      jnp.where(in_feat, qf, 0).astype(BF16)], axis=1)
    acc_s[...] = jnp.zeros_like(acc_s)
    l_s[...] = jnp.zeros_like(l_s)
    m_s[...] = jnp.full_like(m_s, NEG_BIG)

    def step(key_off, kv_blk):
        rows = pl.ds(pl.multiple_of(kv_blk * tk, tk), tk)
        _fox_block(key_off, k_ref.at[rows, :], kf_ref.at[rows, :], vt_ref.at[kv_blk])

    def _fox_block(key_off, k_ref, kf_ref, vt_ref):
        kind_of = lambda ks, qs: _diag_tile_kind(ks, qs, key_sub, qry_sub, False, key_off)
        kx = jnp.concatenate([k_ref[...], kf_ref[...]], axis=1)
        for head in range(2):
            z_s[head] = _dot_t(kx, qx_s[head])
        for head in range(2):
            for qs in range(tq // qry_sub):
                ql = slice(qs * qry_sub, (qs + 1) * qry_sub)

                def score(ks):
                    z = z_s[head, ks * key_sub:(ks + 1) * key_sub, ql]
                    if kind_of(ks, qs) == 'partial':
                        z = jnp.where(_tile_allowed(ks, qs, key_sub, qry_sub, False, key_off), z, NEG_BIG)
                    return z

                live = [ks for ks in range(tk // key_sub) if kind_of(ks, qs) != 'dead']
                m_old = m_s[head:head + 1, ql]
                m_new = m_old
                for ks in live:
                    m_new = jnp.maximum(m_new, jnp.max(score(ks), axis=0, keepdims=True))
                alpha = jnp.exp2(m_old - m_new)
                total = alpha * l_s[head:head + 1, ql]
                for ks in range(tk // key_sub):
                    kl = slice(ks * key_sub, (ks + 1) * key_sub)
                    if ks not in live:
                        p_s[head, kl, ql] = jnp.zeros((key_sub, qry_sub), BF16)
                        continue
                    prob = jnp.exp2(score(ks) - m_new)
                    total = total + jnp.sum(prob, axis=0, keepdims=True)
                    p_s[head, kl, ql] = prob.astype(BF16)
                m_s[head:head + 1, ql] = m_new
                l_s[head:head + 1, ql] = total
                alpha_s[head:head + 1, ql] = alpha
        for head in range(2):
            rows = slice(head * HEAD_DIM, (head + 1) * HEAD_DIM)
            acc_s[rows, :] = alpha_s[head:head + 1, :] * acc_s[rows, :] + _dot(vt_ref[rows, :], p_s[head])

    per_q = tq // tk
    first_diag = q_blk * per_q

    def visit(kv_blk, carry):
        step(None, kv_blk)
        return carry

    lax.fori_loop(0, first_diag, visit, 0)
    for variant in range(per_q):
        step(variant * tk, first_diag + variant)
    for head in range(2):
        rows = slice(head * HEAD_DIM, (head + 1) * HEAD_DIM)
        o_ref[rows, :] = (acc_s[rows, :] / l_s[head:head + 1, :]).astype(o_ref.dtype)


def forgetting_attention(qkv, q_feat, k_feat, batch, seq):
    tk = min(512, seq)
    tq = min(1024, seq)
    key_sub, qry_sub = min(KEY_SUB, tk), min(QRY_SUB, tq)
    assert tq % tk == 0
    n_pairs = W_FOX // LANES
    n_kv = seq // tk
    qkv3 = qkv.reshape(batch, seq, qkv.shape[-1])
    qf3 = q_feat.reshape(batch, seq, n_pairs * LANES)
    kf3 = k_feat.reshape(batch, seq, n_pairs * LANES)
    qc, kc, vc = COL_FX
    v_t = _pair_transpose_in(qkv3, vc, W_FOX).reshape(batch, n_pairs, LANES, n_kv, tk).transpose(0, 1, 3, 2, 4)
    out_t = pl.pallas_call(
        functools.partial(_fox_kernel, tq=tq, tk=tk, key_sub=key_sub, qry_sub=qry_sub),
        grid=(batch, n_pairs, seq // tq),
        in_specs=[pl.BlockSpec((None, tq, LANES), lambda b, p, i: (b, i, qc + p)),
                  pl.BlockSpec((None, tq, LANES), lambda b, p, i: (b, i, p)),
                  pl.BlockSpec((None, seq, LANES), lambda b, p, i: (b, 0, kc + p)),
                  pl.BlockSpec((None, seq, LANES), lambda b, p, i: (b, 0, p)),
                  pl.BlockSpec((None, None, n_kv, LANES, tk), lambda b, p, i: (b, p, 0, 0, 0))],
        out_specs=pl.BlockSpec((None, LANES, tq), lambda b, p, i: (b, p, i)),
        scratch_shapes=[pltpu.VMEM((2, tq, 2 * LANES), BF16), pltpu.VMEM((LANES, tq), F32),
                        pltpu.VMEM((SUBLANES, tq), F32), pltpu.VMEM((SUBLANES, tq), F32),
                        pltpu.VMEM((SUBLANES, tq), F32), pltpu.VMEM((2, tk, tq), F32),
                        pltpu.VMEM((2, tk, tq), BF16)],
        out_shape=jax.ShapeDtypeStruct((batch, W_FOX, seq), BF16),
        compiler_params=_cparams(("arbitrary", "arbitrary", "arbitrary"), 40),
        name="forgetting_attention",
    )(qkv3, qf3, qkv3, kf3, v_t)
    return out_t.transpose(0, 2, 1).reshape(batch * seq, W_FOX)


def _band_kernel(q_ref, kc_ref, vc_ref, kp_ref, vp_ref, o_ref, lse_ref, *, tq):
    i = pl.program_id(2)
    head_a = _pair_masks(tq)
    q = q_ref[...] * QK_SCALE
    kc, vc, kp, vp = kc_ref[...], vc_ref[...], kp_ref[...], vp_ref[...]
    diff = lax.broadcasted_iota(jnp.int32, (tq, tq), 0) - lax.broadcasted_iota(jnp.int32, (tq, tq), 1)
    mask_c = jnp.logical_and(diff >= 0, diff <= BAND)
    rp = lax.broadcasted_iota(jnp.int32, (tq, BAND), 0)
    cp = lax.broadcasted_iota(jnp.int32, (tq, BAND), 1)
    mask_p = jnp.logical_and(cp >= rp, i > 0)
    outs, lses = [], []
    for sel in (head_a, jnp.logical_not(head_a)):
        qh = jnp.where(sel, q, 0).astype(BF16)
        sc = jnp.where(mask_c, _dot_t(qh, kc), NEG_BIG)
        sp = jnp.where(mask_p, _dot_t(qh, kp), NEG_BIG)
        m = jnp.maximum(jnp.max(sc, axis=1, keepdims=True), jnp.max(sp, axis=1, keepdims=True))
        ec = jnp.exp(sc - m)
        ep = jnp.exp(sp - m)
        den = jnp.sum(ec, axis=1, keepdims=True) + jnp.sum(ep, axis=1, keepdims=True)
        outs.append((_dot(ec.astype(BF16), vc) + _dot(ep.astype(BF16), vp)) / den)
        lses.append(m + jnp.log(den))
    o_ref[...] = jnp.where(head_a, outs[0], outs[1])
    lse_ref[...] = jnp.where(head_a, lses[0], lses[1])


def _band_attention(q, k, v):
    nb, u, w = q.shape
    tq = min(256, u)
    assert u % tq == 0 and tq % BAND == 0
    sub = tq // BAND
    cur = pl.BlockSpec((None, tq, LANES), lambda n, p, i: (n, i, p))
    prev = pl.BlockSpec((None, BAND, LANES), lambda n, p, i: (n, jnp.maximum(i * sub - 1, 0), p))
    shp = jax.ShapeDtypeStruct((nb, u, w), F32)
    return pl.pallas_call(
        functools.partial(_band_kernel, tq=tq),
        grid=(nb, w // LANES, u // tq),
        in_specs=[cur, cur, cur, prev, prev],
        out_specs=[cur, cur],
        out_shape=[shp, shp],
        compiler_params=_cparams(("arbitrary", "arbitrary", "arbitrary"), 32),
        name="band_attention",
    )(q, k, v, k, v)


def _dil_mix_kernel(o0, o1, o2, l0, l1, l2, out_ref):
    a, b, c = l0[...], l1[...], l2[...]
    m = jnp.maximum(jnp.maximum(a, b), c)
    ea, eb, ec = jnp.exp(a - m), jnp.exp(b - m), jnp.exp(c - m)
    tot = ea + eb + ec
    out_ref[...] = ((o0[...] * ea + o1[...] * eb + o2[...] * ec) / tot).astype(out_ref.dtype)


def dilated_window_attention(qkv, batch, seq):
    t = batch * seq
    qkv3 = qkv.reshape(batch, seq, qkv.shape[-1])
    gw = W_DIL_OUT
    outs, lses = [], []
    for g, (window, dil) in enumerate(DIL_CONFIGS):
        assert window // dil == BAND and seq % (dil * BAND) == 0
        u = seq // dil

        def split(col):
            a = lax.slice_in_dim(qkv3, col * LANES + g * gw, col * LANES + (g + 1) * gw, axis=2)
            return a.reshape(batch, u, dil, gw).transpose(0, 2, 1, 3).reshape(batch * dil, u, gw)

        o, lse = _band_attention(split(COL_DL[0]), split(COL_DL[1]), split(COL_DL[2]))
        merge = lambda a: a.reshape(batch, dil, u, gw).transpose(0, 2, 1, 3).reshape(t, gw)
        outs.append(merge(o))
        lses.append(merge(lse))
    tm = min(1024, t)
    spec = pl.BlockSpec((tm, gw), lambda i: (i, 0))
    return pl.pallas_call(
        _dil_mix_kernel,
        grid=(t // tm,),
        in_specs=[spec] * 6,
        out_specs=spec,
        out_shape=jax.ShapeDtypeStruct((t, gw), BF16),
        compiler_params=_cparams(("arbitrary",), 32),
        name="dilated_mix",
    )(*outs, *lses)


def _merge_kernel(h_ref, wg0, wg1, wg2, bg0, bg1, bg2, o0, o1, o2, wp0, wp1, wp2, out_ref):
    h = h_ref[...]

    def branch(wg, bg, o, wp):
        return jax.nn.sigmoid(_dot(h, wg[...]) + bg[...]) * _dot(o[...], wp[...])

    merged = branch(wg0, bg0, o0, wp0) + branch(wg1, bg1, o1, wp1) + branch(wg2, bg2, o2, wp2)
    out_ref[...] = merged.astype(out_ref.dtype)


def gated_merge(h, w_gate, b_gate, branch_outs, branch_projs):
    t, d = h.shape
    tm, tn = min(512, t), min(512, d)
    nj = d // tn
    b_gate2 = b_gate.reshape(1, -1)
    gate_w = [pl.BlockSpec((d, tn), lambda j, i, br=br: (0, br * nj + j)) for br in range(3)]
    gate_b = [pl.BlockSpec((1, tn), lambda j, i, br=br: (0, br * nj + j)) for br in range(3)]
    o_specs = [pl.BlockSpec((tm, o.shape[1]), lambda j, i: (i, 0)) for o in branch_outs]
    p_specs = [pl.BlockSpec((w.shape[0], tn), lambda j, i: (0, j)) for w in branch_projs]
    return pl.pallas_call(
        _merge_kernel,
        grid=(nj, t // tm),
        in_specs=[pl.BlockSpec((tm, d), lambda j, i: (i, 0))] + gate_w + gate_b + o_specs + p_specs,
        out_specs=pl.BlockSpec((tm, tn), lambda j, i: (i, j)),
        out_shape=jax.ShapeDtypeStruct((t, d), BF16),
        compiler_params=_cparams(("arbitrary", "arbitrary"), 48),
        name="gated_merge",
    )(h, w_gate, w_gate, w_gate, b_gate2, b_gate2, b_gate2, *branch_outs, *branch_projs)


def _outproj_kernel(m_ref, w_ref, x_ref, gate_ref, g_ref, b_ref, sc_ref, sh_ref, wrh_ref, wrl_ref, br_ref,
                    x1_ref, h2_ref, lg_ref, *, alpha):
    y = _dot(m_ref[...], w_ref[...])
    x1 = _ln(alpha * x_ref[...] + gate_ref[...] * y) * g_ref[...] + b_ref[...]
    x1_ref[...] = x1
    h2 = _ln(x1) * (1.0 + sc_ref[...]) + sh_ref[...]
    h2_ref[...] = h2
    hi = h2.astype(BF16)
    lo = (h2 - hi.astype(F32)).astype(BF16)
    wrh = wrh_ref[...]
    lg_ref[...] = _dot(hi, wrh) + _dot(lo, wrh) + _dot(hi, wrl_ref[...]) + br_ref[...]


def out_projection(merged, w_out, x2d, mod_l, ln_g, ln_b, w_router, b_router, seq, alpha):
    t, d = x2d.shape
    tm = min(256, seq)
    bpb = seq // tm
    n_exp = w_router.shape[1]
    wr = jnp.zeros((d, LANES), F32).at[:, :n_exp].set(w_router)
    wr_hi = wr.astype(BF16)
    wr_lo = (wr - wr_hi.astype(F32)).astype(BF16)
    br = jnp.zeros((1, LANES), F32).at[0, :n_exp].set(b_router)
    row = pl.BlockSpec((tm, d), lambda i: (i, 0))
    vec = pl.BlockSpec((1, d), lambda i: (0, 0))
    wr_spec = pl.BlockSpec((d, LANES), lambda i: (0, 0))
    return pl.pallas_call(
        functools.partial(_outproj_kernel, alpha=alpha),
        grid=(t // tm,),
        in_specs=[row, pl.BlockSpec((d, d), lambda i: (0, 0)), row, _mod_spec(2, d, bpb), vec, vec,
                  _mod_spec(4, d, bpb), _mod_spec(3, d, bpb), wr_spec, wr_spec,
                  pl.BlockSpec((1, LANES), lambda i: (0, 0))],
        out_specs=[row, row, pl.BlockSpec((tm, LANES), lambda i: (i, 0))],
        out_shape=[jax.ShapeDtypeStruct((t, d), F32), jax.ShapeDtypeStruct((t, d), F32),
                   jax.ShapeDtypeStruct((t, LANES), F32)],
        compiler_params=_cparams(("arbitrary",), 48),
        name="out_projection",
    )(merged, w_out, x2d, mod_l, ln_g.reshape(1, d), ln_b.reshape(1, d), mod_l, mod_l, wr_hi, wr_lo, br)


def _route_kernel(lg_ref, tri_ref, ir_ref, w_ref, cnt_ref, carry_ref, *, n_exp):
    @pl.when(pl.program_id(0) == 0)
    def _():
        carry_ref[...] = jnp.zeros_like(carry_ref)

    tr = lg_ref.shape[0]
    lane = lax.broadcasted_iota(jnp.int32, (tr, LANES), 1)
    logits = jnp.where(lane < n_exp, lg_ref[...], -jnp.inf)
    vals, sels, idxs = [], [], []
    for _ in range(TOP_K):
        m = jnp.max(logits, axis=1, keepdims=True)
        idx = jnp.min(jnp.where(logits == m, lane, LANES), axis=1, keepdims=True)
        sel = lane == idx
        vals.append(m)
        idxs.append(idx)
        sels.append(sel)
        logits = jnp.where(sel, -jnp.inf, logits)
    exps = [jnp.exp(v - vals[0]) for v in vals]
    tot = exps[0] + exps[1] + exps[2] + exps[3]
    onehot = jnp.zeros((tr, LANES), F32)
    for sel in sels:
        onehot = onehot + jnp.where(sel, 1.0, 0.0)
    before = _dot(tri_ref[...], onehot.astype(BF16)) + carry_ref[0:1, :]
    ir = jnp.zeros((tr, LANES), jnp.int32)
    wt = jnp.zeros((tr, LANES), F32)
    for k in range(TOP_K):
        rank = jnp.sum(jnp.where(sels[k], before, 0.0), axis=1, keepdims=True).astype(jnp.int32)
        ir = jnp.where(lane == k, idxs[k], ir)
        ir = jnp.where(lane == TOP_K + k, rank, ir)
        wt = jnp.where(lane == k, exps[k] / tot, wt)
    ir_ref[...] = ir
    w_ref[...] = wt
    total = before[-1:, :] + onehot[-1:, :]
    carry_ref[...] = jnp.broadcast_to(total, carry_ref.shape)
    cnt_ref[...] = jnp.broadcast_to(total, cnt_ref.shape)


def route(logits, n_exp):
    t = logits.shape[0]
    tr = min(512, t)
    tri = jnp.asarray((np.arange(tr)[:, None] > np.arange(tr)[None, :]).astype(np.float32), BF16)
    row = pl.BlockSpec((tr, LANES), lambda i: (i, 0))
    return pl.pallas_call(
        functools.partial(_route_kernel, n_exp=n_exp),
        grid=(t // tr,),
        in_specs=[row, pl.BlockSpec((tr, tr), lambda i: (0, 0))],
        out_specs=[row, row, pl.BlockSpec((SUBLANES, LANES), lambda i: (0, 0))],
        out_shape=[jax.ShapeDtypeStruct((t, LANES), jnp.int32), jax.ShapeDtypeStruct((t, LANES), F32),
                   jax.ShapeDtypeStruct((SUBLANES, LANES), F32)],
        scratch_shapes=[pltpu.VMEM((SUBLANES, LANES), F32)],
        compiler_params=_cparams(("arbitrary",), 32),
        name="route",
    )(logits, tri)


def _prep_up_kernel(w_ref, even_ref, odd_ref, g_ref, l_ref):
    w = w_ref[...].astype(BF16)
    g_ref[...] = _dot(w, even_ref[...]).astype(g_ref.dtype)
    l_ref[...] = _dot(w, odd_ref[...]).astype(l_ref.dtype)


def prepare_up_weights(w_up):
    d, f2 = w_up.shape[-2:]
    w3 = w_up.reshape(-1, d, f2)
    n = w3.shape[0]
    tn = min(512, f2)
    half = tn // 2
    even = np.zeros((tn, half), np.float32)
    odd = np.zeros((tn, half), np.float32)
    even[2 * np.arange(half), np.arange(half)] = 1.0
    odd[2 * np.arange(half) + 1, np.arange(half)] = 1.0
    sel = pl.BlockSpec((tn, half), lambda e, j: (0, 0))
    out_spec = pl.BlockSpec((None, d, half), lambda e, j: (e, 0, j))
    out_shape = jax.ShapeDtypeStruct((n, d, f2 // 2), BF16)
    return pl.pallas_call(
        _prep_up_kernel,
        grid=(n, f2 // tn),
        in_specs=[pl.BlockSpec((None, d, tn), lambda e, j: (e, 0, j)), sel, sel],
        out_specs=[out_spec, out_spec],
        out_shape=[out_shape, out_shape],
        compiler_params=_cparams(("arbitrary", "arbitrary"), 32),
        name="prepare_up_weights",
    )(w3, jnp.asarray(even, BF16), jnp.asarray(odd, BF16))


def _expert_kernel(be_ref, tok_cur_ref, tok_next_ref, h_ref, wg_ref, wl_ref, bg_ref, bl_ref, wd_ref, bd_ref,
                   y_ref, xbuf, act_s, sem, *, row_block):
    i = pl.program_id(0)
    n_blocks = pl.num_programs(0)
    slot = i % 2

    def block_wait(dst_slot):
        pltpu.make_async_copy(h_ref.at[pl.ds(0, row_block), :], xbuf.at[dst_slot], sem.at[dst_slot]).wait()

    def start_gather(tok_ref, dst_slot, rows, anchor):
        for r in rows:
            pltpu.make_async_copy(h_ref.at[pl.ds(tok_ref[0, r] + anchor, 1), :],
                                  xbuf.at[dst_slot, pl.ds(r, 1), :], sem.at[dst_slot]).start()

    @pl.when(i == 0)
    def _():
        start_gather(tok_cur_ref, 0, range(row_block), 0)

    block_wait(slot)
    x = xbuf[slot].astype(BF16)
    f = wg_ref.shape[1]
    up_chunk = min(UP_CHUNK, f)
    n_chunks = f // up_chunk
    rows_per_chunk = -(-row_block // n_chunks)
    for c in range(n_chunks):
        cl = slice(c * up_chunk, (c + 1) * up_chunk)
        glu = jnp.minimum(_dot(x, wg_ref[:, cl]) + bg_ref[:, cl], SWIGLU_LIMIT)
        lin = jnp.clip(_dot(x, wl_ref[:, cl]) + bl_ref[:, cl], -SWIGLU_LIMIT, SWIGLU_LIMIT)
        act_s[:, cl] = (glu * jax.nn.sigmoid(SWIGLU_ALPHA * glu) * (lin + 1.0)).astype(BF16)
        bits = lax.bitcast_convert_type(glu[0:1, 0:1], jnp.int32)[0, 0]
        anchor = (bits & 0x7FFFFFFF) >> 31
        start_gather(tok_next_ref, 1 - slot,
                     range(c * rows_per_chunk, min((c + 1) * rows_per_chunk, row_block)), anchor)
    y_ref[...] = _dot(act_s[...], wd_ref[...]) + bd_ref[...]

    @pl.when(i == n_blocks - 1)
    def _():
        block_wait(1 - slot)


def expert_ffn(h2, row_tok, block_e, w_glu, w_lin, up_offset, b_glu, b_lin, w_down, b_down, row_block):
    d = h2.shape[1]
    n_rows = row_tok.shape[0]
    n_blocks = n_rows // row_block
    n_exp = w_down.shape[0]
    f = w_glu.shape[2]
    once = pl.Buffered(1)
    tok3 = row_tok.reshape(n_blocks, 1, row_block)
    grid_spec = pltpu.PrefetchScalarGridSpec(
        num_scalar_prefetch=1,
        grid=(n_blocks,),
        in_specs=[pl.BlockSpec((None, 1, row_block), lambda i, be: (i, 0, 0), memory_space=pltpu.SMEM),
                  pl.BlockSpec((None, 1, row_block), lambda i, be: (jnp.minimum(i + 1, n_blocks - 1), 0, 0),
                               memory_space=pltpu.SMEM),
                  pl.BlockSpec(memory_space=pl.ANY),
                  pl.BlockSpec((None, d, f), lambda i, be: (be[i] + up_offset, 0, 0), pipeline_mode=once),
                  pl.BlockSpec((None, d, f), lambda i, be: (be[i] + up_offset, 0, 0), pipeline_mode=once),
                  pl.BlockSpec((None, 1, f), lambda i, be: (be[i], 0, 0)),
                  pl.BlockSpec((None, 1, f), lambda i, be: (be[i], 0, 0)),
                  pl.BlockSpec((None, f, d), lambda i, be: (be[i], 0, 0), pipeline_mode=once),
                  pl.BlockSpec((None, 1, d), lambda i, be: (be[i], 0, 0))],
        out_specs=pl.BlockSpec((row_block, d), lambda i, be: (i, 0)),
        scratch_shapes=[pltpu.VMEM((2, row_block, d), F32), pltpu.VMEM((row_block, f), BF16),
                        pltpu.SemaphoreType.DMA((2,))])
    return pl.pallas_call(
        functools.partial(_expert_kernel, row_block=row_block),
        grid_spec=grid_spec,
        out_shape=jax.ShapeDtypeStruct((n_rows, d), F32),
        compiler_params=_cparams(("arbitrary",)),
        name="expert_ffn",
    )(block_e, tok3, tok3, h2, w_glu, w_lin, b_glu.reshape(n_exp, 1, f), b_lin.reshape(n_exp, 1, f),
      w_down, b_down.reshape(n_exp, 1, d))


def _combine_kernel(dest_cur_ref, dest_next_ref, ys_ref, w_ref, x_ref, gate_ref, g_ref, b_ref, sc_ref, sh_ref,
                    x2_ref, h_ref, buf, sem, *, tc, alpha):
    i = pl.program_id(0)
    n_tiles = pl.num_programs(0)
    slot = i % 2

    def start_gather(dest_ref, dst_slot):
        for tok in range(tc):
            for k in range(TOP_K):
                pltpu.make_async_copy(ys_ref.at[pl.ds(dest_ref[0, tok * TOP_K + k], 1), :],
                                      buf.at[dst_slot, k, pl.ds(tok, 1), :], sem.at[dst_slot]).start(priority=k % 2)

    def tile_wait(dst_slot):
        for k in range(TOP_K):
            pltpu.make_async_copy(ys_ref.at[pl.ds(0, tc), :], buf.at[dst_slot, k], sem.at[dst_slot]).wait()

    @pl.when(i == 0)
    def _():
        start_gather(dest_cur_ref, 0)

    tile_wait(slot)
    start_gather(dest_next_ref, 1 - slot)
    w = w_ref[...]
    y = w[:, 0:1] * buf[slot, 0]
    for k in range(1, TOP_K):
        y = y + w[:, k:k + 1] * buf[slot, k]
    x2 = _ln(alpha * x_ref[...] + gate_ref[...] * y) * g_ref[...] + b_ref[...]
    x2_ref[...] = x2
    h_ref[...] = (_ln(x2) * (1.0 + sc_ref[...]) + sh_ref[...]).astype(h_ref.dtype)

    @pl.when(i == n_tiles - 1)
    def _():
        tile_wait(1 - slot)


def moe_combine(ys, dest, top_w, x1, mod_l, mod_next, ln_g, ln_b, seq, alpha):
    t, d = x1.shape
    tc = min(128, seq)
    bpb = seq // tc
    row = pl.BlockSpec((tc, d), lambda i: (i, 0))
    vec = pl.BlockSpec((1, d), lambda i: (0, 0))
    n_tiles = t // tc
    dest3 = dest.reshape(n_tiles, 1, tc * TOP_K)
    return pl.pallas_call(
        functools.partial(_combine_kernel, tc=tc, alpha=alpha),
        grid=(n_tiles,),
        in_specs=[pl.BlockSpec((None, 1, tc * TOP_K), lambda i: (i, 0, 0), memory_space=pltpu.SMEM),
                  pl.BlockSpec((None, 1, tc * TOP_K), lambda i: (jnp.minimum(i + 1, n_tiles - 1), 0, 0),
                               memory_space=pltpu.SMEM),
                  pl.BlockSpec(memory_space=pl.ANY),
                  pl.BlockSpec((tc, LANES), lambda i: (i, 0)),
                  row, _mod_spec(5, d, bpb), vec, vec, _mod_spec(1, d, bpb), _mod_spec(0, d, bpb)],
        out_specs=[row, row],
        out_shape=[jax.ShapeDtypeStruct((t, d), F32), jax.ShapeDtypeStruct((t, d), BF16)],
        scratch_shapes=[pltpu.VMEM((2, TOP_K, tc, d), F32), pltpu.SemaphoreType.DMA((2,))],
        compiler_params=_cparams(("arbitrary",), 32),
        name="moe_combine",
    )(dest3, dest3, ys, top_w, x1, mod_l, ln_g.reshape(1, d), ln_b.reshape(1, d), mod_next, mod_next)


def moe_ffn(h2, logits, w_glu, w_lin, up_offset, b_glu, b_lin, w_down, b_down):
    t, d = h2.shape
    n_exp = w_down.shape[0]
    row_block = 256
    idx_rank, top_w, counts = route(logits, n_exp)
    top_idx = idx_rank[:, :TOP_K]
    rank = idx_rank[:, TOP_K:2 * TOP_K]
    counts = counts[0, :n_exp].astype(jnp.int32)
    padded = (counts + row_block - 1) // row_block * row_block
    padded_end = jnp.cumsum(padded)
    padded_start = padded_end - padded
    dest = padded_start[top_idx] + rank
    n_rows = t * TOP_K + n_exp * row_block
    tok_ids = jnp.broadcast_to(jnp.arange(t, dtype=jnp.int32)[:, None], (t, TOP_K))
    row_tok = jnp.zeros((n_rows,), jnp.int32).at[dest.reshape(-1)].set(tok_ids.reshape(-1), unique_indices=True)
    n_blocks = n_rows // row_block
    block_start = jnp.arange(n_blocks, dtype=jnp.int32) * row_block
    block_e = jnp.minimum(jnp.sum(padded_end[None, :] <= block_start[:, None], axis=1), n_exp - 1).astype(jnp.int32)
    ys = expert_ffn(h2, row_tok, block_e, w_glu, w_lin, up_offset, b_glu, b_lin, w_down, b_down, row_block)
    return ys, dest.astype(jnp.int32), top_w


def kernel(x, c, w_ada, b_ada, w_in, b_forget, w_gate, b_gate, w_proj_sb, w_proj_dil, w_proj_fox, w_out,
           ln1_g, ln1_b, w_router, b_router, w_up, b_up, w_down, b_down, ln2_g, ln2_b):
    batch, seq, d = x.shape
    depth = w_ada.shape[0]
    t = batch * seq
    alpha = (2.0 * depth) ** 0.25
    rope_tabs = rope_pair_tables(seq)
    mod = adaln_mod(c, w_ada, b_ada)
    x2d = x.reshape(t, d)
    h = ln_mod(x2d, mod[0], seq, comp_shift=0, comp_scale=1)
    n_exp = w_up.shape[1]
    w_glu, w_lin = prepare_up_weights(w_up)
    fx_q = slice(COL_FX[0] * LANES, COL_FX[0] * LANES + W_FOX)
    col_scale = jnp.ones((1, N_QKV), F32).at[0, fx_q].set(QK_SCALE * LOG2E)
    for l in range(depth):
        w_in_l = w_in[l]
        qkv = in_projection(h, w_in_l[:, :N_QKV].astype(BF16), col_scale, rope_tabs, seq)
        q_feat, k_feat = forget_features(h, w_in_l[:, N_QKV:], b_forget[l], batch, seq)
        o_sb = stick_breaking_attention(qkv, batch, seq)
        o_dl = dilated_window_attention(qkv, batch, seq)
        o_fx = forgetting_attention(qkv, q_feat, k_feat, batch, seq)
        merged = gated_merge(h, w_gate[l].astype(BF16), b_gate[l], (o_sb, o_dl, o_fx),
                             (w_proj_sb[l].astype(BF16), w_proj_dil[l].astype(BF16),
                              w_proj_fox[l].astype(BF16)))
        x1, h2, logits = out_projection(merged, w_out[l].astype(BF16), x2d, mod[l], ln1_g[l], ln1_b[l],
                                        w_router[l], b_router[l], seq, alpha)
        ys, dest, top_w = moe_ffn(h2, logits, w_glu, w_lin, l * n_exp,
                                  b_up[l][:, 0::2], b_up[l][:, 1::2], w_down[l].astype(BF16), b_down[l])
        mod_next = mod[min(l + 1, depth - 1)]
        x2d, h = moe_combine(ys, dest, top_w, x1, mod[l], mod_next, ln2_g[l], ln2_b[l], seq, alpha)
    return x2d.reshape(batch, seq, d)
```

```python
import functools

import numpy as np
import jax
import jax.numpy as jnp
from jax import lax
from jax.experimental import pallas as pl
from jax.experimental.pallas import tpu as pltpu

F32 = jnp.float32
BF16 = jnp.bfloat16

HEAD_DIM = 64
H_SB = 8
DIL_CONFIGS = ((128, 1), (512, 4), (2048, 16))
H_DIL_PER_GROUP = 4
H_FOX = 12
ROPE_THETA = 500000.0
ROPE_DIMS = HEAD_DIM // 4
TOP_K = 4
SWIGLU_ALPHA = 1.702
SWIGLU_LIMIT = 7.0
LN_EPS = 1e-5
QK_SCALE = HEAD_DIM ** -0.5

LANES = 128
SUBLANES = 8
VMEM_BUDGET_MB = 56

N_DIL_GROUPS = len(DIL_CONFIGS)
W_SB = H_SB * HEAD_DIM
W_DIL = N_DIL_GROUPS * H_DIL_PER_GROUP * HEAD_DIM
W_DIL_OUT = H_DIL_PER_GROUP * HEAD_DIM
W_FOX = H_FOX * HEAD_DIM
N_QKV = 3 * (W_SB + W_DIL + W_FOX)
COL_SB = (0, W_SB // LANES, 2 * W_SB // LANES)
_DL0 = 3 * W_SB // LANES
COL_DL = (_DL0, _DL0 + W_DIL // LANES, _DL0 + 2 * W_DIL // LANES)
_FX0 = _DL0 + 3 * W_DIL // LANES
COL_FX = (_FX0, _FX0 + W_FOX // LANES, _FX0 + 2 * W_FOX // LANES)
BAND = 128
NEG_BIG = -1e30
EXP_UNDERFLOW = -104.0


def _cparams(semantics, vmem_mb=VMEM_BUDGET_MB):
    return pltpu.CompilerParams(dimension_semantics=semantics, vmem_limit_bytes=vmem_mb * 2 ** 20)


def _dot(a, b):
    return jnp.dot(a, b, preferred_element_type=F32)


def _dot_t(a, b):
    return lax.dot_general(a, b, (((1,), (1,)), ((), ())), preferred_element_type=F32)


def _ln(x):
    mu = jnp.mean(x, axis=-1, keepdims=True)
    xc = x - mu
    var = jnp.mean(xc * xc, axis=-1, keepdims=True)
    return xc * lax.rsqrt(var + LN_EPS)


def _softplus_neg_abs(z):
    return jnp.log1p(jnp.exp(-jnp.abs(z)))


def _split3(x):
    hi = x.astype(BF16)
    r1 = x - hi.astype(F32)
    mid = r1.astype(BF16)
    lo = (r1 - mid.astype(F32)).astype(BF16)
    return hi, mid, lo


def _mod_kernel(c_ref, w_ref, b_ref, o_ref):
    c = c_ref[...]
    act = (c * jax.nn.sigmoid(c)).astype(BF16)
    o_ref[...] = _dot(act, w_ref[...].astype(BF16)) + b_ref[...]


def adaln_mod(c, w_ada, b_ada):
    n_layers, d, n = w_ada.shape
    b = c.shape[0]
    assert b <= SUBLANES
    tn = 512
    c_pad = jnp.zeros((SUBLANES, d), F32).at[:b].set(c)
    out = pl.pallas_call(
        _mod_kernel,
        grid=(n_layers, n // tn),
        in_specs=[pl.BlockSpec((SUBLANES, d), lambda l, j: (0, 0)),
                  pl.BlockSpec((None, d, tn), lambda l, j: (l, 0, j)),
                  pl.BlockSpec((None, 1, tn), lambda l, j: (l, 0, j))],
        out_specs=pl.BlockSpec((None, SUBLANES, tn), lambda l, j: (l, 0, j)),
        out_shape=jax.ShapeDtypeStruct((n_layers, SUBLANES, n), F32),
        compiler_params=_cparams(("arbitrary", "arbitrary"), 32),
        name="adaln_mod",
    )(c_pad, w_ada, b_ada.reshape(n_layers, 1, n))
    return out[:, :b].reshape(n_layers, b, 6, d).transpose(0, 2, 1, 3)[:, :, :, None, :]


def _mod_spec(comp, d, rows_per_batch_blocks):
    return pl.BlockSpec((None, None, 1, d), lambda i, *_: (comp, i // rows_per_batch_blocks, 0, 0))


def _ln_mod_kernel(x_ref, sc_ref, sh_ref, h_ref):
    h_ref[...] = (_ln(x_ref[...]) * (1.0 + sc_ref[...]) + sh_ref[...]).astype(h_ref.dtype)


def ln_mod(x2d, mod_l, seq, comp_shift, comp_scale):
    t, d = x2d.shape
    ts = min(512, seq)
    bpb = seq // ts
    return pl.pallas_call(
        _ln_mod_kernel,
        grid=(t // ts,),
        in_specs=[pl.BlockSpec((ts, d), lambda i: (i, 0)),
                  _mod_spec(comp_scale, d, bpb), _mod_spec(comp_shift, d, bpb)],
        out_specs=pl.BlockSpec((ts, d), lambda i: (i, 0)),
        out_shape=jax.ShapeDtypeStruct((t, d), BF16),
        compiler_params=_cparams(("arbitrary",), 32),
        name="ln_mod",
    )(x2d, mod_l, mod_l)


def _inproj_kernel(h_ref, w_ref, cos_ref, s1_ref, s2_ref, o_ref, *, rope_lo, rope_hi, tn):
    j = pl.program_id(1)
    acc = _dot(h_ref[...], w_ref[...])
    is_rope = jnp.logical_and(j >= rope_lo, j < rope_hi)

    @pl.when(is_rope)
    def _():
        c, s1, s2 = cos_ref[...], s1_ref[...], s2_ref[...]
        for blk in range(tn // LANES):
            a = acc[:, blk * LANES:(blk + 1) * LANES]
            r = a * c + pltpu.roll(a, LANES - ROPE_DIMS // 2, 1) * s1 + pltpu.roll(a, ROPE_DIMS // 2, 1) * s2
            o_ref[:, blk * LANES:(blk + 1) * LANES] = r.astype(o_ref.dtype)

    @pl.when(jnp.logical_not(is_rope))
    def _():
        o_ref[...] = acc.astype(o_ref.dtype)


def rope_pair_tables(seq):
    pos = jnp.arange(seq, dtype=F32)
    inv = ROPE_THETA ** (-jnp.arange(0, ROPE_DIMS, 2, dtype=F32) / ROPE_DIMS)
    ang = pos[:, None] * inv[None, :]
    cos, sin = jnp.cos(ang), jnp.sin(ang)
    half = ROPE_DIMS // 2
    head_c = jnp.concatenate([cos, cos, jnp.ones((seq, HEAD_DIM - ROPE_DIMS), F32)], axis=1)
    head_s1 = jnp.concatenate([-sin, jnp.zeros((seq, HEAD_DIM - half), F32)], axis=1)
    head_s2 = jnp.concatenate([jnp.zeros((seq, half), F32), sin,
                               jnp.zeros((seq, HEAD_DIM - ROPE_DIMS), F32)], axis=1)
    two = lambda a: jnp.concatenate([a, a], axis=1)
    return two(head_c), two(head_s1), two(head_s2)


def in_projection(h, w_qkv, rope_tabs, seq):
    t, d = h.shape
    n = w_qkv.shape[1]
    tm = min(1024, seq)
    tn = 512
    assert seq % tm == 0 and n % tn == 0
    rope_lo, rope_hi = COL_DL[0] * LANES, COL_DL[2] * LANES
    assert rope_lo % tn == 0 and rope_hi % tn == 0
    spb = seq // tm
    tab_spec = pl.BlockSpec((tm, LANES), lambda i, j: (i % spb, 0))
    return pl.pallas_call(
        functools.partial(_inproj_kernel, rope_lo=rope_lo // tn, rope_hi=rope_hi // tn, tn=tn),
        grid=(t // tm, n // tn),
        in_specs=[pl.BlockSpec((tm, d), lambda i, j: (i, 0)),
                  pl.BlockSpec((d, tn), lambda i, j: (0, j)),
                  tab_spec, tab_spec, tab_spec],
        out_specs=pl.BlockSpec((tm, tn), lambda i, j: (i, j)),
        out_shape=jax.ShapeDtypeStruct((t, n), BF16),
        compiler_params=_cparams(("arbitrary", "arbitrary"), 48),
        name="in_projection",
    )(h, w_qkv, *rope_tabs)


FEAT_PER_HEAD = 6


def _forget_feature_maps():
    width = (H_FOX // 2) * LANES
    pq = np.zeros((3 * LANES, width), np.float32)
    pk = np.zeros((3 * LANES, width), np.float32)
    cq = np.zeros((1, width), np.float32)
    ck = np.zeros((1, width), np.float32)
    for head in range(H_FOX):
        base = (head // 2) * LANES + (head % 2) * FEAT_PER_HEAD
        for piece in range(3):
            pq[piece * LANES + head, base + piece] = 1.0
            pk[piece * LANES + head, base + 3 + piece] = -1.0
        cq[0, base + 3:base + 6] = 1.0
        ck[0, base:base + 3] = 1.0
    return pq, pk, cq, ck


def _forget_kernel(h_ref, w_ref, b_ref, tri_ref, pq_ref, pk_ref, cq_ref, ck_ref, qf_ref, kf_ref, carry_ref):
    @pl.when(pl.program_id(1) == 0)
    def _():
        carry_ref[...] = jnp.zeros_like(carry_ref)

    f = _dot(h_ref[...], w_ref[...]) + b_ref[...]
    log_f = jnp.minimum(f, 0.0) - _softplus_neg_abs(f)
    tri = tri_ref[...]
    hi, mid, lo = _split3(log_f)
    cum = _dot(tri, hi) + _dot(tri, mid) + _dot(tri, lo) + carry_ref[0:1, :]
    carry_ref[...] = jnp.broadcast_to(cum[-1:, :], carry_ref.shape)
    pieces = jnp.concatenate(_split3(cum), axis=1)
    qf_ref[...] = (_dot(pieces, pq_ref[...]) + cq_ref[...]).astype(qf_ref.dtype)
    kf_ref[...] = (_dot(pieces, pk_ref[...]) + ck_ref[...]).astype(kf_ref.dtype)


def forget_features(h, w_f, b_f, batch, seq):
    t, d = h.shape
    ts = min(256, seq)
    nsb = seq // ts
    w_pad = jnp.zeros((d, LANES), BF16).at[:, :H_FOX].set(w_f.astype(BF16))
    b_pad = jnp.zeros((1, LANES), F32).at[0, :H_FOX].set(b_f)
    tri = (np.arange(ts)[:, None] >= np.arange(ts)[None, :]).astype(np.float32)
    pq, pk, cq, ck = _forget_feature_maps()
    width = pq.shape[1]
    const = lambda shape: pl.BlockSpec(shape, lambda b, s: (0, 0))
    out_spec = pl.BlockSpec((ts, width), lambda b, s: (b * nsb + s, 0))
    out_shape = jax.ShapeDtypeStruct((t, width), BF16)
    return pl.pallas_call(
        _forget_kernel,
        grid=(batch, nsb),
        in_specs=[pl.BlockSpec((ts, d), lambda b, s: (b * nsb + s, 0)),
                  const((d, LANES)), const((1, LANES)), const((ts, ts)),
                  const(pq.shape), const(pk.shape), const(cq.shape), const(ck.shape)],
        out_specs=[out_spec, out_spec],
        out_shape=[out_shape, out_shape],
        scratch_shapes=[pltpu.VMEM((SUBLANES, LANES), F32)],
        compiler_params=_cparams(("arbitrary", "arbitrary"), 32),
        name="forget_features",
    )(h, w_pad, b_pad, jnp.asarray(tri, BF16), jnp.asarray(pq, BF16), jnp.asarray(pk, BF16),
      jnp.asarray(cq), jnp.asarray(ck))


def _pair_masks(tq):
    lane = lax.broadcasted_iota(jnp.int32, (tq, LANES), 1)
    return lane < HEAD_DIM


KEY_SUB = 128
QRY_SUB = 128
UP_CHUNK = 256
SUFFIX_SUB = 256


def _diag_tile_kind(ks, qs, key_sub, qry_sub, strict, key_off):
    if key_off is None:
        return 'full'
    k_lo, k_hi = key_off + ks * key_sub, key_off + ks * key_sub + key_sub - 1
    q_lo, q_hi = qs * qry_sub, qs * qry_sub + qry_sub - 1
    if strict:
        if k_lo >= q_hi:
            return 'dead'
        return 'full' if k_hi < q_lo else 'partial'
    if k_lo > q_hi:
        return 'dead'
    return 'full' if k_hi <= q_lo else 'partial'


def _tile_allowed(ks, qs, key_sub, qry_sub, strict, key_off):
    kpos = key_off + ks * key_sub + lax.broadcasted_iota(jnp.int32, (key_sub, qry_sub), 0)
    qpos = qs * qry_sub + lax.broadcasted_iota(jnp.int32, (key_sub, qry_sub), 1)
    return kpos < qpos if strict else kpos <= qpos


def _pair_transpose_in(qkv3, col, width):
    return lax.slice_in_dim(qkv3, col * LANES, col * LANES + width, axis=2).transpose(0, 2, 1)


def _sb_kernel(q_ref, k_ref, vt_ref, later_ref, o_ref,
               qx_s, acc_s, run_s, first_s, z_s, hi_s, lo_s, btw_s, w_s, *, tq, tk, key_sub, qry_sub):
    q_blk = pl.program_id(2)
    head_a = _pair_masks(tq)
    q = q_ref[...] * QK_SCALE
    qx_s[0] = jnp.where(head_a, q, 0).astype(BF16)
    qx_s[1] = jnp.where(head_a, 0, q).astype(BF16)
    acc_s[...] = jnp.zeros_like(acc_s)
    run_s[...] = jnp.zeros_like(run_s)

    def step(key_off, kv_blk):
        k_ref_blk = k_ref.at[pl.ds(pl.multiple_of(kv_blk * tk, tk), tk), :]
        vt_blk = vt_ref.at[kv_blk]
        _sb_block(key_off, k_ref_blk, vt_blk)

    def _sb_block(key_off, k_ref, vt_ref):
        suf = later_ref.shape[0]
        n_suf = tk // suf
        per_suf = suf // key_sub
        tiles = [(ks, qs) for ks in range(tk // key_sub) for qs in range(tq // qry_sub)]
        kind_of = lambda ks, qs: _diag_tile_kind(ks, qs, key_sub, qry_sub, True, key_off)
        k = k_ref[...]
        for head in range(2):
            z_s[head] = _dot_t(k, qx_s[head])
        for head in range(2):
            for ks, qs in tiles:
                kl = slice(ks * key_sub, (ks + 1) * key_sub)
                ql = slice(qs * qry_sub, (qs + 1) * qry_sub)
                first_row = head * n_suf + ks // per_suf
                kind = kind_of(ks, qs)
                if kind == 'dead':
                    hi_s[head, kl, ql] = jnp.zeros((key_sub, qry_sub), BF16)
                    lo_s[head, kl, ql] = jnp.zeros((key_sub, qry_sub), BF16)
                    if ks % per_suf == 0:
                        first_s[first_row:first_row + 1, ql] = jnp.zeros((1, qry_sub), F32)
                    continue
                z = z_s[head, kl, ql]
                log_stop = jnp.minimum(z, 0.0) - jnp.log(1.0 + jnp.exp(-jnp.abs(z)))
                log_cont = log_stop - z
                if kind == 'partial':
                    log_cont = jnp.where(_tile_allowed(ks, qs, key_sub, qry_sub, True, key_off), log_cont, 0.0)
                z_s[head, kl, ql] = log_stop
                hi = log_cont.astype(BF16)
                hi_s[head, kl, ql] = hi
                lo_s[head, kl, ql] = (log_cont - hi.astype(F32)).astype(BF16)
                if ks % per_suf == 0:
                    first_s[first_row:first_row + 1, ql] = log_cont[0:1, :]
        later = later_ref[...]
        for head in range(2):
            for blk in range(n_suf):
                bl = slice(blk * suf, (blk + 1) * suf)
                btw_s[head, bl, :] = _dot(later, hi_s[head, bl, :]) + _dot(later, lo_s[head, bl, :])

        def block_sum(head, blk, ql):
            row = head * n_suf + blk
            return btw_s[head, blk * suf:blk * suf + 1, ql] + first_s[row:row + 1, ql]

        for head in range(2):
            for ks, qs in tiles:
                kl = slice(ks * key_sub, (ks + 1) * key_sub)
                ql = slice(qs * qry_sub, (qs + 1) * qry_sub)
                kind = kind_of(ks, qs)
                if kind == 'dead':
                    w_s[head, kl, ql] = jnp.zeros((key_sub, qry_sub), BF16)
                    continue
                after = run_s[head:head + 1, ql]
                for blk in range(ks // per_suf + 1, n_suf):
                    after = after + block_sum(head, blk, ql)
                w = jnp.exp(z_s[head, kl, ql] + btw_s[head, kl, ql] + after)
                if kind == 'partial':
                    w = jnp.where(_tile_allowed(ks, qs, key_sub, qry_sub, True, key_off), w, 0.0)
                w_s[head, kl, ql] = w.astype(BF16)
        for head in range(2):
            rows = slice(head * HEAD_DIM, (head + 1) * HEAD_DIM)
            acc_s[rows, :] += _dot(vt_ref[rows, :], w_s[head])
            total = run_s[head:head + 1, :]
            for blk in range(n_suf):
                total = total + block_sum(head, blk, slice(None))
            run_s[head:head + 1, :] = total

    def all_weights_zero():
        return jnp.max(run_s[0:2, :]) < EXP_UNDERFLOW

    step(0, q_blk)

    def more(carry):
        kv_blk, dead = carry
        return jnp.logical_and(kv_blk >= 0, jnp.logical_not(dead))

    def visit(carry):
        kv_blk, _ = carry
        step(None, kv_blk)
        return kv_blk - 1, all_weights_zero()

    lax.while_loop(more, visit, (q_blk - 1, all_weights_zero()))
    o_ref[...] = acc_s[...].astype(o_ref.dtype)


def stick_breaking_attention(qkv, batch, seq):
    tq = tk = min(512, seq)
    key_sub, qry_sub = min(KEY_SUB, tk), min(QRY_SUB, tq)
    n_pairs = W_SB // LANES
    n_kv = seq // tk
    qkv3 = qkv.reshape(batch, seq, qkv.shape[-1])
    qc, kc, vc = COL_SB
    v_t = _pair_transpose_in(qkv3, vc, W_SB).reshape(batch, n_pairs, LANES, n_kv, tk).transpose(0, 1, 3, 2, 4)
    suf = min(SUFFIX_SUB, tk)
    assert 2 * (tk // suf) <= SUBLANES and suf % key_sub == 0
    later = jnp.asarray((np.arange(suf)[None, :] > np.arange(suf)[:, None]).astype(np.float32), BF16)
    block_f32 = pltpu.VMEM((2, tk, tq), F32)
    block_bf16 = pltpu.VMEM((2, tk, tq), BF16)
    row_stats = pltpu.VMEM((SUBLANES, tq), F32)
    out_t = pl.pallas_call(
        functools.partial(_sb_kernel, tq=tq, tk=tk, key_sub=key_sub, qry_sub=qry_sub),
        grid=(batch, n_pairs, seq // tq),
        in_specs=[pl.BlockSpec((None, tq, LANES), lambda b, p, i: (b, i, qc + p)),
                  pl.BlockSpec((None, seq, LANES), lambda b, p, i: (b, 0, kc + p)),
                  pl.BlockSpec((None, None, n_kv, LANES, tk), lambda b, p, i: (b, p, 0, 0, 0)),
                  pl.BlockSpec((suf, suf), lambda b, p, i: (0, 0))],
        out_specs=pl.BlockSpec((None, LANES, tq), lambda b, p, i: (b, p, i)),
        scratch_shapes=[pltpu.VMEM((2, tq, LANES), BF16), pltpu.VMEM((LANES, tq), F32),
                        row_stats, row_stats, block_f32, block_bf16, block_bf16, block_f32, block_bf16],
        out_shape=jax.ShapeDtypeStruct((batch, W_SB, seq), BF16),
        compiler_params=_cparams(("arbitrary", "arbitrary", "arbitrary"), 32),
        name="stick_breaking_attention",
    )(qkv3, qkv3, v_t, later)
    return out_t.transpose(0, 2, 1).reshape(batch * seq, W_SB)


def _fox_kernel(q_ref, qf_ref, k_ref, kf_ref, vt_ref, o_ref,
                qx_s, acc_s, m_s, l_s, alpha_s, z_s, p_s, *, tq, tk, key_sub, qry_sub):
    q_blk = pl.program_id(2)
    lane = lax.broadcasted_iota(jnp.int32, (tq, LANES), 1)
    q = q_ref[...] * QK_SCALE
    qf = qf_ref[...]
    for head in range(2):
        in_head = (lane >= head * HEAD_DIM) & (lane < (head + 1) * HEAD_DIM)
        in_feat = (lane >= head * FEAT_PER_HEAD) & (lane < (head + 1) * FEAT_PER_HEAD)
        qx_s[head] = jnp.concatenate([jnp.where(in_head, q, 0).astype(BF16),
                                      jnp.where(in_feat, qf, 0).astype(BF16)], axis=1)
    acc_s[...] = jnp.zeros_like(acc_s)
    l_s[...] = jnp.zeros_like(l_s)
    m_s[...] = jnp.full_like(m_s, NEG_BIG)

    def step(key_off, kv_blk):
        rows = pl.ds(pl.multiple_of(kv_blk * tk, tk), tk)
        _fox_block(key_off, k_ref.at[rows, :], kf_ref.at[rows, :], vt_ref.at[kv_blk])

    def _fox_block(key_off, k_ref, kf_ref, vt_ref):
        kind_of = lambda ks, qs: _diag_tile_kind(ks, qs, key_sub, qry_sub, False, key_off)
        kx = jnp.concatenate([k_ref[...], kf_ref[...]], axis=1)
        for head in range(2):
            z_s[head] = _dot_t(kx, qx_s[head])
        for head in range(2):
            for qs in range(tq // qry_sub):
                ql = slice(qs * qry_sub, (qs + 1) * qry_sub)

                def score(ks):
                    z = z_s[head, ks * key_sub:(ks + 1) * key_sub, ql]
                    if kind_of(ks, qs) == 'partial':
                        z = jnp.where(_tile_allowed(ks, qs, key_sub, qry_sub, False, key_off), z, NEG_BIG)
                    return z

                live = [ks for ks in range(tk // key_sub) if kind_of(ks, qs) != 'dead']
                m_old = m_s[head:head + 1, ql]
                m_new = m_old
                for ks in live:
                    m_new = jnp.maximum(m_new, jnp.max(score(ks), axis=0, keepdims=True))
                alpha = jnp.exp(m_old - m_new)
                total = alpha * l_s[head:head + 1, ql]
                for ks in range(tk // key_sub):
                    kl = slice(ks * key_sub, (ks + 1) * key_sub)
                    if ks not in live:
                        p_s[head, kl, ql] = jnp.zeros((key_sub, qry_sub), BF16)
                        continue
                    prob = jnp.exp(score(ks) - m_new)
                    total = total + jnp.sum(prob, axis=0, keepdims=True)
                    p_s[head, kl, ql] = prob.astype(BF16)
                m_s[head:head + 1, ql] = m_new
                l_s[head:head + 1, ql] = total
                alpha_s[head:head + 1, ql] = alpha
        for head in range(2):
            rows = slice(head * HEAD_DIM, (head + 1) * HEAD_DIM)
            acc_s[rows, :] = alpha_s[head:head + 1, :] * acc_s[rows, :] + _dot(vt_ref[rows, :], p_s[head])

    per_q = tq // tk
    first_diag = q_blk * per_q

    def visit(kv_blk, carry):
        step(None, kv_blk)
        return carry

    lax.fori_loop(0, first_diag, visit, 0)
    for variant in range(per_q):
        step(variant * tk, first_diag + variant)
    for head in range(2):
        rows = slice(head * HEAD_DIM, (head + 1) * HEAD_DIM)
        o_ref[rows, :] = (acc_s[rows, :] / l_s[head:head + 1, :]).astype(o_ref.dtype)


def forgetting_attention(qkv, q_feat, k_feat, batch, seq):
    tk = min(512, seq)
    tq = min(1024, seq)
    key_sub, qry_sub = min(KEY_SUB, tk), min(QRY_SUB, tq)
    assert tq % tk == 0
    n_pairs = W_FOX // LANES
    n_kv = seq // tk
    qkv3 = qkv.reshape(batch, seq, qkv.shape[-1])
    qf3 = q_feat.reshape(batch, seq, n_pairs * LANES)
    kf3 = k_feat.reshape(batch, seq, n_pairs * LANES)
    qc, kc, vc = COL_FX
    v_t = _pair_transpose_in(qkv3, vc, W_FOX).reshape(batch, n_pairs, LANES, n_kv, tk).transpose(0, 1, 3, 2, 4)
    out_t = pl.pallas_call(
        functools.partial(_fox_kernel, tq=tq, tk=tk, key_sub=key_sub, qry_sub=qry_sub),
        grid=(batch, n_pairs, seq // tq),
        in_specs=[pl.BlockSpec((None, tq, LANES), lambda b, p, i: (b, i, qc + p)),
                  pl.BlockSpec((None, tq, LANES), lambda b, p, i: (b, i, p)),
                  pl.BlockSpec((None, seq, LANES), lambda b, p, i: (b, 0, kc + p)),
                  pl.BlockSpec((None, seq, LANES), lambda b, p, i: (b, 0, p)),
                  pl.BlockSpec((None, None, n_kv, LANES, tk), lambda b, p, i: (b, p, 0, 0, 0))],
        out_specs=pl.BlockSpec((None, LANES, tq), lambda b, p, i: (b, p, i)),
        scratch_shapes=[pltpu.VMEM((2, tq, 2 * LANES), BF16), pltpu.VMEM((LANES, tq), F32),
                        pltpu.VMEM((SUBLANES, tq), F32), pltpu.VMEM((SUBLANES, tq), F32),
                        pltpu.VMEM((SUBLANES, tq), F32), pltpu.VMEM((2, tk, tq), F32),
                        pltpu.VMEM((2, tk, tq), BF16)],
        out_shape=jax.ShapeDtypeStruct((batch, W_FOX, seq), BF16),
        compiler_params=_cparams(("arbitrary", "arbitrary", "arbitrary"), 40),
        name="forgetting_attention",
    )(qkv3, qf3, qkv3, kf3, v_t)
    return out_t.transpose(0, 2, 1).reshape(batch * seq, W_FOX)


def _band_kernel(q_ref, kc_ref, vc_ref, kp_ref, vp_ref, o_ref, lse_ref, *, tq):
    i = pl.program_id(2)
    head_a = _pair_masks(tq)
    q = q_ref[...] * QK_SCALE
    kc, vc, kp, vp = kc_ref[...], vc_ref[...], kp_ref[...], vp_ref[...]
    diff = lax.broadcasted_iota(jnp.int32, (tq, tq), 0) - lax.broadcasted_iota(jnp.int32, (tq, tq), 1)
    mask_c = jnp.logical_and(diff >= 0, diff <= BAND)
    rp = lax.broadcasted_iota(jnp.int32, (tq, BAND), 0)
    cp = lax.broadcasted_iota(jnp.int32, (tq, BAND), 1)
    mask_p = jnp.logical_and(cp >= rp, i > 0)
    outs, lses = [], []
    for sel in (head_a, jnp.logical_not(head_a)):
        qh = jnp.where(sel, q, 0).astype(BF16)
        sc = jnp.where(mask_c, _dot_t(qh, kc), NEG_BIG)
        sp = jnp.where(mask_p, _dot_t(qh, kp), NEG_BIG)
        m = jnp.maximum(jnp.max(sc, axis=1, keepdims=True), jnp.max(sp, axis=1, keepdims=True))
        ec = jnp.exp(sc - m)
        ep = jnp.exp(sp - m)
        den = jnp.sum(ec, axis=1, keepdims=True) + jnp.sum(ep, axis=1, keepdims=True)
        outs.append((_dot(ec.astype(BF16), vc) + _dot(ep.astype(BF16), vp)) / den)
        lses.append(m + jnp.log(den))
    o_ref[...] = jnp.where(head_a, outs[0], outs[1])
    lse_ref[...] = jnp.where(head_a, lses[0], lses[1])


def _band_attention(q, k, v):
    nb, u, w = q.shape
    tq = min(256, u)
    assert u % tq == 0 and tq % BAND == 0
    sub = tq // BAND
    cur = pl.BlockSpec((None, tq, LANES), lambda n, p, i: (n, i, p))
    prev = pl.BlockSpec((None, BAND, LANES), lambda n, p, i: (n, jnp.maximum(i * sub - 1, 0), p))
    shp = jax.ShapeDtypeStruct((nb, u, w), F32)
    return pl.pallas_call(
        functools.partial(_band_kernel, tq=tq),
        grid=(nb, w // LANES, u // tq),
        in_specs=[cur, cur, cur, prev, prev],
        out_specs=[cur, cur],
        out_shape=[shp, shp],
        compiler_params=_cparams(("arbitrary", "arbitrary", "arbitrary"), 32),
        name="band_attention",
    )(q, k, v, k, v)


def _dil_mix_kernel(o0, o1, o2, l0, l1, l2, out_ref):
    a, b, c = l0[...], l1[...], l2[...]
    m = jnp.maximum(jnp.maximum(a, b), c)
    ea, eb, ec = jnp.exp(a - m), jnp.exp(b - m), jnp.exp(c - m)
    tot = ea + eb + ec
    out_ref[...] = ((o0[...] * ea + o1[...] * eb + o2[...] * ec) / tot).astype(out_ref.dtype)


def dilated_window_attention(qkv, batch, seq):
    t = batch * seq
    qkv3 = qkv.reshape(batch, seq, qkv.shape[-1])
    gw = W_DIL_OUT
    outs, lses = [], []
    for g, (window, dil) in enumerate(DIL_CONFIGS):
        assert window // dil == BAND and seq % (dil * BAND) == 0
        u = seq // dil

        def split(col):
            a = lax.slice_in_dim(qkv3, col * LANES + g * gw, col * LANES + (g + 1) * gw, axis=2)
            return a.reshape(batch, u, dil, gw).transpose(0, 2, 1, 3).reshape(batch * dil, u, gw)

        o, lse = _band_attention(split(COL_DL[0]), split(COL_DL[1]), split(COL_DL[2]))
        merge = lambda a: a.reshape(batch, dil, u, gw).transpose(0, 2, 1, 3).reshape(t, gw)
        outs.append(merge(o))
        lses.append(merge(lse))
    tm = min(1024, t)
    spec = pl.BlockSpec((tm, gw), lambda i: (i, 0))
    return pl.pallas_call(
        _dil_mix_kernel,
        grid=(t // tm,),
        in_specs=[spec] * 6,
        out_specs=spec,
        out_shape=jax.ShapeDtypeStruct((t, gw), BF16),
        compiler_params=_cparams(("arbitrary",), 32),
        name="dilated_mix",
    )(*outs, *lses)


def _merge_kernel(h_ref, wg0, wg1, wg2, bg0, bg1, bg2, o0, o1, o2, wp0, wp1, wp2, out_ref):
    h = h_ref[...]

    def branch(wg, bg, o, wp):
        return jax.nn.sigmoid(_dot(h, wg[...]) + bg[...]) * _dot(o[...], wp[...])

    merged = branch(wg0, bg0, o0, wp0) + branch(wg1, bg1, o1, wp1) + branch(wg2, bg2, o2, wp2)
    out_ref[...] = merged.astype(out_ref.dtype)


def gated_merge(h, w_gate, b_gate, branch_outs, branch_projs):
    t, d = h.shape
    tm, tn = min(512, t), min(512, d)
    nj = d // tn
    b_gate2 = b_gate.reshape(1, -1)
    gate_w = [pl.BlockSpec((d, tn), lambda j, i, br=br: (0, br * nj + j)) for br in range(3)]
    gate_b = [pl.BlockSpec((1, tn), lambda j, i, br=br: (0, br * nj + j)) for br in range(3)]
    o_specs = [pl.BlockSpec((tm, o.shape[1]), lambda j, i: (i, 0)) for o in branch_outs]
    p_specs = [pl.BlockSpec((w.shape[0], tn), lambda j, i: (0, j)) for w in branch_projs]
    return pl.pallas_call(
        _merge_kernel,
        grid=(nj, t // tm),
        in_specs=[pl.BlockSpec((tm, d), lambda j, i: (i, 0))] + gate_w + gate_b + o_specs + p_specs,
        out_specs=pl.BlockSpec((tm, tn), lambda j, i: (i, j)),
        out_shape=jax.ShapeDtypeStruct((t, d), BF16),
        compiler_params=_cparams(("arbitrary", "arbitrary"), 48),
        name="gated_merge",
    )(h, w_gate, w_gate, w_gate, b_gate2, b_gate2, b_gate2, *branch_outs, *branch_projs)


def _outproj_kernel(m_ref, w_ref, x_ref, gate_ref, g_ref, b_ref, sc_ref, sh_ref, wrh_ref, wrl_ref, br_ref,
                    x1_ref, h2_ref, lg_ref, *, alpha):
    y = _dot(m_ref[...], w_ref[...])
    x1 = _ln(alpha * x_ref[...] + gate_ref[...] * y) * g_ref[...] + b_ref[...]
    x1_ref[...] = x1
    h2 = _ln(x1) * (1.0 + sc_ref[...]) + sh_ref[...]
    h2_ref[...] = h2
    hi = h2.astype(BF16)
    lo = (h2 - hi.astype(F32)).astype(BF16)
    wrh = wrh_ref[...]
    lg_ref[...] = _dot(hi, wrh) + _dot(lo, wrh) + _dot(hi, wrl_ref[...]) + br_ref[...]


def out_projection(merged, w_out, x2d, mod_l, ln_g, ln_b, w_router, b_router, seq, alpha):
    t, d = x2d.shape
    tm = min(256, seq)
    bpb = seq // tm
    n_exp = w_router.shape[1]
    wr = jnp.zeros((d, LANES), F32).at[:, :n_exp].set(w_router)
    wr_hi = wr.astype(BF16)
    wr_lo = (wr - wr_hi.astype(F32)).astype(BF16)
    br = jnp.zeros((1, LANES), F32).at[0, :n_exp].set(b_router)
    row = pl.BlockSpec((tm, d), lambda i: (i, 0))
    vec = pl.BlockSpec((1, d), lambda i: (0, 0))
    wr_spec = pl.BlockSpec((d, LANES), lambda i: (0, 0))
    return pl.pallas_call(
        functools.partial(_outproj_kernel, alpha=alpha),
        grid=(t // tm,),
        in_specs=[row, pl.BlockSpec((d, d), lambda i: (0, 0)), row, _mod_spec(2, d, bpb), vec, vec,
                  _mod_spec(4, d, bpb), _mod_spec(3, d, bpb), wr_spec, wr_spec,
                  pl.BlockSpec((1, LANES), lambda i: (0, 0))],
        out_specs=[row, row, pl.BlockSpec((tm, LANES), lambda i: (i, 0))],
        out_shape=[jax.ShapeDtypeStruct((t, d), F32), jax.ShapeDtypeStruct((t, d), F32),
                   jax.ShapeDtypeStruct((t, LANES), F32)],
        compiler_params=_cparams(("arbitrary",), 48),
        name="out_projection",
    )(merged, w_out, x2d, mod_l, ln_g.reshape(1, d), ln_b.reshape(1, d), mod_l, mod_l, wr_hi, wr_lo, br)


def _route_kernel(lg_ref, tri_ref, ir_ref, w_ref, cnt_ref, carry_ref, *, n_exp):
    @pl.when(pl.program_id(0) == 0)
    def _():
        carry_ref[...] = jnp.zeros_like(carry_ref)

    tr = lg_ref.shape[0]
    lane = lax.broadcasted_iota(jnp.int32, (tr, LANES), 1)
    logits = jnp.where(lane < n_exp, lg_ref[...], -jnp.inf)
    vals, sels, idxs = [], [], []
    for _ in range(TOP_K):
        m = jnp.max(logits, axis=1, keepdims=True)
        idx = jnp.min(jnp.where(logits == m, lane, LANES), axis=1, keepdims=True)
        sel = lane == idx
        vals.append(m)
        idxs.append(idx)
        sels.append(sel)
        logits = jnp.where(sel, -jnp.inf, logits)
    exps = [jnp.exp(v - vals[0]) for v in vals]
    tot = exps[0] + exps[1] + exps[2] + exps[3]
    onehot = jnp.zeros((tr, LANES), F32)
    for sel in sels:
        onehot = onehot + jnp.where(sel, 1.0, 0.0)
    before = _dot(tri_ref[...], onehot.astype(BF16)) + carry_ref[0:1, :]
    ir = jnp.zeros((tr, LANES), jnp.int32)
    wt = jnp.zeros((tr, LANES), F32)
    for k in range(TOP_K):
        rank = jnp.sum(jnp.where(sels[k], before, 0.0), axis=1, keepdims=True).astype(jnp.int32)
        ir = jnp.where(lane == k, idxs[k], ir)
        ir = jnp.where(lane == TOP_K + k, rank, ir)
        wt = jnp.where(lane == k, exps[k] / tot, wt)
    ir_ref[...] = ir
    w_ref[...] = wt
    total = before[-1:, :] + onehot[-1:, :]
    carry_ref[...] = jnp.broadcast_to(total, carry_ref.shape)
    cnt_ref[...] = jnp.broadcast_to(total, cnt_ref.shape)


def route(logits, n_exp):
    t = logits.shape[0]
    tr = min(512, t)
    tri = jnp.asarray((np.arange(tr)[:, None] > np.arange(tr)[None, :]).astype(np.float32), BF16)
    row = pl.BlockSpec((tr, LANES), lambda i: (i, 0))
    return pl.pallas_call(
        functools.partial(_route_kernel, n_exp=n_exp),
        grid=(t // tr,),
        in_specs=[row, pl.BlockSpec((tr, tr), lambda i: (0, 0))],
        out_specs=[row, row, pl.BlockSpec((SUBLANES, LANES), lambda i: (0, 0))],
        out_shape=[jax.ShapeDtypeStruct((t, LANES), jnp.int32), jax.ShapeDtypeStruct((t, LANES), F32),
                   jax.ShapeDtypeStruct((SUBLANES, LANES), F32)],
        scratch_shapes=[pltpu.VMEM((SUBLANES, LANES), F32)],
        compiler_params=_cparams(("arbitrary",), 32),
        name="route",
    )(logits, tri)


def _prep_up_kernel(w_ref, even_ref, odd_ref, g_ref, l_ref):
    w = w_ref[...].astype(BF16)
    g_ref[...] = _dot(w, even_ref[...]).astype(g_ref.dtype)
    l_ref[...] = _dot(w, odd_ref[...]).astype(l_ref.dtype)


def prepare_up_weights(w_up):
    d, f2 = w_up.shape[-2:]
    w3 = w_up.reshape(-1, d, f2)
    n = w3.shape[0]
    tn = min(512, f2)
    half = tn // 2
    even = np.zeros((tn, half), np.float32)
    odd = np.zeros((tn, half), np.float32)
    even[2 * np.arange(half), np.arange(half)] = 1.0
    odd[2 * np.arange(half) + 1, np.arange(half)] = 1.0
    sel = pl.BlockSpec((tn, half), lambda e, j: (0, 0))
    out_spec = pl.BlockSpec((None, d, half), lambda e, j: (e, 0, j))
    out_shape = jax.ShapeDtypeStruct((n, d, f2 // 2), BF16)
    return pl.pallas_call(
        _prep_up_kernel,
        grid=(n, f2 // tn),
        in_specs=[pl.BlockSpec((None, d, tn), lambda e, j: (e, 0, j)), sel, sel],
        out_specs=[out_spec, out_spec],
        out_shape=[out_shape, out_shape],
        compiler_params=_cparams(("arbitrary", "arbitrary"), 32),
        name="prepare_up_weights",
    )(w3, jnp.asarray(even, BF16), jnp.asarray(odd, BF16))


def _expert_kernel(be_ref, tok_cur_ref, tok_next_ref, h_ref, wg_ref, wl_ref, bg_ref, bl_ref, wd_ref, bd_ref,
                   y_ref, xbuf, act_s, sem, *, row_block):
    i = pl.program_id(0)
    n_blocks = pl.num_programs(0)
    slot = i % 2

    def block_wait(dst_slot):
        pltpu.make_async_copy(h_ref.at[pl.ds(0, row_block), :], xbuf.at[dst_slot], sem.at[dst_slot]).wait()

    def start_gather(tok_ref, dst_slot, rows, anchor):
        for r in rows:
            pltpu.make_async_copy(h_ref.at[pl.ds(tok_ref[0, r] + anchor, 1), :],
                                  xbuf.at[dst_slot, pl.ds(r, 1), :], sem.at[dst_slot]).start()

    @pl.when(i == 0)
    def _():
        start_gather(tok_cur_ref, 0, range(row_block), 0)

    block_wait(slot)
    x = xbuf[slot].astype(BF16)
    f = wg_ref.shape[1]
    up_chunk = min(UP_CHUNK, f)
    n_chunks = f // up_chunk
    rows_per_chunk = -(-row_block // n_chunks)
    for c in range(n_chunks):
        cl = slice(c * up_chunk, (c + 1) * up_chunk)
        glu = jnp.minimum(_dot(x, wg_ref[:, cl]) + bg_ref[:, cl], SWIGLU_LIMIT)
        lin = jnp.clip(_dot(x, wl_ref[:, cl]) + bl_ref[:, cl], -SWIGLU_LIMIT, SWIGLU_LIMIT)
        act_s[:, cl] = (glu * jax.nn.sigmoid(SWIGLU_ALPHA * glu) * (lin + 1.0)).astype(BF16)
        bits = lax.bitcast_convert_type(glu[0:1, 0:1], jnp.int32)[0, 0]
        anchor = (bits & 0x7FFFFFFF) >> 31
        start_gather(tok_next_ref, 1 - slot,
                     range(c * rows_per_chunk, min((c + 1) * rows_per_chunk, row_block)), anchor)
    y_ref[...] = _dot(act_s[...], wd_ref[...]) + bd_ref[...]

    @pl.when(i == n_blocks - 1)
    def _():
        block_wait(1 - slot)


def expert_ffn(h2, row_tok, block_e, w_glu, w_lin, up_offset, b_glu, b_lin, w_down, b_down, row_block):
    d = h2.shape[1]
    n_rows = row_tok.shape[0]
    n_blocks = n_rows // row_block
    n_exp = w_down.shape[0]
    f = w_glu.shape[2]
    weight_buffers = pl.Buffered(2)
    tok3 = row_tok.reshape(n_blocks, 1, row_block)
    grid_spec = pltpu.PrefetchScalarGridSpec(
        num_scalar_prefetch=1,
        grid=(n_blocks,),
        in_specs=[pl.BlockSpec((None, 1, row_block), lambda i, be: (i, 0, 0), memory_space=pltpu.SMEM),
                  pl.BlockSpec((None, 1, row_block), lambda i, be: (jnp.minimum(i + 1, n_blocks - 1), 0, 0),
                               memory_space=pltpu.SMEM),
                  pl.BlockSpec(memory_space=pl.ANY),
                  pl.BlockSpec((None, d, f), lambda i, be: (be[i] + up_offset, 0, 0), pipeline_mode=weight_buffers),
                  pl.BlockSpec((None, d, f), lambda i, be: (be[i] + up_offset, 0, 0), pipeline_mode=weight_buffers),
                  pl.BlockSpec((None, 1, f), lambda i, be: (be[i], 0, 0)),
                  pl.BlockSpec((None, 1, f), lambda i, be: (be[i], 0, 0)),
                  pl.BlockSpec((None, f, d), lambda i, be: (be[i], 0, 0), pipeline_mode=weight_buffers),
                  pl.BlockSpec((None, 1, d), lambda i, be: (be[i], 0, 0))],
        out_specs=pl.BlockSpec((row_block, d), lambda i, be: (i, 0)),
        scratch_shapes=[pltpu.VMEM((2, row_block, d), F32), pltpu.VMEM((row_block, f), BF16),
                        pltpu.SemaphoreType.DMA((2,))])
    return pl.pallas_call(
        functools.partial(_expert_kernel, row_block=row_block),
        grid_spec=grid_spec,
        out_shape=jax.ShapeDtypeStruct((n_rows, d), F32),
        compiler_params=_cparams(("arbitrary",)),
        name="expert_ffn",
    )(block_e, tok3, tok3, h2, w_glu, w_lin, b_glu.reshape(n_exp, 1, f), b_lin.reshape(n_exp, 1, f),
      w_down, b_down.reshape(n_exp, 1, d))


def _combine_kernel(dest_cur_ref, dest_next_ref, ys_ref, w_ref, x_ref, gate_ref, g_ref, b_ref, sc_ref, sh_ref,
                    x2_ref, h_ref, buf, sem, *, tc, alpha):
    i = pl.program_id(0)
    n_tiles = pl.num_programs(0)
    slot = i % 2

    def start_gather(dest_ref, dst_slot):
        for tok in range(tc):
            for k in range(TOP_K):
                pltpu.make_async_copy(ys_ref.at[pl.ds(dest_ref[0, tok * TOP_K + k], 1), :],
                                      buf.at[dst_slot, k, pl.ds(tok, 1), :], sem.at[dst_slot]).start(priority=k % 2)

    def tile_wait(dst_slot):
        for k in range(TOP_K):
            pltpu.make_async_copy(ys_ref.at[pl.ds(0, tc), :], buf.at[dst_slot, k], sem.at[dst_slot]).wait()

    @pl.when(i == 0)
    def _():
        start_gather(dest_cur_ref, 0)

    tile_wait(slot)
    start_gather(dest_next_ref, 1 - slot)
    w = w_ref[...]
    y = w[:, 0:1] * buf[slot, 0]
    for k in range(1, TOP_K):
        y = y + w[:, k:k + 1] * buf[slot, k]
    x2 = _ln(alpha * x_ref[...] + gate_ref[...] * y) * g_ref[...] + b_ref[...]
    x2_ref[...] = x2
    h_ref[...] = (_ln(x2) * (1.0 + sc_ref[...]) + sh_ref[...]).astype(h_ref.dtype)

    @pl.when(i == n_tiles - 1)
    def _():
        tile_wait(1 - slot)


def moe_combine(ys, dest, top_w, x1, mod_l, mod_next, ln_g, ln_b, seq, alpha):
    t, d = x1.shape
    tc = min(128, seq)
    bpb = seq // tc
    row = pl.BlockSpec((tc, d), lambda i: (i, 0))
    vec = pl.BlockSpec((1, d), lambda i: (0, 0))
    n_tiles = t // tc
    dest3 = dest.reshape(n_tiles, 1, tc * TOP_K)
    return pl.pallas_call(
        functools.partial(_combine_kernel, tc=tc, alpha=alpha),
        grid=(n_tiles,),
        in_specs=[pl.BlockSpec((None, 1, tc * TOP_K), lambda i: (i, 0, 0), memory_space=pltpu.SMEM),
                  pl.BlockSpec((None, 1, tc * TOP_K), lambda i: (jnp.minimum(i + 1, n_tiles - 1), 0, 0),
                               memory_space=pltpu.SMEM),
                  pl.BlockSpec(memory_space=pl.ANY),
                  pl.BlockSpec((tc, LANES), lambda i: (i, 0)),
                  row, _mod_spec(5, d, bpb), vec, vec, _mod_spec(1, d, bpb), _mod_spec(0, d, bpb)],
        out_specs=[row, row],
        out_shape=[jax.ShapeDtypeStruct((t, d), F32), jax.ShapeDtypeStruct((t, d), BF16)],
        scratch_shapes=[pltpu.VMEM((2, TOP_K, tc, d), F32), pltpu.SemaphoreType.DMA((2,))],
        compiler_params=_cparams(("arbitrary",), 32),
        name="moe_combine",
    )(dest3, dest3, ys, top_w, x1, mod_l, ln_g.reshape(1, d), ln_b.reshape(1, d), mod_next, mod_next)


def moe_ffn(h2, logits, w_glu, w_lin, up_offset, b_glu, b_lin, w_down, b_down):
    t, d = h2.shape
    n_exp = w_down.shape[0]
    row_block = 256
    idx_rank, top_w, counts = route(logits, n_exp)
    top_idx = idx_rank[:, :TOP_K]
    rank = idx_rank[:, TOP_K:2 * TOP_K]
    counts = counts[0, :n_exp].astype(jnp.int32)
    padded = (counts + row_block - 1) // row_block * row_block
    padded_end = jnp.cumsum(padded)
    padded_start = padded_end - padded
    dest = padded_start[top_idx] + rank
    n_rows = t * TOP_K + n_exp * row_block
    tok_ids = jnp.broadcast_to(jnp.arange(t, dtype=jnp.int32)[:, None], (t, TOP_K))
    row_tok = jnp.zeros((n_rows,), jnp.int32).at[dest.reshape(-1)].set(tok_ids.reshape(-1), unique_indices=True)
    n_blocks = n_rows // row_block
    block_start = jnp.arange(n_blocks, dtype=jnp.int32) * row_block
    block_e = jnp.minimum(jnp.sum(padded_end[None, :] <= block_start[:, None], axis=1), n_exp - 1).astype(jnp.int32)
    ys = expert_ffn(h2, row_tok, block_e, w_glu, w_lin, up_offset, b_glu, b_lin, w_down, b_down, row_block)
    return ys, dest.astype(jnp.int32), top_w


def kernel(x, c, w_ada, b_ada, w_in, b_forget, w_gate, b_gate, w_proj_sb, w_proj_dil, w_proj_fox, w_out,
           ln1_g, ln1_b, w_router, b_router, w_up, b_up, w_down, b_down, ln2_g, ln2_b):
    batch, seq, d = x.shape
    depth = w_ada.shape[0]
    t = batch * seq
    alpha = (2.0 * depth) ** 0.25
    rope_tabs = rope_pair_tables(seq)
    mod = adaln_mod(c, w_ada, b_ada)
    x2d = x.reshape(t, d)
    h = ln_mod(x2d, mod[0], seq, comp_shift=0, comp_scale=1)
    n_exp = w_up.shape[1]
    w_glu, w_lin = prepare_up_weights(w_up)
    for l in range(depth):
        w_in_l = w_in[l]
        qkv = in_projection(h, w_in_l[:, :N_QKV].astype(BF16), rope_tabs, seq)
        q_feat, k_feat = forget_features(h, w_in_l[:, N_QKV:], b_forget[l], batch, seq)
        o_sb = stick_breaking_attention(qkv, batch, seq)
        o_dl = dilated_window_attention(qkv, batch, seq)
        o_fx = forgetting_attention(qkv, q_feat, k_feat, batch, seq)
        merged = gated_merge(h, w_gate[l].astype(BF16), b_gate[l], (o_sb, o_dl, o_fx),
                             (w_proj_sb[l].astype(BF16), w_proj_dil[l].astype(BF16),
                              w_proj_fox[l].astype(BF16)))
        x1, h2, logits = out_projection(merged, w_out[l].astype(BF16), x2d, mod[l], ln1_g[l], ln1_b[l],
                                        w_router[l], b_router[l], seq, alpha)
        ys, dest, top_w = moe_ffn(h2, logits, w_glu, w_lin, l * n_exp,
                                  b_up[l][:, 0::2], b_up[l][:, 1::2], w_down[l].astype(BF16), b_down[l])
        mod_next = mod[min(l + 1, depth - 1)]
        x2d, h = moe_combine(ys, dest, top_w, x1, mod[l], mod_next, ln2_g[l], ln2_b[l], seq, alpha)
    return x2d.reshape(batch, seq, d)
```

```python
import functools

import numpy as np
import jax
import jax.numpy as jnp
from jax import lax
from jax.experimental import pallas as pl
from jax.experimental.pallas import tpu as pltpu

F32 = jnp.float32
BF16 = jnp.bfloat16

HEAD_DIM = 64
H_SB = 8
DIL_CONFIGS = ((128, 1), (512, 4), (2048, 16))
H_DIL_PER_GROUP = 4
H_FOX = 12
ROPE_THETA = 500000.0
ROPE_DIMS = HEAD_DIM // 4
TOP_K = 4
SWIGLU_ALPHA = 1.702
SWIGLU_LIMIT = 7.0
LN_EPS = 1e-5
QK_SCALE = HEAD_DIM ** -0.5

LANES = 128
SUBLANES = 8
VMEM_BUDGET_MB = 56

N_DIL_GROUPS = len(DIL_CONFIGS)
W_SB = H_SB * HEAD_DIM
W_DIL = N_DIL_GROUPS * H_DIL_PER_GROUP * HEAD_DIM
W_DIL_OUT = H_DIL_PER_GROUP * HEAD_DIM
W_FOX = H_FOX * HEAD_DIM
N_QKV = 3 * (W_SB + W_DIL + W_FOX)
COL_SB = (0, W_SB // LANES, 2 * W_SB // LANES)
_DL0 = 3 * W_SB // LANES
COL_DL = (_DL0, _DL0 + W_DIL // LANES, _DL0 + 2 * W_DIL // LANES)
_FX0 = _DL0 + 3 * W_DIL // LANES
COL_FX = (_FX0, _FX0 + W_FOX // LANES, _FX0 + 2 * W_FOX // LANES)
BAND = 128
NEG_BIG = -1e30
EXP_UNDERFLOW = -104.0


def _cparams(semantics, vmem_mb=VMEM_BUDGET_MB):
    return pltpu.CompilerParams(dimension_semantics=semantics, vmem_limit_bytes=vmem_mb * 2 ** 20)


def _dot(a, b):
    return jnp.dot(a, b, preferred_element_type=F32)


def _dot_t(a, b):
    return lax.dot_general(a, b, (((1,), (1,)), ((), ())), preferred_element_type=F32)


def _ln(x):
    mu = jnp.mean(x, axis=-1, keepdims=True)
    xc = x - mu
    var = jnp.mean(xc * xc, axis=-1, keepdims=True)
    return xc * lax.rsqrt(var + LN_EPS)


def _softplus_neg_abs(z):
    return jnp.log1p(jnp.exp(-jnp.abs(z)))


def _split3(x):
    hi = x.astype(BF16)
    r1 = x - hi.astype(F32)
    mid = r1.astype(BF16)
    lo = (r1 - mid.astype(F32)).astype(BF16)
    return hi, mid, lo


def _mod_kernel(c_ref, w_ref, b_ref, o_ref):
    c = c_ref[...]
    act = (c * jax.nn.sigmoid(c)).astype(BF16)
    o_ref[...] = _dot(act, w_ref[...].astype(BF16)) + b_ref[...]


def adaln_mod(c, w_ada, b_ada):
    n_layers, d, n = w_ada.shape
    b = c.shape[0]
    assert b <= SUBLANES
    tn = 512
    c_pad = jnp.zeros((SUBLANES, d), F32).at[:b].set(c)
    out = pl.pallas_call(
        _mod_kernel,
        grid=(n_layers, n // tn),
        in_specs=[pl.BlockSpec((SUBLANES, d), lambda l, j: (0, 0)),
                  pl.BlockSpec((None, d, tn), lambda l, j: (l, 0, j)),
                  pl.BlockSpec((None, 1, tn), lambda l, j: (l, 0, j))],
        out_specs=pl.BlockSpec((None, SUBLANES, tn), lambda l, j: (l, 0, j)),
        out_shape=jax.ShapeDtypeStruct((n_layers, SUBLANES, n), F32),
        compiler_params=_cparams(("arbitrary", "arbitrary"), 32),
        name="adaln_mod",
    )(c_pad, w_ada, b_ada.reshape(n_layers, 1, n))
    return out[:, :b].reshape(n_layers, b, 6, d).transpose(0, 2, 1, 3)[:, :, :, None, :]


def _mod_spec(comp, d, rows_per_batch_blocks):
    return pl.BlockSpec((None, None, 1, d), lambda i, *_: (comp, i // rows_per_batch_blocks, 0, 0))


def _ln_mod_kernel(x_ref, sc_ref, sh_ref, h_ref):
    h_ref[...] = (_ln(x_ref[...]) * (1.0 + sc_ref[...]) + sh_ref[...]).astype(h_ref.dtype)


def ln_mod(x2d, mod_l, seq, comp_shift, comp_scale):
    t, d = x2d.shape
    ts = min(512, seq)
    bpb = seq // ts
    return pl.pallas_call(
        _ln_mod_kernel,
        grid=(t // ts,),
        in_specs=[pl.BlockSpec((ts, d), lambda i: (i, 0)),
                  _mod_spec(comp_scale, d, bpb), _mod_spec(comp_shift, d, bpb)],
        out_specs=pl.BlockSpec((ts, d), lambda i: (i, 0)),
        out_shape=jax.ShapeDtypeStruct((t, d), BF16),
        compiler_params=_cparams(("arbitrary",), 32),
        name="ln_mod",
    )(x2d, mod_l, mod_l)


def _inproj_kernel(h_ref, w_ref, cos_ref, s1_ref, s2_ref, o_ref, *, rope_lo, rope_hi, tn):
    j = pl.program_id(1)
    acc = _dot(h_ref[...], w_ref[...])
    is_rope = jnp.logical_and(j >= rope_lo, j < rope_hi)

    @pl.when(is_rope)
    def _():
        c, s1, s2 = cos_ref[...], s1_ref[...], s2_ref[...]
        for blk in range(tn // LANES):
            a = acc[:, blk * LANES:(blk + 1) * LANES]
            r = a * c + pltpu.roll(a, LANES - ROPE_DIMS // 2, 1) * s1 + pltpu.roll(a, ROPE_DIMS // 2, 1) * s2
            o_ref[:, blk * LANES:(blk + 1) * LANES] = r.astype(o_ref.dtype)

    @pl.when(jnp.logical_not(is_rope))
    def _():
        o_ref[...] = acc.astype(o_ref.dtype)


def rope_pair_tables(seq):
    pos = jnp.arange(seq, dtype=F32)
    inv = ROPE_THETA ** (-jnp.arange(0, ROPE_DIMS, 2, dtype=F32) / ROPE_DIMS)
    ang = pos[:, None] * inv[None, :]
    cos, sin = jnp.cos(ang), jnp.sin(ang)
    half = ROPE_DIMS // 2
    head_c = jnp.concatenate([cos, cos, jnp.ones((seq, HEAD_DIM - ROPE_DIMS), F32)], axis=1)
    head_s1 = jnp.concatenate([-sin, jnp.zeros((seq, HEAD_DIM - half), F32)], axis=1)
    head_s2 = jnp.concatenate([jnp.zeros((seq, half), F32), sin,
                               jnp.zeros((seq, HEAD_DIM - ROPE_DIMS), F32)], axis=1)
    two = lambda a: jnp.concatenate([a, a], axis=1)
    return two(head_c), two(head_s1), two(head_s2)


def in_projection(h, w_qkv, rope_tabs, seq):
    t, d = h.shape
    n = w_qkv.shape[1]
    tm = min(1024, seq)
    tn = 768
    assert seq % tm == 0 and n % tn == 0
    rope_lo, rope_hi = COL_DL[0] * LANES, COL_DL[2] * LANES
    assert rope_lo % tn == 0 and rope_hi % tn == 0
    spb = seq // tm
    tab_spec = pl.BlockSpec((tm, LANES), lambda i, j: (i % spb, 0))
    return pl.pallas_call(
        functools.partial(_inproj_kernel, rope_lo=rope_lo // tn, rope_hi=rope_hi // tn, tn=tn),
        grid=(t // tm, n // tn),
        in_specs=[pl.BlockSpec((tm, d), lambda i, j: (i, 0)),
                  pl.BlockSpec((d, tn), lambda i, j: (0, j)),
                  tab_spec, tab_spec, tab_spec],
        out_specs=pl.BlockSpec((tm, tn), lambda i, j: (i, j)),
        out_shape=jax.ShapeDtypeStruct((t, n), BF16),
        compiler_params=_cparams(("arbitrary", "arbitrary"), 48),
        name="in_projection",
    )(h, w_qkv, *rope_tabs)


FEAT_PER_HEAD = 6


def _forget_feature_maps():
    width = (H_FOX // 2) * LANES
    pq = np.zeros((3 * LANES, width), np.float32)
    pk = np.zeros((3 * LANES, width), np.float32)
    cq = np.zeros((1, width), np.float32)
    ck = np.zeros((1, width), np.float32)
    for head in range(H_FOX):
        base = (head // 2) * LANES + (head % 2) * FEAT_PER_HEAD
        for piece in range(3):
            pq[piece * LANES + head, base + piece] = 1.0
            pk[piece * LANES + head, base + 3 + piece] = -1.0
        cq[0, base + 3:base + 6] = 1.0
        ck[0, base:base + 3] = 1.0
    return pq, pk, cq, ck


def _forget_kernel(h_ref, w_ref, b_ref, tri_ref, pq_ref, pk_ref, cq_ref, ck_ref, qf_ref, kf_ref, carry_ref):
    @pl.when(pl.program_id(1) == 0)
    def _():
        carry_ref[...] = jnp.zeros_like(carry_ref)

    f = _dot(h_ref[...], w_ref[...]) + b_ref[...]
    log_f = jnp.minimum(f, 0.0) - _softplus_neg_abs(f)
    tri = tri_ref[...]
    hi, mid, lo = _split3(log_f)
    cum = _dot(tri, hi) + _dot(tri, mid) + _dot(tri, lo) + carry_ref[0:1, :]
    carry_ref[...] = jnp.broadcast_to(cum[-1:, :], carry_ref.shape)
    pieces = jnp.concatenate(_split3(cum), axis=1)
    qf_ref[...] = (_dot(pieces, pq_ref[...]) + cq_ref[...]).astype(qf_ref.dtype)
    kf_ref[...] = (_dot(pieces, pk_ref[...]) + ck_ref[...]).astype(kf_ref.dtype)


def forget_features(h, w_f, b_f, batch, seq):
    t, d = h.shape
    ts = min(256, seq)
    nsb = seq // ts
    w_pad = jnp.zeros((d, LANES), BF16).at[:, :H_FOX].set(w_f.astype(BF16))
    b_pad = jnp.zeros((1, LANES), F32).at[0, :H_FOX].set(b_f)
    tri = (np.arange(ts)[:, None] >= np.arange(ts)[None, :]).astype(np.float32)
    pq, pk, cq, ck = _forget_feature_maps()
    width = pq.shape[1]
    const = lambda shape: pl.BlockSpec(shape, lambda b, s: (0, 0))
    out_spec = pl.BlockSpec((ts, width), lambda b, s: (b * nsb + s, 0))
    out_shape = jax.ShapeDtypeStruct((t, width), BF16)
    return pl.pallas_call(
        _forget_kernel,
        grid=(batch, nsb),
        in_specs=[pl.BlockSpec((ts, d), lambda b, s: (b * nsb + s, 0)),
                  const((d, LANES)), const((1, LANES)), const((ts, ts)),
                  const(pq.shape), const(pk.shape), const(cq.shape), const(ck.shape)],
        out_specs=[out_spec, out_spec],
        out_shape=[out_shape, out_shape],
        scratch_shapes=[pltpu.VMEM((SUBLANES, LANES), F32)],
        compiler_params=_cparams(("arbitrary", "arbitrary"), 32),
        name="forget_features",
    )(h, w_pad, b_pad, jnp.asarray(tri, BF16), jnp.asarray(pq, BF16), jnp.asarray(pk, BF16),
      jnp.asarray(cq), jnp.asarray(ck))


def _pair_masks(tq):
    lane = lax.broadcasted_iota(jnp.int32, (tq, LANES), 1)
    return lane < HEAD_DIM


KEY_SUB = 128
QRY_SUB = 128
UP_CHUNK = 256
SUFFIX_SUB = 256


def _diag_tile_kind(ks, qs, key_sub, qry_sub, strict, key_off):
    if key_off is None:
        return 'full'
    k_lo, k_hi = key_off + ks * key_sub, key_off + ks * key_sub + key_sub - 1
    q_lo, q_hi = qs * qry_sub, qs * qry_sub + qry_sub - 1
    if strict:
        if k_lo >= q_hi:
            return 'dead'
        return 'full' if k_hi < q_lo else 'partial'
    if k_lo > q_hi:
        return 'dead'
    return 'full' if k_hi <= q_lo else 'partial'


def _tile_allowed(ks, qs, key_sub, qry_sub, strict, key_off):
    kpos = key_off + ks * key_sub + lax.broadcasted_iota(jnp.int32, (key_sub, qry_sub), 0)
    qpos = qs * qry_sub + lax.broadcasted_iota(jnp.int32, (key_sub, qry_sub), 1)
    return kpos < qpos if strict else kpos <= qpos


def _pair_transpose_in(qkv3, col, width):
    return lax.slice_in_dim(qkv3, col * LANES, col * LANES + width, axis=2).transpose(0, 2, 1)


def _sb_kernel(q_ref, k_ref, vt_ref, later_ref, o_ref,
               qx_s, acc_s, run_s, first_s, z_s, hi_s, lo_s, btw_s, w_s, *, tq, tk, key_sub, qry_sub):
    q_blk = pl.program_id(2)
    head_a = _pair_masks(tq)
    q = q_ref[...] * QK_SCALE
    qx_s[0] = jnp.where(head_a, q, 0).astype(BF16)
    qx_s[1] = jnp.where(head_a, 0, q).astype(BF16)
    acc_s[...] = jnp.zeros_like(acc_s)
    run_s[...] = jnp.zeros_like(run_s)

    def step(key_off, kv_blk):
        k_ref_blk = k_ref.at[pl.ds(pl.multiple_of(kv_blk * tk, tk), tk), :]
        vt_blk = vt_ref.at[kv_blk]
        _sb_block(key_off, k_ref_blk, vt_blk)

    def _sb_block(key_off, k_ref, vt_ref):
        suf = later_ref.shape[0]
        n_suf = tk // suf
        per_suf = suf // key_sub
        tiles = [(ks, qs) for ks in range(tk // key_sub) for qs in range(tq // qry_sub)]
        kind_of = lambda ks, qs: _diag_tile_kind(ks, qs, key_sub, qry_sub, True, key_off)
        k = k_ref[...]
        for head in range(2):
            z_s[head] = _dot_t(k, qx_s[head])
        for head in range(2):
            for ks, qs in tiles:
                kl = slice(ks * key_sub, (ks + 1) * key_sub)
                ql = slice(qs * qry_sub, (qs + 1) * qry_sub)
                first_row = head * n_suf + ks // per_suf
                kind = kind_of(ks, qs)
                if kind == 'dead':
                    hi_s[head, kl, ql] = jnp.zeros((key_sub, qry_sub), BF16)
                    lo_s[head, kl, ql] = jnp.zeros((key_sub, qry_sub), BF16)
                    if ks % per_suf == 0:
                        first_s[first_row:first_row + 1, ql] = jnp.zeros((1, qry_sub), F32)
                    continue
                z = z_s[head, kl, ql]
                log_stop = jnp.minimum(z, 0.0) - jnp.log(1.0 + jnp.exp(-jnp.abs(z)))
                log_cont = log_stop - z
                if kind == 'partial':
                    log_cont = jnp.where(_tile_allowed(ks, qs, key_sub, qry_sub, True, key_off), log_cont, 0.0)
                z_s[head, kl, ql] = log_stop
                hi = log_cont.astype(BF16)
                hi_s[head, kl, ql] = hi
                lo_s[head, kl, ql] = (log_cont - hi.astype(F32)).astype(BF16)
                if ks % per_suf == 0:
                    first_s[first_row:first_row + 1, ql] = log_cont[0:1, :]
        later = later_ref[...]
        for head in range(2):
            for blk in range(n_suf):
                bl = slice(blk * suf, (blk + 1) * suf)
                btw_s[head, bl, :] = _dot(later, hi_s[head, bl, :]) + _dot(later, lo_s[head, bl, :])

        def block_sum(head, blk, ql):
            row = head * n_suf + blk
            return btw_s[head, blk * suf:blk * suf + 1, ql] + first_s[row:row + 1, ql]

        for head in range(2):
            for ks, qs in tiles:
                kl = slice(ks * key_sub, (ks + 1) * key_sub)
                ql = slice(qs * qry_sub, (qs + 1) * qry_sub)
                kind = kind_of(ks, qs)
                if kind == 'dead':
                    w_s[head, kl, ql] = jnp.zeros((key_sub, qry_sub), BF16)
                    continue
                after = run_s[head:head + 1, ql]
                for blk in range(ks // per_suf + 1, n_suf):
                    after = after + block_sum(head, blk, ql)
                w = jnp.exp(z_s[head, kl, ql] + btw_s[head, kl, ql] + after)
                if kind == 'partial':
                    w = jnp.where(_tile_allowed(ks, qs, key_sub, qry_sub, True, key_off), w, 0.0)
                w_s[head, kl, ql] = w.astype(BF16)
        for head in range(2):
            rows = slice(head * HEAD_DIM, (head + 1) * HEAD_DIM)
            acc_s[rows, :] += _dot(vt_ref[rows, :], w_s[head])
            total = run_s[head:head + 1, :]
            for blk in range(n_suf):
                total = total + block_sum(head, blk, slice(None))
            run_s[head:head + 1, :] = total

    def all_weights_zero():
        return jnp.max(run_s[0:2, :]) < EXP_UNDERFLOW

    step(0, q_blk)

    def more(carry):
        kv_blk, dead = carry
        return jnp.logical_and(kv_blk >= 0, jnp.logical_not(dead))

    def visit(carry):
        kv_blk, _ = carry
        step(None, kv_blk)
        return kv_blk - 1, all_weights_zero()

    lax.while_loop(more, visit, (q_blk - 1, all_weights_zero()))
    o_ref[...] = acc_s[...].astype(o_ref.dtype)


def stick_breaking_attention(qkv, batch, seq):
    tq = tk = min(512, seq)
    key_sub, qry_sub = min(KEY_SUB, tk), min(QRY_SUB, tq)
    n_pairs = W_SB // LANES
    n_kv = seq // tk
    qkv3 = qkv.reshape(batch, seq, qkv.shape[-1])
    qc, kc, vc = COL_SB
    v_t = _pair_transpose_in(qkv3, vc, W_SB).reshape(batch, n_pairs, LANES, n_kv, tk).transpose(0, 1, 3, 2, 4)
    suf = min(SUFFIX_SUB, tk)
    assert 2 * (tk // suf) <= SUBLANES and suf % key_sub == 0
    later = jnp.asarray((np.arange(suf)[None, :] > np.arange(suf)[:, None]).astype(np.float32), BF16)
    block_f32 = pltpu.VMEM((2, tk, tq), F32)
    block_bf16 = pltpu.VMEM((2, tk, tq), BF16)
    row_stats = pltpu.VMEM((SUBLANES, tq), F32)
    out_t = pl.pallas_call(
        functools.partial(_sb_kernel, tq=tq, tk=tk, key_sub=key_sub, qry_sub=qry_sub),
        grid=(batch, n_pairs, seq // tq),
        in_specs=[pl.BlockSpec((None, tq, LANES), lambda b, p, i: (b, i, qc + p)),
                  pl.BlockSpec((None, seq, LANES), lambda b, p, i: (b, 0, kc + p)),
                  pl.BlockSpec((None, None, n_kv, LANES, tk), lambda b, p, i: (b, p, 0, 0, 0)),
                  pl.BlockSpec((suf, suf), lambda b, p, i: (0, 0))],
        out_specs=pl.BlockSpec((None, LANES, tq), lambda b, p, i: (b, p, i)),
        scratch_shapes=[pltpu.VMEM((2, tq, LANES), BF16), pltpu.VMEM((LANES, tq), F32),
                        row_stats, row_stats, block_f32, block_bf16, block_bf16, block_f32, block_bf16],
        out_shape=jax.ShapeDtypeStruct((batch, W_SB, seq), BF16),
        compiler_params=_cparams(("arbitrary", "arbitrary", "arbitrary"), 32),
        name="stick_breaking_attention",
    )(qkv3, qkv3, v_t, later)
    return out_t.transpose(0, 2, 1).reshape(batch * seq, W_SB)


def _fox_kernel(q_ref, qf_ref, k_ref, kf_ref, vt_ref, o_ref,
                qx_s, acc_s, m_s, l_s, alpha_s, z_s, p_s, *, tq, tk, key_sub, qry_sub):
    q_blk = pl.program_id(2)
    lane = lax.broadcasted_iota(jnp.int32, (tq, LANES), 1)
    q = q_ref[...] * QK_SCALE
    qf = qf_ref[...]
    for head in range(2):
        in_head = (lane >= head * HEAD_DIM) & (lane < (head + 1) * HEAD_DIM)
        in_feat = (lane >= head * FEAT_PER_HEAD) & (lane < (head + 1) * FEAT_PER_HEAD)
        qx_s[head] = jnp.concatenate([jnp.where(in_head, q, 0).astype(BF16),
                                      jnp.where(in_feat, qf, 0).astype(BF16)], axis=1)
    acc_s[...] = jnp.zeros_like(acc_s)
    l_s[...] = jnp.zeros_like(l_s)
    m_s[...] = jnp.full_like(m_s, NEG_BIG)

    def step(key_off, kv_blk):
        rows = pl.ds(pl.multiple_of(kv_blk * tk, tk), tk)
        _fox_block(key_off, k_ref.at[rows, :], kf_ref.at[rows, :], vt_ref.at[kv_blk])

    def _fox_block(key_off, k_ref, kf_ref, vt_ref):
        kind_of = lambda ks, qs: _diag_tile_kind(ks, qs, key_sub, qry_sub, False, key_off)
        kx = jnp.concatenate([k_ref[...], kf_ref[...]], axis=1)
        for head in range(2):
            z_s[head] = _dot_t(kx, qx_s[head])
        for head in range(2):
            for qs in range(tq // qry_sub):
                ql = slice(qs * qry_sub, (qs + 1) * qry_sub)

                def score(ks):
                    z = z_s[head, ks * key_sub:(ks + 1) * key_sub, ql]
                    if kind_of(ks, qs) == 'partial':
                        z = jnp.where(_tile_allowed(ks, qs, key_sub, qry_sub, False, key_off), z, NEG_BIG)
                    return z

                live = [ks for ks in range(tk // key_sub) if kind_of(ks, qs) != 'dead']
                m_old = m_s[head:head + 1, ql]
                m_new = m_old
                for ks in live:
                    m_new = jnp.maximum(m_new, jnp.max(score(ks), axis=0, keepdims=True))
                alpha = jnp.exp(m_old - m_new)
                total = alpha * l_s[head:head + 1, ql]
                for ks in range(tk // key_sub):
                    kl = slice(ks * key_sub, (ks + 1) * key_sub)
                    if ks not in live:
                        p_s[head, kl, ql] = jnp.zeros((key_sub, qry_sub), BF16)
                        continue
                    prob = jnp.exp(score(ks) - m_new)
                    total = total + jnp.sum(prob, axis=0, keepdims=True)
                    p_s[head, kl, ql] = prob.astype(BF16)
                m_s[head:head + 1, ql] = m_new
                l_s[head:head + 1, ql] = total
                alpha_s[head:head + 1, ql] = alpha
        for head in range(2):
            rows = slice(head * HEAD_DIM, (head + 1) * HEAD_DIM)
            acc_s[rows, :] = alpha_s[head:head + 1, :] * acc_s[rows, :] + _dot(vt_ref[rows, :], p_s[head])

    per_q = tq // tk
    first_diag = q_blk * per_q

    def visit(kv_blk, carry):
        step(None, kv_blk)
        return carry

    lax.fori_loop(0, first_diag, visit, 0)
    for variant in range(per_q):
        step(variant * tk, first_diag + variant)
    for head in range(2):
        rows = slice(head * HEAD_DIM, (head + 1) * HEAD_DIM)
        o_ref[rows, :] = (acc_s[rows, :] / l_s[head:head + 1, :]).astype(o_ref.dtype)


def forgetting_attention(qkv, q_feat, k_feat, batch, seq):
    tk = min(512, seq)
    tq = min(1024, seq)
    key_sub, qry_sub = min(KEY_SUB, tk), min(QRY_SUB, tq)
    assert tq % tk == 0
    n_pairs = W_FOX // LANES
    n_kv = seq // tk
    qkv3 = qkv.reshape(batch, seq, qkv.shape[-1])
    qf3 = q_feat.reshape(batch, seq, n_pairs * LANES)
    kf3 = k_feat.reshape(batch, seq, n_pairs * LANES)
    qc, kc, vc = COL_FX
    v_t = _pair_transpose_in(qkv3, vc, W_FOX).reshape(batch, n_pairs, LANES, n_kv, tk).transpose(0, 1, 3, 2, 4)
    out_t = pl.pallas_call(
        functools.partial(_fox_kernel, tq=tq, tk=tk, key_sub=key_sub, qry_sub=qry_sub),
        grid=(batch, n_pairs, seq // tq),
        in_specs=[pl.BlockSpec((None, tq, LANES), lambda b, p, i: (b, i, qc + p)),
                  pl.BlockSpec((None, tq, LANES), lambda b, p, i: (b, i, p)),
                  pl.BlockSpec((None, seq, LANES), lambda b, p, i: (b, 0, kc + p)),
                  pl.BlockSpec((None, seq, LANES), lambda b, p, i: (b, 0, p)),
                  pl.BlockSpec((None, None, n_kv, LANES, tk), lambda b, p, i: (b, p, 0, 0, 0))],
        out_specs=pl.BlockSpec((None, LANES, tq), lambda b, p, i: (b, p, i)),
        scratch_shapes=[pltpu.VMEM((2, tq, 2 * LANES), BF16), pltpu.VMEM((LANES, tq), F32),
                        pltpu.VMEM((SUBLANES, tq), F32), pltpu.VMEM((SUBLANES, tq), F32),
                        pltpu.VMEM((SUBLANES, tq), F32), pltpu.VMEM((2, tk, tq), F32),
                        pltpu.VMEM((2, tk, tq), BF16)],
        out_shape=jax.ShapeDtypeStruct((batch, W_FOX, seq), BF16),
        compiler_params=_cparams(("arbitrary", "arbitrary", "arbitrary"), 40),
        name="forgetting_attention",
    )(qkv3, qf3, qkv3, kf3, v_t)
    return out_t.transpose(0, 2, 1).reshape(batch * seq, W_FOX)


def _band_kernel(q_ref, kc_ref, vc_ref, kp_ref, vp_ref, o_ref, lse_ref, *, tq):
    i = pl.program_id(1)
    head_a = _pair_masks(tq)
    diff = lax.broadcasted_iota(jnp.int32, (tq, tq), 0) - lax.broadcasted_iota(jnp.int32, (tq, tq), 1)
    mask_c = jnp.logical_and(diff >= 0, diff <= BAND)
    rp = lax.broadcasted_iota(jnp.int32, (tq, BAND), 0)
    cp = lax.broadcasted_iota(jnp.int32, (tq, BAND), 1)
    mask_p = jnp.logical_and(cp >= rp, i > 0)
    for pair in range(q_ref.shape[1] // LANES):
        pl_ = slice(pair * LANES, (pair + 1) * LANES)
        q = q_ref[:, pl_] * QK_SCALE
        kc, vc, kp, vp = kc_ref[:, pl_], vc_ref[:, pl_], kp_ref[:, pl_], vp_ref[:, pl_]
        outs, lses = [], []
        for sel in (head_a, jnp.logical_not(head_a)):
            qh = jnp.where(sel, q, 0).astype(BF16)
            sc = jnp.where(mask_c, _dot_t(qh, kc), NEG_BIG)
            sp = jnp.where(mask_p, _dot_t(qh, kp), NEG_BIG)
            m = jnp.maximum(jnp.max(sc, axis=1, keepdims=True), jnp.max(sp, axis=1, keepdims=True))
            ec = jnp.exp(sc - m)
            ep = jnp.exp(sp - m)
            den = jnp.sum(ec, axis=1, keepdims=True) + jnp.sum(ep, axis=1, keepdims=True)
            outs.append((_dot(ec.astype(BF16), vc) + _dot(ep.astype(BF16), vp)) / den)
            lses.append(m + jnp.log(den))
        o_ref[:, pl_] = jnp.where(head_a, outs[0], outs[1])
        lse_ref[:, pl_] = jnp.where(head_a, lses[0], lses[1])


def _band_attention(q, k, v):
    nb, u, w = q.shape
    tq = min(256, u)
    assert u % tq == 0 and tq % BAND == 0
    sub = tq // BAND
    cur = pl.BlockSpec((None, tq, w), lambda n, i: (n, i, 0))
    prev = pl.BlockSpec((None, BAND, w), lambda n, i: (n, jnp.maximum(i * sub - 1, 0), 0))
    shp = jax.ShapeDtypeStruct((nb, u, w), F32)
    return pl.pallas_call(
        functools.partial(_band_kernel, tq=tq),
        grid=(nb, u // tq),
        in_specs=[cur, cur, cur, prev, prev],
        out_specs=[cur, cur],
        out_shape=[shp, shp],
        compiler_params=_cparams(("arbitrary", "arbitrary"), 32),
        name="band_attention",
    )(q, k, v, k, v)


def _dil_mix_kernel(o0, o1, o2, l0, l1, l2, out_ref):
    a, b, c = l0[...], l1[...], l2[...]
    m = jnp.maximum(jnp.maximum(a, b), c)
    ea, eb, ec = jnp.exp(a - m), jnp.exp(b - m), jnp.exp(c - m)
    tot = ea + eb + ec
    out_ref[...] = ((o0[...] * ea + o1[...] * eb + o2[...] * ec) / tot).astype(out_ref.dtype)


def dilated_window_attention(qkv, batch, seq):
    t = batch * seq
    qkv3 = qkv.reshape(batch, seq, qkv.shape[-1])
    gw = W_DIL_OUT
    outs, lses = [], []
    for g, (window, dil) in enumerate(DIL_CONFIGS):
        assert window // dil == BAND and seq % (dil * BAND) == 0
        u = seq // dil

        def split(col):
            a = lax.slice_in_dim(qkv3, col * LANES + g * gw, col * LANES + (g + 1) * gw, axis=2)
            return a.reshape(batch, u, dil, gw).transpose(0, 2, 1, 3).reshape(batch * dil, u, gw)

        o, lse = _band_attention(split(COL_DL[0]), split(COL_DL[1]), split(COL_DL[2]))
        merge = lambda a: a.reshape(batch, dil, u, gw).transpose(0, 2, 1, 3).reshape(t, gw)
        outs.append(merge(o))
        lses.append(merge(lse))
    tm = min(1024, t)
    spec = pl.BlockSpec((tm, gw), lambda i: (i, 0))
    return pl.pallas_call(
        _dil_mix_kernel,
        grid=(t // tm,),
        in_specs=[spec] * 6,
        out_specs=spec,
        out_shape=jax.ShapeDtypeStruct((t, gw), BF16),
        compiler_params=_cparams(("arbitrary",), 32),
        name="dilated_mix",
    )(*outs, *lses)


def _merge_kernel(h_ref, wg0, wg1, wg2, bg0, bg1, bg2, o0, o1, o2, wp0, wp1, wp2, out_ref):
    h = h_ref[...]

    def branch(wg, bg, o, wp):
        return jax.nn.sigmoid(_dot(h, wg[...]) + bg[...]) * _dot(o[...], wp[...])

    merged = branch(wg0, bg0, o0, wp0) + branch(wg1, bg1, o1, wp1) + branch(wg2, bg2, o2, wp2)
    out_ref[...] = merged.astype(out_ref.dtype)


def gated_merge(h, w_gate, b_gate, branch_outs, branch_projs):
    t, d = h.shape
    tm, tn = min(512, t), min(512, d)
    nj = d // tn
    b_gate2 = b_gate.reshape(1, -1)
    gate_w = [pl.BlockSpec((d, tn), lambda j, i, br=br: (0, br * nj + j)) for br in range(3)]
    gate_b = [pl.BlockSpec((1, tn), lambda j, i, br=br: (0, br * nj + j)) for br in range(3)]
    o_specs = [pl.BlockSpec((tm, o.shape[1]), lambda j, i: (i, 0)) for o in branch_outs]
    p_specs = [pl.BlockSpec((w.shape[0], tn), lambda j, i: (0, j)) for w in branch_projs]
    return pl.pallas_call(
        _merge_kernel,
        grid=(nj, t // tm),
        in_specs=[pl.BlockSpec((tm, d), lambda j, i: (i, 0))] + gate_w + gate_b + o_specs + p_specs,
        out_specs=pl.BlockSpec((tm, tn), lambda j, i: (i, j)),
        out_shape=jax.ShapeDtypeStruct((t, d), BF16),
        compiler_params=_cparams(("arbitrary", "arbitrary"), 48),
        name="gated_merge",
    )(h, w_gate, w_gate, w_gate, b_gate2, b_gate2, b_gate2, *branch_outs, *branch_projs)


def _outproj_kernel(m_ref, w_ref, x_ref, gate_ref, g_ref, b_ref, sc_ref, sh_ref, wrh_ref, wrl_ref, br_ref,
                    x1_ref, h2_ref, lg_ref, *, alpha):
    y = _dot(m_ref[...], w_ref[...])
    x1 = _ln(alpha * x_ref[...] + gate_ref[...] * y) * g_ref[...] + b_ref[...]
    x1_ref[...] = x1
    h2 = _ln(x1) * (1.0 + sc_ref[...]) + sh_ref[...]
    h2_ref[...] = h2
    hi = h2.astype(BF16)
    lo = (h2 - hi.astype(F32)).astype(BF16)
    wrh = wrh_ref[...]
    lg_ref[...] = _dot(hi, wrh) + _dot(lo, wrh) + _dot(hi, wrl_ref[...]) + br_ref[...]


def out_projection(merged, w_out, x2d, mod_l, ln_g, ln_b, w_router, b_router, seq, alpha):
    t, d = x2d.shape
    tm = min(256, seq)
    bpb = seq // tm
    n_exp = w_router.shape[1]
    wr = jnp.zeros((d, LANES), F32).at[:, :n_exp].set(w_router)
    wr_hi = wr.astype(BF16)
    wr_lo = (wr - wr_hi.astype(F32)).astype(BF16)
    br = jnp.zeros((1, LANES), F32).at[0, :n_exp].set(b_router)
    row = pl.BlockSpec((tm, d), lambda i: (i, 0))
    vec = pl.BlockSpec((1, d), lambda i: (0, 0))
    wr_spec = pl.BlockSpec((d, LANES), lambda i: (0, 0))
    return pl.pallas_call(
        functools.partial(_outproj_kernel, alpha=alpha),
        grid=(t // tm,),
        in_specs=[row, pl.BlockSpec((d, d), lambda i: (0, 0)), row, _mod_spec(2, d, bpb), vec, vec,
                  _mod_spec(4, d, bpb), _mod_spec(3, d, bpb), wr_spec, wr_spec,
                  pl.BlockSpec((1, LANES), lambda i: (0, 0))],
        out_specs=[row, row, pl.BlockSpec((tm, LANES), lambda i: (i, 0))],
        out_shape=[jax.ShapeDtypeStruct((t, d), F32), jax.ShapeDtypeStruct((t, d), F32),
                   jax.ShapeDtypeStruct((t, LANES), F32)],
        compiler_params=_cparams(("arbitrary",), 48),
        name="out_projection",
    )(merged, w_out, x2d, mod_l, ln_g.reshape(1, d), ln_b.reshape(1, d), mod_l, mod_l, wr_hi, wr_lo, br)


def _route_kernel(lg_ref, tri_ref, ir_ref, w_ref, cnt_ref, carry_ref, *, n_exp):
    @pl.when(pl.program_id(0) == 0)
    def _():
        carry_ref[...] = jnp.zeros_like(carry_ref)

    tr = lg_ref.shape[0]
    lane = lax.broadcasted_iota(jnp.int32, (tr, LANES), 1)
    logits = jnp.where(lane < n_exp, lg_ref[...], -jnp.inf)
    vals, sels, idxs = [], [], []
    for _ in range(TOP_K):
        m = jnp.max(logits, axis=1, keepdims=True)
        idx = jnp.min(jnp.where(logits == m, lane, LANES), axis=1, keepdims=True)
        sel = lane == idx
        vals.append(m)
        idxs.append(idx)
        sels.append(sel)
        logits = jnp.where(sel, -jnp.inf, logits)
    exps = [jnp.exp(v - vals[0]) for v in vals]
    tot = exps[0] + exps[1] + exps[2] + exps[3]
    onehot = jnp.zeros((tr, LANES), F32)
    for sel in sels:
        onehot = onehot + jnp.where(sel, 1.0, 0.0)
    before = _dot(tri_ref[...], onehot.astype(BF16)) + carry_ref[0:1, :]
    ir = jnp.zeros((tr, LANES), jnp.int32)
    wt = jnp.zeros((tr, LANES), F32)
    for k in range(TOP_K):
        rank = jnp.sum(jnp.where(sels[k], before, 0.0), axis=1, keepdims=True).astype(jnp.int32)
        ir = jnp.where(lane == k, idxs[k], ir)
        ir = jnp.where(lane == TOP_K + k, rank, ir)
        wt = jnp.where(lane == k, exps[k] / tot, wt)
    ir_ref[...] = ir
    w_ref[...] = wt
    total = before[-1:, :] + onehot[-1:, :]
    carry_ref[...] = jnp.broadcast_to(total, carry_ref.shape)
    cnt_ref[...] = jnp.broadcast_to(total, cnt_ref.shape)


def route(logits, n_exp):
    t = logits.shape[0]
    tr = min(512, t)
    tri = jnp.asarray((np.arange(tr)[:, None] > np.arange(tr)[None, :]).astype(np.float32), BF16)
    row = pl.BlockSpec((tr, LANES), lambda i: (i, 0))
    return pl.pallas_call(
        functools.partial(_route_kernel, n_exp=n_exp),
        grid=(t // tr,),
        in_specs=[row, pl.BlockSpec((tr, tr), lambda i: (0, 0))],
        out_specs=[row, row, pl.BlockSpec((SUBLANES, LANES), lambda i: (0, 0))],
        out_shape=[jax.ShapeDtypeStruct((t, LANES), jnp.int32), jax.ShapeDtypeStruct((t, LANES), F32),
                   jax.ShapeDtypeStruct((SUBLANES, LANES), F32)],
        scratch_shapes=[pltpu.VMEM((SUBLANES, LANES), F32)],
        compiler_params=_cparams(("arbitrary",), 32),
        name="route",
    )(logits, tri)


def _prep_up_kernel(w_ref, even_ref, odd_ref, g_ref, l_ref):
    w = w_ref[...].astype(BF16)
    g_ref[...] = _dot(w, even_ref[...]).astype(g_ref.dtype)
    l_ref[...] = _dot(w, odd_ref[...]).astype(l_ref.dtype)


def prepare_up_weights(w_up):
    d, f2 = w_up.shape[-2:]
    w3 = w_up.reshape(-1, d, f2)
    n = w3.shape[0]
    tn = min(512, f2)
    half = tn // 2
    even = np.zeros((tn, half), np.float32)
    odd = np.zeros((tn, half), np.float32)
    even[2 * np.arange(half), np.arange(half)] = 1.0
    odd[2 * np.arange(half) + 1, np.arange(half)] = 1.0
    sel = pl.BlockSpec((tn, half), lambda e, j: (0, 0))
    out_spec = pl.BlockSpec((None, d, half), lambda e, j: (e, 0, j))
    out_shape = jax.ShapeDtypeStruct((n, d, f2 // 2), BF16)
    return pl.pallas_call(
        _prep_up_kernel,
        grid=(n, f2 // tn),
        in_specs=[pl.BlockSpec((None, d, tn), lambda e, j: (e, 0, j)), sel, sel],
        out_specs=[out_spec, out_spec],
        out_shape=[out_shape, out_shape],
        compiler_params=_cparams(("arbitrary", "arbitrary"), 32),
        name="prepare_up_weights",
    )(w3, jnp.asarray(even, BF16), jnp.asarray(odd, BF16))


def _expert_kernel(be_ref, tok_cur_ref, tok_next_ref, h_ref, wg_ref, wl_ref, bg_ref, bl_ref, wd_ref, bd_ref,
                   y_ref, xbuf, act_s, sem, *, row_block):
    i = pl.program_id(0)
    n_blocks = pl.num_programs(0)
    slot = i % 2

    def block_wait(dst_slot):
        pltpu.make_async_copy(h_ref.at[pl.ds(0, row_block), :], xbuf.at[dst_slot], sem.at[dst_slot]).wait()

    def start_gather(tok_ref, dst_slot, rows, anchor):
        for r in rows:
            pltpu.make_async_copy(h_ref.at[pl.ds(tok_ref[0, r] + anchor, 1), :],
                                  xbuf.at[dst_slot, pl.ds(r, 1), :], sem.at[dst_slot]).start()

    @pl.when(i == 0)
    def _():
        start_gather(tok_cur_ref, 0, range(row_block), 0)

    block_wait(slot)
    x = xbuf[slot].astype(BF16)
    f = wg_ref.shape[1]
    up_chunk = min(UP_CHUNK, f)
    n_chunks = f // up_chunk
    rows_per_chunk = -(-row_block // n_chunks)
    for c in range(n_chunks):
        cl = slice(c * up_chunk, (c + 1) * up_chunk)
        glu = jnp.minimum(_dot(x, wg_ref[:, cl]) + bg_ref[:, cl], SWIGLU_LIMIT)
        lin = jnp.clip(_dot(x, wl_ref[:, cl]) + bl_ref[:, cl], -SWIGLU_LIMIT, SWIGLU_LIMIT)
        act_s[:, cl] = (glu * jax.nn.sigmoid(SWIGLU_ALPHA * glu) * (lin + 1.0)).astype(BF16)
        bits = lax.bitcast_convert_type(glu[0:1, 0:1], jnp.int32)[0, 0]
        anchor = (bits & 0x7FFFFFFF) >> 31
        start_gather(tok_next_ref, 1 - slot,
                     range(c * rows_per_chunk, min((c + 1) * rows_per_chunk, row_block)), anchor)
    y_ref[...] = _dot(act_s[...], wd_ref[...]) + bd_ref[...]

    @pl.when(i == n_blocks - 1)
    def _():
        block_wait(1 - slot)


def expert_ffn(h2, row_tok, block_e, w_glu, w_lin, up_offset, b_glu, b_lin, w_down, b_down, row_block):
    d = h2.shape[1]
    n_rows = row_tok.shape[0]
    n_blocks = n_rows // row_block
    n_exp = w_down.shape[0]
    f = w_glu.shape[2]
    weight_buffers = pl.Buffered(2)
    tok3 = row_tok.reshape(n_blocks, 1, row_block)
    grid_spec = pltpu.PrefetchScalarGridSpec(
        num_scalar_prefetch=1,
        grid=(n_blocks,),
        in_specs=[pl.BlockSpec((None, 1, row_block), lambda i, be: (i, 0, 0), memory_space=pltpu.SMEM),
                  pl.BlockSpec((None, 1, row_block), lambda i, be: (jnp.minimum(i + 1, n_blocks - 1), 0, 0),
                               memory_space=pltpu.SMEM),
                  pl.BlockSpec(memory_space=pl.ANY),
                  pl.BlockSpec((None, d, f), lambda i, be: (be[i] + up_offset, 0, 0), pipeline_mode=weight_buffers),
                  pl.BlockSpec((None, d, f), lambda i, be: (be[i] + up_offset, 0, 0), pipeline_mode=weight_buffers),
                  pl.BlockSpec((None, 1, f), lambda i, be: (be[i], 0, 0)),
                  pl.BlockSpec((None, 1, f), lambda i, be: (be[i], 0, 0)),
                  pl.BlockSpec((None, f, d), lambda i, be: (be[i], 0, 0), pipeline_mode=weight_buffers),
                  pl.BlockSpec((None, 1, d), lambda i, be: (be[i], 0, 0))],
        out_specs=pl.BlockSpec((row_block, d), lambda i, be: (i, 0)),
        scratch_shapes=[pltpu.VMEM((2, row_block, d), F32), pltpu.VMEM((row_block, f), BF16),
                        pltpu.SemaphoreType.DMA((2,))])
    return pl.pallas_call(
        functools.partial(_expert_kernel, row_block=row_block),
        grid_spec=grid_spec,
        out_shape=jax.ShapeDtypeStruct((n_rows, d), F32),
        compiler_params=_cparams(("arbitrary",)),
        name="expert_ffn",
    )(block_e, tok3, tok3, h2, w_glu, w_lin, b_glu.reshape(n_exp, 1, f), b_lin.reshape(n_exp, 1, f),
      w_down, b_down.reshape(n_exp, 1, d))


def _combine_kernel(dest_cur_ref, dest_next_ref, ys_ref, w_ref, x_ref, gate_ref, g_ref, b_ref, sc_ref, sh_ref,
                    x2_ref, h_ref, buf, sem, *, tc, alpha):
    i = pl.program_id(0)
    n_tiles = pl.num_programs(0)
    slot = i % 2

    def start_gather(dest_ref, dst_slot):
        for tok in range(tc):
            for k in range(TOP_K):
                pltpu.make_async_copy(ys_ref.at[pl.ds(dest_ref[0, tok * TOP_K + k], 1), :],
                                      buf.at[dst_slot, k, pl.ds(tok, 1), :], sem.at[dst_slot]).start(priority=k % 2)

    def tile_wait(dst_slot):
        for k in range(TOP_K):
            pltpu.make_async_copy(ys_ref.at[pl.ds(0, tc), :], buf.at[dst_slot, k], sem.at[dst_slot]).wait()

    @pl.when(i == 0)
    def _():
        start_gather(dest_cur_ref, 0)

    tile_wait(slot)
    start_gather(dest_next_ref, 1 - slot)
    w = w_ref[...]
    y = w[:, 0:1] * buf[slot, 0]
    for k in range(1, TOP_K):
        y = y + w[:, k:k + 1] * buf[slot, k]
    x2 = _ln(alpha * x_ref[...] + gate_ref[...] * y) * g_ref[...] + b_ref[...]
    x2_ref[...] = x2
    h_ref[...] = (_ln(x2) * (1.0 + sc_ref[...]) + sh_ref[...]).astype(h_ref.dtype)

    @pl.when(i == n_tiles - 1)
    def _():
        tile_wait(1 - slot)


def moe_combine(ys, dest, top_w, x1, mod_l, mod_next, ln_g, ln_b, seq, alpha):
    t, d = x1.shape
    tc = min(128, seq)
    bpb = seq // tc
    row = pl.BlockSpec((tc, d), lambda i: (i, 0))
    vec = pl.BlockSpec((1, d), lambda i: (0, 0))
    n_tiles = t // tc
    dest3 = dest.reshape(n_tiles, 1, tc * TOP_K)
    return pl.pallas_call(
        functools.partial(_combine_kernel, tc=tc, alpha=alpha),
        grid=(n_tiles,),
        in_specs=[pl.BlockSpec((None, 1, tc * TOP_K), lambda i: (i, 0, 0), memory_space=pltpu.SMEM),
                  pl.BlockSpec((None, 1, tc * TOP_K), lambda i: (jnp.minimum(i + 1, n_tiles - 1), 0, 0),
                               memory_space=pltpu.SMEM),
                  pl.BlockSpec(memory_space=pl.ANY),
                  pl.BlockSpec((tc, LANES), lambda i: (i, 0)),
                  row, _mod_spec(5, d, bpb), vec, vec, _mod_spec(1, d, bpb), _mod_spec(0, d, bpb)],
        out_specs=[row, row],
        out_shape=[jax.ShapeDtypeStruct((t, d), F32), jax.ShapeDtypeStruct((t, d), BF16)],
        scratch_shapes=[pltpu.VMEM((2, TOP_K, tc, d), F32), pltpu.SemaphoreType.DMA((2,))],
        compiler_params=_cparams(("arbitrary",), 32),
        name="moe_combine",
    )(dest3, dest3, ys, top_w, x1, mod_l, ln_g.reshape(1, d), ln_b.reshape(1, d), mod_next, mod_next)


def moe_ffn(h2, logits, w_glu, w_lin, up_offset, b_glu, b_lin, w_down, b_down):
    t, d = h2.shape
    n_exp = w_down.shape[0]
    row_block = 256
    idx_rank, top_w, counts = route(logits, n_exp)
    top_idx = idx_rank[:, :TOP_K]
    rank = idx_rank[:, TOP_K:2 * TOP_K]
    counts = counts[0, :n_exp].astype(jnp.int32)
    padded = (counts + row_block - 1) // row_block * row_block
    padded_end = jnp.cumsum(padded)
    padded_start = padded_end - padded
    dest = padded_start[top_idx] + rank
    n_rows = t * TOP_K + n_exp * row_block
    tok_ids = jnp.broadcast_to(jnp.arange(t, dtype=jnp.int32)[:, None], (t, TOP_K))
    row_tok = jnp.zeros((n_rows,), jnp.int32).at[dest.reshape(-1)].set(tok_ids.reshape(-1), unique_indices=True)
    n_blocks = n_rows // row_block
    block_start = jnp.arange(n_blocks, dtype=jnp.int32) * row_block
    block_e = jnp.minimum(jnp.sum(padded_end[None, :] <= block_start[:, None], axis=1), n_exp - 1).astype(jnp.int32)
    ys = expert_ffn(h2, row_tok, block_e, w_glu, w_lin, up_offset, b_glu, b_lin, w_down, b_down, row_block)
    return ys, dest.astype(jnp.int32), top_w


def kernel(x, c, w_ada, b_ada, w_in, b_forget, w_gate, b_gate, w_proj_sb, w_proj_dil, w_proj_fox, w_out,
           ln1_g, ln1_b, w_router, b_router, w_up, b_up, w_down, b_down, ln2_g, ln2_b):
    batch, seq, d = x.shape
    depth = w_ada.shape[0]
    t = batch * seq
    alpha = (2.0 * depth) ** 0.25
    rope_tabs = rope_pair_tables(seq)
    mod = adaln_mod(c, w_ada, b_ada)
    x2d = x.reshape(t, d)
    h = ln_mod(x2d, mod[0], seq, comp_shift=0, comp_scale=1)
    n_exp = w_up.shape[1]
    w_glu, w_lin = prepare_up_weights(w_up)
    for l in range(depth):
        w_in_l = w_in[l]
        qkv = in_projection(h, w_in_l[:, :N_QKV].astype(BF16), rope_tabs, seq)
        q_feat, k_feat = forget_features(h, w_in_l[:, N_QKV:], b_forget[l], batch, seq)
        o_sb = stick_breaking_attention(qkv, batch, seq)
        o_dl = dilated_window_attention(qkv, batch, seq)
        o_fx = forgetting_attention(qkv, q_feat, k_feat, batch, seq)
        merged = gated_merge(h, w_gate[l].astype(BF16), b_gate[l], (o_sb, o_dl, o_fx),
                             (w_proj_sb[l].astype(BF16), w_proj_dil[l].astype(BF16),
                              w_proj_fox[l].astype(BF16)))
        x1, h2, logits = out_projection(merged, w_out[l].astype(BF16), x2d, mod[l], ln1_g[l], ln1_b[l],
                                        w_router[l], b_router[l], seq, alpha)
        ys, dest, top_w = moe_ffn(h2, logits, w_glu, w_lin, l * n_exp,
                                  b_up[l][:, 0::2], b_up[l][:, 1::2], w_down[l].astype(BF16), b_down[l])
        mod_next = mod[min(l + 1, depth - 1)]
        x2d, h = moe_combine(ys, dest, top_w, x1, mod[l], mod_next, ln2_g[l], ln2_b[l], seq, alpha)
    return x2d.reshape(batch, seq, d)
```

```python
import functools

import numpy as np
import jax
import jax.numpy as jnp
from jax import lax
from jax.experimental import pallas as pl
from jax.experimental.pallas import tpu as pltpu

F32 = jnp.float32
BF16 = jnp.bfloat16

HEAD_DIM = 64
H_SB = 8
DIL_CONFIGS = ((128, 1), (512, 4), (2048, 16))
H_DIL_PER_GROUP = 4
H_FOX = 12
ROPE_THETA = 500000.0
ROPE_DIMS = HEAD_DIM // 4
TOP_K = 4
SWIGLU_ALPHA = 1.702
SWIGLU_LIMIT = 7.0
LN_EPS = 1e-5
QK_SCALE = HEAD_DIM ** -0.5

LANES = 128
SUBLANES = 8
VMEM_BUDGET_MB = 56

N_DIL_GROUPS = len(DIL_CONFIGS)
W_SB = H_SB * HEAD_DIM
W_DIL = N_DIL_GROUPS * H_DIL_PER_GROUP * HEAD_DIM
W_DIL_OUT = H_DIL_PER_GROUP * HEAD_DIM
W_FOX = H_FOX * HEAD_DIM
N_QKV = 3 * (W_SB + W_DIL + W_FOX)
COL_SB = (0, W_SB // LANES, 2 * W_SB // LANES)
_DL0 = 3 * W_SB // LANES
COL_DL = (_DL0, _DL0 + W_DIL // LANES, _DL0 + 2 * W_DIL // LANES)
_FX0 = _DL0 + 3 * W_DIL // LANES
COL_FX = (_FX0, _FX0 + W_FOX // LANES, _FX0 + 2 * W_FOX // LANES)
BAND = 128
NEG_BIG = -1e30
EXP_UNDERFLOW = -104.0


def _cparams(semantics, vmem_mb=VMEM_BUDGET_MB):
    return pltpu.CompilerParams(dimension_semantics=semantics, vmem_limit_bytes=vmem_mb * 2 ** 20)


def _dot(a, b):
    return jnp.dot(a, b, preferred_element_type=F32)


def _dot_t(a, b):
    return lax.dot_general(a, b, (((1,), (1,)), ((), ())), preferred_element_type=F32)


def _ln(x):
    mu = jnp.mean(x, axis=-1, keepdims=True)
    xc = x - mu
    var = jnp.mean(xc * xc, axis=-1, keepdims=True)
    return xc * lax.rsqrt(var + LN_EPS)


def _softplus_neg_abs(z):
    return jnp.log1p(jnp.exp(-jnp.abs(z)))


def _split3(x):
    hi = x.astype(BF16)
    r1 = x - hi.astype(F32)
    mid = r1.astype(BF16)
    lo = (r1 - mid.astype(F32)).astype(BF16)
    return hi, mid, lo


def _mod_kernel(c_ref, w_ref, b_ref, o_ref):
    c = c_ref[...]
    act = (c * jax.nn.sigmoid(c)).astype(BF16)
    o_ref[...] = _dot(act, w_ref[...].astype(BF16)) + b_ref[...]


def adaln_mod(c, w_ada, b_ada):
    n_layers, d, n = w_ada.shape
    b = c.shape[0]
    assert b <= SUBLANES
    tn = 512
    c_pad = jnp.zeros((SUBLANES, d), F32).at[:b].set(c)
    out = pl.pallas_call(
        _mod_kernel,
        grid=(n_layers, n // tn),
        in_specs=[pl.BlockSpec((SUBLANES, d), lambda l, j: (0, 0)),
                  pl.BlockSpec((None, d, tn), lambda l, j: (l, 0, j)),
                  pl.BlockSpec((None, 1, tn), lambda l, j: (l, 0, j))],
        out_specs=pl.BlockSpec((None, SUBLANES, tn), lambda l, j: (l, 0, j)),
        out_shape=jax.ShapeDtypeStruct((n_layers, SUBLANES, n), F32),
        compiler_params=_cparams(("arbitrary", "arbitrary"), 32),
        name="adaln_mod",
    )(c_pad, w_ada, b_ada.reshape(n_layers, 1, n))
    return out[:, :b].reshape(n_layers, b, 6, d).transpose(0, 2, 1, 3)[:, :, :, None, :]


def _mod_spec(comp, d, rows_per_batch_blocks):
    return pl.BlockSpec((None, None, 1, d), lambda i, *_: (comp, i // rows_per_batch_blocks, 0, 0))


def _ln_mod_kernel(x_ref, sc_ref, sh_ref, h_ref):
    h_ref[...] = (_ln(x_ref[...]) * (1.0 + sc_ref[...]) + sh_ref[...]).astype(h_ref.dtype)


def ln_mod(x2d, mod_l, seq, comp_shift, comp_scale):
    t, d = x2d.shape
    ts = min(512, seq)
    bpb = seq // ts
    return pl.pallas_call(
        _ln_mod_kernel,
        grid=(t // ts,),
        in_specs=[pl.BlockSpec((ts, d), lambda i: (i, 0)),
                  _mod_spec(comp_scale, d, bpb), _mod_spec(comp_shift, d, bpb)],
        out_specs=pl.BlockSpec((ts, d), lambda i: (i, 0)),
        out_shape=jax.ShapeDtypeStruct((t, d), BF16),
        compiler_params=_cparams(("arbitrary",), 32),
        name="ln_mod",
    )(x2d, mod_l, mod_l)


def _inproj_kernel(h_ref, w_ref, cos_ref, s1_ref, s2_ref, o_ref, *, rope_lo, rope_hi, tn):
    j = pl.program_id(1)
    acc = _dot(h_ref[...], w_ref[...])
    is_rope = jnp.logical_and(j >= rope_lo, j < rope_hi)

    @pl.when(is_rope)
    def _():
        c, s1, s2 = cos_ref[...], s1_ref[...], s2_ref[...]
        for blk in range(tn // LANES):
            a = acc[:, blk * LANES:(blk + 1) * LANES]
            r = a * c + pltpu.roll(a, LANES - ROPE_DIMS // 2, 1) * s1 + pltpu.roll(a, ROPE_DIMS // 2, 1) * s2
            o_ref[:, blk * LANES:(blk + 1) * LANES] = r.astype(o_ref.dtype)

    @pl.when(jnp.logical_not(is_rope))
    def _():
        o_ref[...] = acc.astype(o_ref.dtype)


def rope_pair_tables(seq):
    pos = jnp.arange(seq, dtype=F32)
    inv = ROPE_THETA ** (-jnp.arange(0, ROPE_DIMS, 2, dtype=F32) / ROPE_DIMS)
    ang = pos[:, None] * inv[None, :]
    cos, sin = jnp.cos(ang), jnp.sin(ang)
    half = ROPE_DIMS // 2
    head_c = jnp.concatenate([cos, cos, jnp.ones((seq, HEAD_DIM - ROPE_DIMS), F32)], axis=1)
    head_s1 = jnp.concatenate([-sin, jnp.zeros((seq, HEAD_DIM - half), F32)], axis=1)
    head_s2 = jnp.concatenate([jnp.zeros((seq, half), F32), sin,
                               jnp.zeros((seq, HEAD_DIM - ROPE_DIMS), F32)], axis=1)
    two = lambda a: jnp.concatenate([a, a], axis=1)
    return two(head_c), two(head_s1), two(head_s2)


def in_projection(h, w_qkv, rope_tabs, seq):
    t, d = h.shape
    n = w_qkv.shape[1]
    tm = min(1024, seq)
    tn = 1536
    assert seq % tm == 0 and n % tn == 0
    rope_lo, rope_hi = COL_DL[0] * LANES, COL_DL[2] * LANES
    assert rope_lo % tn == 0 and rope_hi % tn == 0
    spb = seq // tm
    tab_spec = pl.BlockSpec((tm, LANES), lambda i, j: (i % spb, 0))
    return pl.pallas_call(
        functools.partial(_inproj_kernel, rope_lo=rope_lo // tn, rope_hi=rope_hi // tn, tn=tn),
        grid=(t // tm, n // tn),
        in_specs=[pl.BlockSpec((tm, d), lambda i, j: (i, 0)),
                  pl.BlockSpec((d, tn), lambda i, j: (0, j)),
                  tab_spec, tab_spec, tab_spec],
        out_specs=pl.BlockSpec((tm, tn), lambda i, j: (i, j)),
        out_shape=jax.ShapeDtypeStruct((t, n), BF16),
        compiler_params=_cparams(("arbitrary", "arbitrary"), 48),
        name="in_projection",
    )(h, w_qkv, *rope_tabs)


FEAT_PER_HEAD = 6


def _forget_feature_maps():
    width = (H_FOX // 2) * LANES
    pq = np.zeros((3 * LANES, width), np.float32)
    pk = np.zeros((3 * LANES, width), np.float32)
    cq = np.zeros((1, width), np.float32)
    ck = np.zeros((1, width), np.float32)
    for head in range(H_FOX):
        base = (head // 2) * LANES + (head % 2) * FEAT_PER_HEAD
        for piece in range(3):
            pq[piece * LANES + head, base + piece] = 1.0
            pk[piece * LANES + head, base + 3 + piece] = -1.0
        cq[0, base + 3:base + 6] = 1.0
        ck[0, base:base + 3] = 1.0
    return pq, pk, cq, ck


def _forget_kernel(h_ref, w_ref, b_ref, tri_ref, pq_ref, pk_ref, cq_ref, ck_ref, qf_ref, kf_ref, carry_ref):
    @pl.when(pl.program_id(1) == 0)
    def _():
        carry_ref[...] = jnp.zeros_like(carry_ref)

    f = _dot(h_ref[...], w_ref[...]) + b_ref[...]
    log_f = jnp.minimum(f, 0.0) - _softplus_neg_abs(f)
    tri = tri_ref[...]
    hi, mid, lo = _split3(log_f)
    cum = _dot(tri, hi) + _dot(tri, mid) + _dot(tri, lo) + carry_ref[0:1, :]
    carry_ref[...] = jnp.broadcast_to(cum[-1:, :], carry_ref.shape)
    pieces = jnp.concatenate(_split3(cum), axis=1)
    qf_ref[...] = (_dot(pieces, pq_ref[...]) + cq_ref[...]).astype(qf_ref.dtype)
    kf_ref[...] = (_dot(pieces, pk_ref[...]) + ck_ref[...]).astype(kf_ref.dtype)


def forget_features(h, w_f, b_f, batch, seq):
    t, d = h.shape
    ts = min(256, seq)
    nsb = seq // ts
    w_pad = jnp.zeros((d, LANES), BF16).at[:, :H_FOX].set(w_f.astype(BF16))
    b_pad = jnp.zeros((1, LANES), F32).at[0, :H_FOX].set(b_f)
    tri = (np.arange(ts)[:, None] >= np.arange(ts)[None, :]).astype(np.float32)
    pq, pk, cq, ck = _forget_feature_maps()
    width = pq.shape[1]
    const = lambda shape: pl.BlockSpec(shape, lambda b, s: (0, 0))
    out_spec = pl.BlockSpec((ts, width), lambda b, s: (b * nsb + s, 0))
    out_shape = jax.ShapeDtypeStruct((t, width), BF16)
    return pl.pallas_call(
        _forget_kernel,
        grid=(batch, nsb),
        in_specs=[pl.BlockSpec((ts, d), lambda b, s: (b * nsb + s, 0)),
                  const((d, LANES)), const((1, LANES)), const((ts, ts)),
                  const(pq.shape), const(pk.shape), const(cq.shape), const(ck.shape)],
        out_specs=[out_spec, out_spec],
        out_shape=[out_shape, out_shape],
        scratch_shapes=[pltpu.VMEM((SUBLANES, LANES), F32)],
        compiler_params=_cparams(("arbitrary", "arbitrary"), 32),
        name="forget_features",
    )(h, w_pad, b_pad, jnp.asarray(tri, BF16), jnp.asarray(pq, BF16), jnp.asarray(pk, BF16),
      jnp.asarray(cq), jnp.asarray(ck))


def _pair_masks(tq):
    lane = lax.broadcasted_iota(jnp.int32, (tq, LANES), 1)
    return lane < HEAD_DIM


KEY_SUB = 128
QRY_SUB = 128
UP_CHUNK = 256
SUFFIX_SUB = 256


def _diag_tile_kind(ks, qs, key_sub, qry_sub, strict, key_off):
    if key_off is None:
        return 'full'
    k_lo, k_hi = key_off + ks * key_sub, key_off + ks * key_sub + key_sub - 1
    q_lo, q_hi = qs * qry_sub, qs * qry_sub + qry_sub - 1
    if strict:
        if k_lo >= q_hi:
            return 'dead'
        return 'full' if k_hi < q_lo else 'partial'
    if k_lo > q_hi:
        return 'dead'
    return 'full' if k_hi <= q_lo else 'partial'


def _tile_allowed(ks, qs, key_sub, qry_sub, strict, key_off):
    kpos = key_off + ks * key_sub + lax.broadcasted_iota(jnp.int32, (key_sub, qry_sub), 0)
    qpos = qs * qry_sub + lax.broadcasted_iota(jnp.int32, (key_sub, qry_sub), 1)
    return kpos < qpos if strict else kpos <= qpos


def _pair_transpose_in(qkv3, col, width):
    return lax.slice_in_dim(qkv3, col * LANES, col * LANES + width, axis=2).transpose(0, 2, 1)


def _sb_kernel(q_ref, k_ref, vt_ref, later_ref, o_ref,
               qx_s, acc_s, run_s, first_s, z_s, hi_s, lo_s, btw_s, w_s, *, tq, tk, key_sub, qry_sub):
    q_blk = pl.program_id(2)
    head_a = _pair_masks(tq)
    q = q_ref[...] * QK_SCALE
    qx_s[0] = jnp.where(head_a, q, 0).astype(BF16)
    qx_s[1] = jnp.where(head_a, 0, q).astype(BF16)
    acc_s[...] = jnp.zeros_like(acc_s)
    run_s[...] = jnp.zeros_like(run_s)

    def step(key_off, kv_blk):
        k_ref_blk = k_ref.at[pl.ds(pl.multiple_of(kv_blk * tk, tk), tk), :]
        vt_blk = vt_ref.at[kv_blk]
        _sb_block(key_off, k_ref_blk, vt_blk)

    def _sb_block(key_off, k_ref, vt_ref):
        suf = later_ref.shape[0]
        n_suf = tk // suf
        per_suf = suf // key_sub
        tiles = [(ks, qs) for ks in range(tk // key_sub) for qs in range(tq // qry_sub)]
        kind_of = lambda ks, qs: _diag_tile_kind(ks, qs, key_sub, qry_sub, True, key_off)
        k = k_ref[...]
        for head in range(2):
            z_s[head] = _dot_t(k, qx_s[head])
        for head in range(2):
            for ks, qs in tiles:
                kl = slice(ks * key_sub, (ks + 1) * key_sub)
                ql = slice(qs * qry_sub, (qs + 1) * qry_sub)
                first_row = head * n_suf + ks // per_suf
                kind = kind_of(ks, qs)
                if kind == 'dead':
                    hi_s[head, kl, ql] = jnp.zeros((key_sub, qry_sub), BF16)
                    lo_s[head, kl, ql] = jnp.zeros((key_sub, qry_sub), BF16)
                    if ks % per_suf == 0:
                        first_s[first_row:first_row + 1, ql] = jnp.zeros((1, qry_sub), F32)
                    continue
                z = z_s[head, kl, ql]
                log_stop = jnp.minimum(z, 0.0) - jnp.log(1.0 + jnp.exp(-jnp.abs(z)))
                log_cont = log_stop - z
                if kind == 'partial':
                    log_cont = jnp.where(_tile_allowed(ks, qs, key_sub, qry_sub, True, key_off), log_cont, 0.0)
                z_s[head, kl, ql] = log_stop
                hi = log_cont.astype(BF16)
                hi_s[head, kl, ql] = hi
                lo_s[head, kl, ql] = (log_cont - hi.astype(F32)).astype(BF16)
                if ks % per_suf == 0:
                    first_s[first_row:first_row + 1, ql] = log_cont[0:1, :]
        later = later_ref[...]
        for head in range(2):
            for blk in range(n_suf):
                bl = slice(blk * suf, (blk + 1) * suf)
                btw_s[head, bl, :] = _dot(later, hi_s[head, bl, :]) + _dot(later, lo_s[head, bl, :])

        def block_sum(head, blk, ql):
            row = head * n_suf + blk
            return btw_s[head, blk * suf:blk * suf + 1, ql] + first_s[row:row + 1, ql]

        for head in range(2):
            for ks, qs in tiles:
                kl = slice(ks * key_sub, (ks + 1) * key_sub)
                ql = slice(qs * qry_sub, (qs + 1) * qry_sub)
                kind = kind_of(ks, qs)
                if kind == 'dead':
                    w_s[head, kl, ql] = jnp.zeros((key_sub, qry_sub), BF16)
                    continue
                after = run_s[head:head + 1, ql]
                for blk in range(ks // per_suf + 1, n_suf):
                    after = after + block_sum(head, blk, ql)
                w = jnp.exp(z_s[head, kl, ql] + btw_s[head, kl, ql] + after)
                if kind == 'partial':
                    w = jnp.where(_tile_allowed(ks, qs, key_sub, qry_sub, True, key_off), w, 0.0)
                w_s[head, kl, ql] = w.astype(BF16)
        for head in range(2):
            rows = slice(head * HEAD_DIM, (head + 1) * HEAD_DIM)
            acc_s[rows, :] += _dot(vt_ref[rows, :], w_s[head])
            total = run_s[head:head + 1, :]
            for blk in range(n_suf):
                total = total + block_sum(head, blk, slice(None))
            run_s[head:head + 1, :] = total

    def all_weights_zero():
        return jnp.max(run_s[0:2, :]) < EXP_UNDERFLOW

    step(0, q_blk)

    def more(carry):
        kv_blk, dead = carry
        return jnp.logical_and(kv_blk >= 0, jnp.logical_not(dead))

    def visit(carry):
        kv_blk, _ = carry
        step(None, kv_blk)
        return kv_blk - 1, all_weights_zero()

    lax.while_loop(more, visit, (q_blk - 1, all_weights_zero()))
    o_ref[...] = acc_s[...].astype(o_ref.dtype)


def stick_breaking_attention(qkv, batch, seq):
    tq = tk = min(512, seq)
    key_sub, qry_sub = min(KEY_SUB, tk), min(QRY_SUB, tq)
    n_pairs = W_SB // LANES
    n_kv = seq // tk
    qkv3 = qkv.reshape(batch, seq, qkv.shape[-1])
    qc, kc, vc = COL_SB
    v_t = _pair_transpose_in(qkv3, vc, W_SB).reshape(batch, n_pairs, LANES, n_kv, tk).transpose(0, 1, 3, 2, 4)
    suf = min(SUFFIX_SUB, tk)
    assert 2 * (tk // suf) <= SUBLANES and suf % key_sub == 0
    later = jnp.asarray((np.arange(suf)[None, :] > np.arange(suf)[:, None]).astype(np.float32), BF16)
    block_f32 = pltpu.VMEM((2, tk, tq), F32)
    block_bf16 = pltpu.VMEM((2, tk, tq), BF16)
    row_stats = pltpu.VMEM((SUBLANES, tq), F32)
    out_t = pl.pallas_call(
        functools.partial(_sb_kernel, tq=tq, tk=tk, key_sub=key_sub, qry_sub=qry_sub),
        grid=(batch, n_pairs, seq // tq),
        in_specs=[pl.BlockSpec((None, tq, LANES), lambda b, p, i: (b, i, qc + p)),
                  pl.BlockSpec((None, seq, LANES), lambda b, p, i: (b, 0, kc + p)),
                  pl.BlockSpec((None, None, n_kv, LANES, tk), lambda b, p, i: (b, p, 0, 0, 0)),
                  pl.BlockSpec((suf, suf), lambda b, p, i: (0, 0))],
        out_specs=pl.BlockSpec((None, LANES, tq), lambda b, p, i: (b, p, i)),
        scratch_shapes=[pltpu.VMEM((2, tq, LANES), BF16), pltpu.VMEM((LANES, tq), F32),
                        row_stats, row_stats, block_f32, block_bf16, block_bf16, block_f32, block_bf16],
        out_shape=jax.ShapeDtypeStruct((batch, W_SB, seq), BF16),
        compiler_params=_cparams(("arbitrary", "arbitrary", "arbitrary"), 32),
        name="stick_breaking_attention",
    )(qkv3, qkv3, v_t, later)
    return out_t.transpose(0, 2, 1).reshape(batch * seq, W_SB)


def _fox_kernel(q_ref, qf_ref, k_ref, kf_ref, vt_ref, o_ref,
                qx_s, acc_s, m_s, l_s, alpha_s, z_s, p_s, *, tq, tk, key_sub, qry_sub):
    q_blk = pl.program_id(2)
    lane = lax.broadcasted_iota(jnp.int32, (tq, LANES), 1)
    q = q_ref[...] * QK_SCALE
    qf = qf_ref[...]
    for head in range(2):
        in_head = (lane >= head * HEAD_DIM) & (lane < (head + 1) * HEAD_DIM)
        in_feat = (lane >= head * FEAT_PER_HEAD) & (lane < (head + 1) * FEAT_PER_HEAD)
        qx_s[head] = jnp.concatenate([jnp.where(in_head, q, 0).astype(BF16),
                                      jnp.where(in_feat, qf, 0).astype(BF16)], axis=1)
    acc_s[...] = jnp.zeros_like(acc_s)
    l_s[...] = jnp.zeros_like(l_s)
    m_s[...] = jnp.full_like(m_s, NEG_BIG)

    def step(key_off, kv_blk):
        rows = pl.ds(pl.multiple_of(kv_blk * tk, tk), tk)
        _fox_block(key_off, k_ref.at[rows, :], kf_ref.at[rows, :], vt_ref.at[kv_blk])

    def _fox_block(key_off, k_ref, kf_ref, vt_ref):
        kind_of = lambda ks, qs: _diag_tile_kind(ks, qs, key_sub, qry_sub, False, key_off)
        kx = jnp.concatenate([k_ref[...], kf_ref[...]], axis=1)
        for head in range(2):
            z_s[head] = _dot_t(kx, qx_s[head])
        for head in range(2):
            for qs in range(tq // qry_sub):
                ql = slice(qs * qry_sub, (qs + 1) * qry_sub)

                def score(ks):
                    z = z_s[head, ks * key_sub:(ks + 1) * key_sub, ql]
                    if kind_of(ks, qs) == 'partial':
                        z = jnp.where(_tile_allowed(ks, qs, key_sub, qry_sub, False, key_off), z, NEG_BIG)
                    return z

                live = [ks for ks in range(tk // key_sub) if kind_of(ks, qs) != 'dead']
                m_old = m_s[head:head + 1, ql]
                m_new = m_old
                for ks in live:
                    m_new = jnp.maximum(m_new, jnp.max(score(ks), axis=0, keepdims=True))
                alpha = jnp.exp(m_old - m_new)
                total = alpha * l_s[head:head + 1, ql]
                for ks in range(tk // key_sub):
                    kl = slice(ks * key_sub, (ks + 1) * key_sub)
                    if ks not in live:
                        p_s[head, kl, ql] = jnp.zeros((key_sub, qry_sub), BF16)
                        continue
                    prob = jnp.exp(score(ks) - m_new)
                    total = total + jnp.sum(prob, axis=0, keepdims=True)
                    p_s[head, kl, ql] = prob.astype(BF16)
                m_s[head:head + 1, ql] = m_new
                l_s[head:head + 1, ql] = total
                alpha_s[head:head + 1, ql] = alpha
        for head in range(2):
            rows = slice(head * HEAD_DIM, (head + 1) * HEAD_DIM)
            acc_s[rows, :] = alpha_s[head:head + 1, :] * acc_s[rows, :] + _dot(vt_ref[rows, :], p_s[head])

    per_q = tq // tk
    first_diag = q_blk * per_q

    def visit(kv_blk, carry):
        step(None, kv_blk)
        return carry

    lax.fori_loop(0, first_diag, visit, 0)
    for variant in range(per_q):
        step(variant * tk, first_diag + variant)
    for head in range(2):
        rows = slice(head * HEAD_DIM, (head + 1) * HEAD_DIM)
        o_ref[rows, :] = (acc_s[rows, :] / l_s[head:head + 1, :]).astype(o_ref.dtype)


def forgetting_attention(qkv, q_feat, k_feat, batch, seq):
    tk = min(512, seq)
    tq = min(1024, seq)
    key_sub, qry_sub = min(KEY_SUB, tk), min(QRY_SUB, tq)
    assert tq % tk == 0
    n_pairs = W_FOX // LANES
    n_kv = seq // tk
    qkv3 = qkv.reshape(batch, seq, qkv.shape[-1])
    qf3 = q_feat.reshape(batch, seq, n_pairs * LANES)
    kf3 = k_feat.reshape(batch, seq, n_pairs * LANES)
    qc, kc, vc = COL_FX
    v_t = _pair_transpose_in(qkv3, vc, W_FOX).reshape(batch, n_pairs, LANES, n_kv, tk).transpose(0, 1, 3, 2, 4)
    out_t = pl.pallas_call(
        functools.partial(_fox_kernel, tq=tq, tk=tk, key_sub=key_sub, qry_sub=qry_sub),
        grid=(batch, n_pairs, seq // tq),
        in_specs=[pl.BlockSpec((None, tq, LANES), lambda b, p, i: (b, i, qc + p)),
                  pl.BlockSpec((None, tq, LANES), lambda b, p, i: (b, i, p)),
                  pl.BlockSpec((None, seq, LANES), lambda b, p, i: (b, 0, kc + p)),
                  pl.BlockSpec((None, seq, LANES), lambda b, p, i: (b, 0, p)),
                  pl.BlockSpec((None, None, n_kv, LANES, tk), lambda b, p, i: (b, p, 0, 0, 0))],
        out_specs=pl.BlockSpec((None, LANES, tq), lambda b, p, i: (b, p, i)),
        scratch_shapes=[pltpu.VMEM((2, tq, 2 * LANES), BF16), pltpu.VMEM((LANES, tq), F32),
                        pltpu.VMEM((SUBLANES, tq), F32), pltpu.VMEM((SUBLANES, tq), F32),
                        pltpu.VMEM((SUBLANES, tq), F32), pltpu.VMEM((2, tk, tq), F32),
                        pltpu.VMEM((2, tk, tq), BF16)],
        out_shape=jax.ShapeDtypeStruct((batch, W_FOX, seq), BF16),
        compiler_params=_cparams(("arbitrary", "arbitrary", "arbitrary"), 40),
        name="forgetting_attention",
    )(qkv3, qf3, qkv3, kf3, v_t)
    return out_t.transpose(0, 2, 1).reshape(batch * seq, W_FOX)


def _band_kernel(q_ref, kc_ref, vc_ref, kp_ref, vp_ref, o_ref, lse_ref, *, tq):
    i = pl.program_id(1)
    head_a = _pair_masks(tq)
    diff = lax.broadcasted_iota(jnp.int32, (tq, tq), 0) - lax.broadcasted_iota(jnp.int32, (tq, tq), 1)
    mask_c = jnp.logical_and(diff >= 0, diff <= BAND)
    rp = lax.broadcasted_iota(jnp.int32, (tq, BAND), 0)
    cp = lax.broadcasted_iota(jnp.int32, (tq, BAND), 1)
    mask_p = jnp.logical_and(cp >= rp, i > 0)
    for pair in range(q_ref.shape[1] // LANES):
        pl_ = slice(pair * LANES, (pair + 1) * LANES)
        q = q_ref[:, pl_] * QK_SCALE
        kc, vc, kp, vp = kc_ref[:, pl_], vc_ref[:, pl_], kp_ref[:, pl_], vp_ref[:, pl_]
        outs, lses = [], []
        for sel in (head_a, jnp.logical_not(head_a)):
            qh = jnp.where(sel, q, 0).astype(BF16)
            sc = jnp.where(mask_c, _dot_t(qh, kc), NEG_BIG)
            sp = jnp.where(mask_p, _dot_t(qh, kp), NEG_BIG)
            m = jnp.maximum(jnp.max(sc, axis=1, keepdims=True), jnp.max(sp, axis=1, keepdims=True))
            ec = jnp.exp(sc - m)
            ep = jnp.exp(sp - m)
            den = jnp.sum(ec, axis=1, keepdims=True) + jnp.sum(ep, axis=1, keepdims=True)
            outs.append((_dot(ec.astype(BF16), vc) + _dot(ep.astype(BF16), vp)) / den)
            lses.append(m + jnp.log(den))
        o_ref[:, pl_] = jnp.where(head_a, outs[0], outs[1])
        lse_ref[:, pl_] = jnp.where(head_a, lses[0], lses[1])


def _band_attention(q, k, v):
    nb, u, w = q.shape
    tq = min(256, u)
    assert u % tq == 0 and tq % BAND == 0
    sub = tq // BAND
    cur = pl.BlockSpec((None, tq, w), lambda n, i: (n, i, 0))
    prev = pl.BlockSpec((None, BAND, w), lambda n, i: (n, jnp.maximum(i * sub - 1, 0), 0))
    shp = jax.ShapeDtypeStruct((nb, u, w), F32)
    return pl.pallas_call(
        functools.partial(_band_kernel, tq=tq),
        grid=(nb, u // tq),
        in_specs=[cur, cur, cur, prev, prev],
        out_specs=[cur, cur],
        out_shape=[shp, shp],
        compiler_params=_cparams(("arbitrary", "arbitrary"), 32),
        name="band_attention",
    )(q, k, v, k, v)


def _dil_mix_kernel(o0, o1, o2, l0, l1, l2, out_ref):
    a, b, c = l0[...], l1[...], l2[...]
    m = jnp.maximum(jnp.maximum(a, b), c)
    ea, eb, ec = jnp.exp(a - m), jnp.exp(b - m), jnp.exp(c - m)
    tot = ea + eb + ec
    out_ref[...] = ((o0[...] * ea + o1[...] * eb + o2[...] * ec) / tot).astype(out_ref.dtype)


def dilated_window_attention(qkv, batch, seq):
    t = batch * seq
    qkv3 = qkv.reshape(batch, seq, qkv.shape[-1])
    gw = W_DIL_OUT
    outs, lses = [], []
    for g, (window, dil) in enumerate(DIL_CONFIGS):
        assert window // dil == BAND and seq % (dil * BAND) == 0
        u = seq // dil

        def split(col):
            a = lax.slice_in_dim(qkv3, col * LANES + g * gw, col * LANES + (g + 1) * gw, axis=2)
            return a.reshape(batch, u, dil, gw).transpose(0, 2, 1, 3).reshape(batch * dil, u, gw)

        o, lse = _band_attention(split(COL_DL[0]), split(COL_DL[1]), split(COL_DL[2]))
        merge = lambda a: a.reshape(batch, dil, u, gw).transpose(0, 2, 1, 3).reshape(t, gw)
        outs.append(merge(o))
        lses.append(merge(lse))
    tm = min(1024, t)
    spec = pl.BlockSpec((tm, gw), lambda i: (i, 0))
    return pl.pallas_call(
        _dil_mix_kernel,
        grid=(t // tm,),
        in_specs=[spec] * 6,
        out_specs=spec,
        out_shape=jax.ShapeDtypeStruct((t, gw), BF16),
        compiler_params=_cparams(("arbitrary",), 32),
        name="dilated_mix",
    )(*outs, *lses)


def _merge_kernel(h_ref, wg0, wg1, wg2, bg0, bg1, bg2, o0, o1, o2, wp0, wp1, wp2, out_ref):
    h = h_ref[...]

    def branch(wg, bg, o, wp):
        return jax.nn.sigmoid(_dot(h, wg[...]) + bg[...]) * _dot(o[...], wp[...])

    merged = branch(wg0, bg0, o0, wp0) + branch(wg1, bg1, o1, wp1) + branch(wg2, bg2, o2, wp2)
    out_ref[...] = merged.astype(out_ref.dtype)


def gated_merge(h, w_gate, b_gate, branch_outs, branch_projs):
    t, d = h.shape
    tm, tn = min(1024, t), min(512, d)
    nj = d // tn
    b_gate2 = b_gate.reshape(1, -1)
    gate_w = [pl.BlockSpec((d, tn), lambda j, i, br=br: (0, br * nj + j)) for br in range(3)]
    gate_b = [pl.BlockSpec((1, tn), lambda j, i, br=br: (0, br * nj + j)) for br in range(3)]
    o_specs = [pl.BlockSpec((tm, o.shape[1]), lambda j, i: (i, 0)) for o in branch_outs]
    p_specs = [pl.BlockSpec((w.shape[0], tn), lambda j, i: (0, j)) for w in branch_projs]
    return pl.pallas_call(
        _merge_kernel,
        grid=(nj, t // tm),
        in_specs=[pl.BlockSpec((tm, d), lambda j, i: (i, 0))] + gate_w + gate_b + o_specs + p_specs,
        out_specs=pl.BlockSpec((tm, tn), lambda j, i: (i, j)),
        out_shape=jax.ShapeDtypeStruct((t, d), BF16),
        compiler_params=_cparams(("arbitrary", "arbitrary"), 48),
        name="gated_merge",
    )(h, w_gate, w_gate, w_gate, b_gate2, b_gate2, b_gate2, *branch_outs, *branch_projs)


def _outproj_kernel(m_ref, w_ref, x_ref, gate_ref, g_ref, b_ref, sc_ref, sh_ref, wrh_ref, wrl_ref, br_ref,
                    x1_ref, h2_ref, lg_ref, *, alpha):
    y = _dot(m_ref[...], w_ref[...])
    x1 = _ln(alpha * x_ref[...] + gate_ref[...] * y) * g_ref[...] + b_ref[...]
    x1_ref[...] = x1
    h2 = _ln(x1) * (1.0 + sc_ref[...]) + sh_ref[...]
    h2_ref[...] = h2
    hi = h2.astype(BF16)
    lo = (h2 - hi.astype(F32)).astype(BF16)
    wrh = wrh_ref[...]
    lg_ref[...] = _dot(hi, wrh) + _dot(lo, wrh) + _dot(hi, wrl_ref[...]) + br_ref[...]


def out_projection(merged, w_out, x2d, mod_l, ln_g, ln_b, w_router, b_router, seq, alpha):
    t, d = x2d.shape
    tm = min(256, seq)
    bpb = seq // tm
    n_exp = w_router.shape[1]
    wr = jnp.zeros((d, LANES), F32).at[:, :n_exp].set(w_router)
    wr_hi = wr.astype(BF16)
    wr_lo = (wr - wr_hi.astype(F32)).astype(BF16)
    br = jnp.zeros((1, LANES), F32).at[0, :n_exp].set(b_router)
    row = pl.BlockSpec((tm, d), lambda i: (i, 0))
    vec = pl.BlockSpec((1, d), lambda i: (0, 0))
    wr_spec = pl.BlockSpec((d, LANES), lambda i: (0, 0))
    return pl.pallas_call(
        functools.partial(_outproj_kernel, alpha=alpha),
        grid=(t // tm,),
        in_specs=[row, pl.BlockSpec((d, d), lambda i: (0, 0)), row, _mod_spec(2, d, bpb), vec, vec,
                  _mod_spec(4, d, bpb), _mod_spec(3, d, bpb), wr_spec, wr_spec,
                  pl.BlockSpec((1, LANES), lambda i: (0, 0))],
        out_specs=[row, row, pl.BlockSpec((tm, LANES), lambda i: (i, 0))],
        out_shape=[jax.ShapeDtypeStruct((t, d), F32), jax.ShapeDtypeStruct((t, d), F32),
                   jax.ShapeDtypeStruct((t, LANES), F32)],
        compiler_params=_cparams(("arbitrary",), 48),
        name="out_projection",
    )(merged, w_out, x2d, mod_l, ln_g.reshape(1, d), ln_b.reshape(1, d), mod_l, mod_l, wr_hi, wr_lo, br)


def _route_kernel(lg_ref, tri_ref, ir_ref, w_ref, cnt_ref, carry_ref, *, n_exp):
    @pl.when(pl.program_id(0) == 0)
    def _():
        carry_ref[...] = jnp.zeros_like(carry_ref)

    tr = lg_ref.shape[0]
    lane = lax.broadcasted_iota(jnp.int32, (tr, LANES), 1)
    logits = jnp.where(lane < n_exp, lg_ref[...], -jnp.inf)
    vals, sels, idxs = [], [], []
    for _ in range(TOP_K):
        m = jnp.max(logits, axis=1, keepdims=True)
        idx = jnp.min(jnp.where(logits == m, lane, LANES), axis=1, keepdims=True)
        sel = lane == idx
        vals.append(m)
        idxs.append(idx)
        sels.append(sel)
        logits = jnp.where(sel, -jnp.inf, logits)
    exps = [jnp.exp(v - vals[0]) for v in vals]
    tot = exps[0] + exps[1] + exps[2] + exps[3]
    onehot = jnp.zeros((tr, LANES), F32)
    for sel in sels:
        onehot = onehot + jnp.where(sel, 1.0, 0.0)
    before = _dot(tri_ref[...], onehot.astype(BF16)) + carry_ref[0:1, :]
    ir = jnp.zeros((tr, LANES), jnp.int32)
    wt = jnp.zeros((tr, LANES), F32)
    for k in range(TOP_K):
        rank = jnp.sum(jnp.where(sels[k], before, 0.0), axis=1, keepdims=True).astype(jnp.int32)
        ir = jnp.where(lane == k, idxs[k], ir)
        ir = jnp.where(lane == TOP_K + k, rank, ir)
        wt = jnp.where(lane == k, exps[k] / tot, wt)
    ir_ref[...] = ir
    w_ref[...] = wt
    total = before[-1:, :] + onehot[-1:, :]
    carry_ref[...] = jnp.broadcast_to(total, carry_ref.shape)
    cnt_ref[...] = jnp.broadcast_to(total, cnt_ref.shape)


def route(logits, n_exp):
    t = logits.shape[0]
    tr = min(512, t)
    tri = jnp.asarray((np.arange(tr)[:, None] > np.arange(tr)[None, :]).astype(np.float32), BF16)
    row = pl.BlockSpec((tr, LANES), lambda i: (i, 0))
    return pl.pallas_call(
        functools.partial(_route_kernel, n_exp=n_exp),
        grid=(t // tr,),
        in_specs=[row, pl.BlockSpec((tr, tr), lambda i: (0, 0))],
        out_specs=[row, row, pl.BlockSpec((SUBLANES, LANES), lambda i: (0, 0))],
        out_shape=[jax.ShapeDtypeStruct((t, LANES), jnp.int32), jax.ShapeDtypeStruct((t, LANES), F32),
                   jax.ShapeDtypeStruct((SUBLANES, LANES), F32)],
        scratch_shapes=[pltpu.VMEM((SUBLANES, LANES), F32)],
        compiler_params=_cparams(("arbitrary",), 32),
        name="route",
    )(logits, tri)


def _prep_up_kernel(w_ref, even_ref, odd_ref, g_ref, l_ref):
    w = w_ref[...].astype(BF16)
    g_ref[...] = _dot(w, even_ref[...]).astype(g_ref.dtype)
    l_ref[...] = _dot(w, odd_ref[...]).astype(l_ref.dtype)


def prepare_up_weights(w_up):
    d, f2 = w_up.shape[-2:]
    w3 = w_up.reshape(-1, d, f2)
    n = w3.shape[0]
    tn = min(512, f2)
    half = tn // 2
    even = np.zeros((tn, half), np.float32)
    odd = np.zeros((tn, half), np.float32)
    even[2 * np.arange(half), np.arange(half)] = 1.0
    odd[2 * np.arange(half) + 1, np.arange(half)] = 1.0
    sel = pl.BlockSpec((tn, half), lambda e, j: (0, 0))
    out_spec = pl.BlockSpec((None, d, half), lambda e, j: (e, 0, j))
    out_shape = jax.ShapeDtypeStruct((n, d, f2 // 2), BF16)
    return pl.pallas_call(
        _prep_up_kernel,
        grid=(n, f2 // tn),
        in_specs=[pl.BlockSpec((None, d, tn), lambda e, j: (e, 0, j)), sel, sel],
        out_specs=[out_spec, out_spec],
        out_shape=[out_shape, out_shape],
        compiler_params=_cparams(("arbitrary", "arbitrary"), 32),
        name="prepare_up_weights",
    )(w3, jnp.asarray(even, BF16), jnp.asarray(odd, BF16))


def _expert_kernel(be_ref, tok_cur_ref, tok_next_ref, h_ref, wg_ref, wl_ref, bg_ref, bl_ref, wd_ref, bd_ref,
                   y_ref, xbuf, act_s, sem, *, row_block):
    i = pl.program_id(0)
    n_blocks = pl.num_programs(0)
    slot = i % 2

    def block_wait(dst_slot):
        pltpu.make_async_copy(h_ref.at[pl.ds(0, row_block), :], xbuf.at[dst_slot], sem.at[dst_slot]).wait()

    def start_gather(tok_ref, dst_slot, rows, anchor):
        for r in rows:
            pltpu.make_async_copy(h_ref.at[pl.ds(tok_ref[0, r] + anchor, 1), :],
                                  xbuf.at[dst_slot, pl.ds(r, 1), :], sem.at[dst_slot]).start()

    @pl.when(i == 0)
    def _():
        start_gather(tok_cur_ref, 0, range(row_block), 0)

    block_wait(slot)
    x = xbuf[slot].astype(BF16)
    f = wg_ref.shape[1]
    up_chunk = min(UP_CHUNK, f)
    n_chunks = f // up_chunk
    rows_per_chunk = -(-row_block // n_chunks)
    def anchored_gather(result, rows):
        bits = lax.bitcast_convert_type(result[0:1, 0:1], jnp.int32)[0, 0]
        start_gather(tok_next_ref, 1 - slot, rows, (bits & 0x7FFFFFFF) >> 31)

    for c in range(n_chunks):
        cl = slice(c * up_chunk, (c + 1) * up_chunk)
        lo = min(c * rows_per_chunk, row_block)
        hi = min((c + 1) * rows_per_chunk, row_block)
        mid = (lo + hi) // 2
        glu = jnp.minimum(_dot(x, wg_ref[:, cl]) + bg_ref[:, cl], SWIGLU_LIMIT)
        anchored_gather(glu, range(lo, mid))
        lin = jnp.clip(_dot(x, wl_ref[:, cl]) + bl_ref[:, cl], -SWIGLU_LIMIT, SWIGLU_LIMIT)
        anchored_gather(lin, range(mid, hi))
        act_s[:, cl] = (glu * jax.nn.sigmoid(SWIGLU_ALPHA * glu) * (lin + 1.0)).astype(BF16)
    y_ref[...] = _dot(act_s[...], wd_ref[...]) + bd_ref[...]

    @pl.when(i == n_blocks - 1)
    def _():
        block_wait(1 - slot)


def expert_ffn(h2, row_tok, block_e, w_glu, w_lin, up_offset, b_glu, b_lin, w_down, b_down, row_block):
    d = h2.shape[1]
    n_rows = row_tok.shape[0]
    n_blocks = n_rows // row_block
    n_exp = w_down.shape[0]
    f = w_glu.shape[2]
    weight_buffers = pl.Buffered(2)
    tok3 = row_tok.reshape(n_blocks, 1, row_block)
    grid_spec = pltpu.PrefetchScalarGridSpec(
        num_scalar_prefetch=1,
        grid=(n_blocks,),
        in_specs=[pl.BlockSpec((None, 1, row_block), lambda i, be: (i, 0, 0), memory_space=pltpu.SMEM),
                  pl.BlockSpec((None, 1, row_block), lambda i, be: (jnp.minimum(i + 1, n_blocks - 1), 0, 0),
                               memory_space=pltpu.SMEM),
                  pl.BlockSpec(memory_space=pl.ANY),
                  pl.BlockSpec((None, d, f), lambda i, be: (be[i] + up_offset, 0, 0), pipeline_mode=weight_buffers),
                  pl.BlockSpec((None, d, f), lambda i, be: (be[i] + up_offset, 0, 0), pipeline_mode=weight_buffers),
                  pl.BlockSpec((None, 1, f), lambda i, be: (be[i], 0, 0)),
                  pl.BlockSpec((None, 1, f), lambda i, be: (be[i], 0, 0)),
                  pl.BlockSpec((None, f, d), lambda i, be: (be[i], 0, 0), pipeline_mode=weight_buffers),
                  pl.BlockSpec((None, 1, d), lambda i, be: (be[i], 0, 0))],
        out_specs=pl.BlockSpec((row_block, d), lambda i, be: (i, 0)),
        scratch_shapes=[pltpu.VMEM((2, row_block, d), F32), pltpu.VMEM((row_block, f), BF16),
                        pltpu.SemaphoreType.DMA((2,))])
    return pl.pallas_call(
        functools.partial(_expert_kernel, row_block=row_block),
        grid_spec=grid_spec,
        out_shape=jax.ShapeDtypeStruct((n_rows, d), F32),
        compiler_params=_cparams(("arbitrary",)),
        name="expert_ffn",
    )(block_e, tok3, tok3, h2, w_glu, w_lin, b_glu.reshape(n_exp, 1, f), b_lin.reshape(n_exp, 1, f),
      w_down, b_down.reshape(n_exp, 1, d))


def _combine_kernel(dest_cur_ref, dest_next_ref, ys_ref, w_ref, x_ref, gate_ref, g_ref, b_ref, sc_ref, sh_ref,
                    x2_ref, h_ref, buf, sem, *, tc, alpha):
    i = pl.program_id(0)
    n_tiles = pl.num_programs(0)
    slot = i % 2

    def start_gather(dest_ref, dst_slot):
        for tok in range(tc):
            for k in range(TOP_K):
                pltpu.make_async_copy(ys_ref.at[pl.ds(dest_ref[0, tok * TOP_K + k], 1), :],
                                      buf.at[dst_slot, k, pl.ds(tok, 1), :], sem.at[dst_slot]).start(priority=k % 2)

    def tile_wait(dst_slot):
        for k in range(TOP_K):
            pltpu.make_async_copy(ys_ref.at[pl.ds(0, tc), :], buf.at[dst_slot, k], sem.at[dst_slot]).wait()

    @pl.when(i == 0)
    def _():
        start_gather(dest_cur_ref, 0)

    tile_wait(slot)
    start_gather(dest_next_ref, 1 - slot)
    w = w_ref[...]
    y = w[:, 0:1] * buf[slot, 0]
    for k in range(1, TOP_K):
        y = y + w[:, k:k + 1] * buf[slot, k]
    x2 = _ln(alpha * x_ref[...] + gate_ref[...] * y) * g_ref[...] + b_ref[...]
    x2_ref[...] = x2
    h_ref[...] = (_ln(x2) * (1.0 + sc_ref[...]) + sh_ref[...]).astype(h_ref.dtype)

    @pl.when(i == n_tiles - 1)
    def _():
        tile_wait(1 - slot)


def moe_combine(ys, dest, top_w, x1, mod_l, mod_next, ln_g, ln_b, seq, alpha):
    t, d = x1.shape
    tc = min(128, seq)
    bpb = seq // tc
    row = pl.BlockSpec((tc, d), lambda i: (i, 0))
    vec = pl.BlockSpec((1, d), lambda i: (0, 0))
    n_tiles = t // tc
    dest3 = dest.reshape(n_tiles, 1, tc * TOP_K)
    return pl.pallas_call(
        functools.partial(_combine_kernel, tc=tc, alpha=alpha),
        grid=(n_tiles,),
        in_specs=[pl.BlockSpec((None, 1, tc * TOP_K), lambda i: (i, 0, 0), memory_space=pltpu.SMEM),
                  pl.BlockSpec((None, 1, tc * TOP_K), lambda i: (jnp.minimum(i + 1, n_tiles - 1), 0, 0),
                               memory_space=pltpu.SMEM),
                  pl.BlockSpec(memory_space=pl.ANY),
                  pl.BlockSpec((tc, LANES), lambda i: (i, 0)),
                  row, _mod_spec(5, d, bpb), vec, vec, _mod_spec(1, d, bpb), _mod_spec(0, d, bpb)],
        out_specs=[row, row],
        out_shape=[jax.ShapeDtypeStruct((t, d), F32), jax.ShapeDtypeStruct((t, d), BF16)],
        scratch_shapes=[pltpu.VMEM((2, TOP_K, tc, d), F32), pltpu.SemaphoreType.DMA((2,))],
        compiler_params=_cparams(("arbitrary",), 32),
        name="moe_combine",
    )(dest3, dest3, ys, top_w, x1, mod_l, ln_g.reshape(1, d), ln_b.reshape(1, d), mod_next, mod_next)


def moe_ffn(h2, logits, w_glu, w_lin, up_offset, b_glu, b_lin, w_down, b_down):
    t, d = h2.shape
    n_exp = w_down.shape[0]
    row_block = 256
    idx_rank, top_w, counts = route(logits, n_exp)
    top_idx = idx_rank[:, :TOP_K]
    rank = idx_rank[:, TOP_K:2 * TOP_K]
    counts = counts[0, :n_exp].astype(jnp.int32)
    padded = (counts + row_block - 1) // row_block * row_block
    padded_end = jnp.cumsum(padded)
    padded_start = padded_end - padded
    dest = padded_start[top_idx] + rank
    n_rows = t * TOP_K + n_exp * row_block
    tok_ids = jnp.broadcast_to(jnp.arange(t, dtype=jnp.int32)[:, None], (t, TOP_K))
    row_tok = jnp.zeros((n_rows,), jnp.int32).at[dest.reshape(-1)].set(tok_ids.reshape(-1), unique_indices=True)
    n_blocks = n_rows // row_block
    block_start = jnp.arange(n_blocks, dtype=jnp.int32) * row_block
    block_e = jnp.minimum(jnp.sum(padded_end[None, :] <= block_start[:, None], axis=1), n_exp - 1).astype(jnp.int32)
    ys = expert_ffn(h2, row_tok, block_e, w_glu, w_lin, up_offset, b_glu, b_lin, w_down, b_down, row_block)
    return ys, dest.astype(jnp.int32), top_w


def kernel(x, c, w_ada, b_ada, w_in, b_forget, w_gate, b_gate, w_proj_sb, w_proj_dil, w_proj_fox, w_out,
           ln1_g, ln1_b, w_router, b_router, w_up, b_up, w_down, b_down, ln2_g, ln2_b):
    batch, seq, d = x.shape
    depth = w_ada.shape[0]
    t = batch * seq
    alpha = (2.0 * depth) ** 0.25
    rope_tabs = rope_pair_tables(seq)
    mod = adaln_mod(c, w_ada, b_ada)
    x2d = x.reshape(t, d)
    h = ln_mod(x2d, mod[0], seq, comp_shift=0, comp_scale=1)
    n_exp = w_up.shape[1]
    w_glu, w_lin = prepare_up_weights(w_up)
    for l in range(depth):
        w_in_l = w_in[l]
        qkv = in_projection(h, w_in_l[:, :N_QKV].astype(BF16), rope_tabs, seq)
        q_feat, k_feat = forget_features(h, w_in_l[:, N_QKV:], b_forget[l], batch, seq)
        o_sb = stick_breaking_attention(qkv, batch, seq)
        o_dl = dilated_window_attention(qkv, batch, seq)
        o_fx = forgetting_attention(qkv, q_feat, k_feat, batch, seq)
        merged = gated_merge(h, w_gate[l].astype(BF16), b_gate[l], (o_sb, o_dl, o_fx),
                             (w_proj_sb[l].astype(BF16), w_proj_dil[l].astype(BF16),
                              w_proj_fox[l].astype(BF16)))
        x1, h2, logits = out_projection(merged, w_out[l].astype(BF16), x2d, mod[l], ln1_g[l], ln1_b[l],
                                        w_router[l], b_router[l], seq, alpha)
        ys, dest, top_w = moe_ffn(h2, logits, w_glu, w_lin, l * n_exp,
                                  b_up[l][:, 0::2], b_up[l][:, 1::2], w_down[l].astype(BF16), b_down[l])
        mod_next = mod[min(l + 1, depth - 1)]
        x2d, h = moe_combine(ys, dest, top_w, x1, mod[l], mod_next, ln2_g[l], ln2_b[l], seq, alpha)
    return x2d.reshape(batch, seq, d)
```

```python
import functools

import numpy as np
import jax
import jax.numpy as jnp
from jax import lax
from jax.experimental import pallas as pl
from jax.experimental.pallas import tpu as pltpu

F32 = jnp.float32
BF16 = jnp.bfloat16

HEAD_DIM = 64
H_SB = 8
DIL_CONFIGS = ((128, 1), (512, 4), (2048, 16))
H_DIL_PER_GROUP = 4
H_FOX = 12
ROPE_THETA = 500000.0
ROPE_DIMS = HEAD_DIM // 4
TOP_K = 4
SWIGLU_ALPHA = 1.702
SWIGLU_LIMIT = 7.0
LN_EPS = 1e-5
QK_SCALE = HEAD_DIM ** -0.5

LANES = 128
SUBLANES = 8
VMEM_BUDGET_MB = 56

N_DIL_GROUPS = len(DIL_CONFIGS)
W_SB = H_SB * HEAD_DIM
W_DIL = N_DIL_GROUPS * H_DIL_PER_GROUP * HEAD_DIM
W_DIL_OUT = H_DIL_PER_GROUP * HEAD_DIM
W_FOX = H_FOX * HEAD_DIM
N_QKV = 3 * (W_SB + W_DIL + W_FOX)
COL_SB = (0, W_SB // LANES, 2 * W_SB // LANES)
_DL0 = 3 * W_SB // LANES
COL_DL = (_DL0, _DL0 + W_DIL // LANES, _DL0 + 2 * W_DIL // LANES)
_FX0 = _DL0 + 3 * W_DIL // LANES
COL_FX = (_FX0, _FX0 + W_FOX // LANES, _FX0 + 2 * W_FOX // LANES)
BAND = 128
NEG_BIG = -1e30
EXP_UNDERFLOW = -104.0


def _cparams(semantics, vmem_mb=VMEM_BUDGET_MB):
    return pltpu.CompilerParams(dimension_semantics=semantics, vmem_limit_bytes=vmem_mb * 2 ** 20)


def _dot(a, b):
    return jnp.dot(a, b, preferred_element_type=F32)


def _dot_t(a, b):
    return lax.dot_general(a, b, (((1,), (1,)), ((), ())), preferred_element_type=F32)


def _ln(x):
    mu = jnp.mean(x, axis=-1, keepdims=True)
    xc = x - mu
    var = jnp.mean(xc * xc, axis=-1, keepdims=True)
    return xc * lax.rsqrt(var + LN_EPS)


def _softplus_neg_abs(z):
    return jnp.log1p(jnp.exp(-jnp.abs(z)))


def _split3(x):
    hi = x.astype(BF16)
    r1 = x - hi.astype(F32)
    mid = r1.astype(BF16)
    lo = (r1 - mid.astype(F32)).astype(BF16)
    return hi, mid, lo


def _mod_kernel(c_ref, w_ref, b_ref, o_ref):
    c = c_ref[...]
    act = (c * jax.nn.sigmoid(c)).astype(BF16)
    o_ref[...] = _dot(act, w_ref[...].astype(BF16)) + b_ref[...]


def adaln_mod(c, w_ada, b_ada):
    n_layers, d, n = w_ada.shape
    b = c.shape[0]
    assert b <= SUBLANES
    tn = 512
    c_pad = jnp.zeros((SUBLANES, d), F32).at[:b].set(c)
    out = pl.pallas_call(
        _mod_kernel,
        grid=(n_layers, n // tn),
        in_specs=[pl.BlockSpec((SUBLANES, d), lambda l, j: (0, 0)),
                  pl.BlockSpec((None, d, tn), lambda l, j: (l, 0, j)),
                  pl.BlockSpec((None, 1, tn), lambda l, j: (l, 0, j))],
        out_specs=pl.BlockSpec((None, SUBLANES, tn), lambda l, j: (l, 0, j)),
        out_shape=jax.ShapeDtypeStruct((n_layers, SUBLANES, n), F32),
        compiler_params=_cparams(("arbitrary", "arbitrary"), 32),
        name="adaln_mod",
    )(c_pad, w_ada, b_ada.reshape(n_layers, 1, n))
    return out[:, :b].reshape(n_layers, b, 6, d).transpose(0, 2, 1, 3)[:, :, :, None, :]


def _mod_spec(comp, d, rows_per_batch_blocks):
    return pl.BlockSpec((None, None, 1, d), lambda i, *_: (comp, i // rows_per_batch_blocks, 0, 0))


def _ln_mod_kernel(x_ref, sc_ref, sh_ref, h_ref):
    h_ref[...] = (_ln(x_ref[...]) * (1.0 + sc_ref[...]) + sh_ref[...]).astype(h_ref.dtype)


def ln_mod(x2d, mod_l, seq, comp_shift, comp_scale):
    t, d = x2d.shape
    ts = min(512, seq)
    bpb = seq // ts
    return pl.pallas_call(
        _ln_mod_kernel,
        grid=(t // ts,),
        in_specs=[pl.BlockSpec((ts, d), lambda i: (i, 0)),
                  _mod_spec(comp_scale, d, bpb), _mod_spec(comp_shift, d, bpb)],
        out_specs=pl.BlockSpec((ts, d), lambda i: (i, 0)),
        out_shape=jax.ShapeDtypeStruct((t, d), BF16),
        compiler_params=_cparams(("arbitrary",), 32),
        name="ln_mod",
    )(x2d, mod_l, mod_l)


def _inproj_kernel(h_ref, w_ref, cos_ref, s1_ref, s2_ref, o_ref, *, rope_lo, rope_hi, tn):
    j = pl.program_id(1)
    acc = _dot(h_ref[...], w_ref[...])
    is_rope = jnp.logical_and(j >= rope_lo, j < rope_hi)

    @pl.when(is_rope)
    def _():
        c, s1, s2 = cos_ref[...], s1_ref[...], s2_ref[...]
        for blk in range(tn // LANES):
            a = acc[:, blk * LANES:(blk + 1) * LANES]
            r = a * c + pltpu.roll(a, LANES - ROPE_DIMS // 2, 1) * s1 + pltpu.roll(a, ROPE_DIMS // 2, 1) * s2
            o_ref[:, blk * LANES:(blk + 1) * LANES] = r.astype(o_ref.dtype)

    @pl.when(jnp.logical_not(is_rope))
    def _():
        o_ref[...] = acc.astype(o_ref.dtype)


def rope_pair_tables(seq):
    pos = jnp.arange(seq, dtype=F32)
    inv = ROPE_THETA ** (-jnp.arange(0, ROPE_DIMS, 2, dtype=F32) / ROPE_DIMS)
    ang = pos[:, None] * inv[None, :]
    cos, sin = jnp.cos(ang), jnp.sin(ang)
    half = ROPE_DIMS // 2
    head_c = jnp.concatenate([cos, cos, jnp.ones((seq, HEAD_DIM - ROPE_DIMS), F32)], axis=1)
    head_s1 = jnp.concatenate([-sin, jnp.zeros((seq, HEAD_DIM - half), F32)], axis=1)
    head_s2 = jnp.concatenate([jnp.zeros((seq, half), F32), sin,
                               jnp.zeros((seq, HEAD_DIM - ROPE_DIMS), F32)], axis=1)
    two = lambda a: jnp.concatenate([a, a], axis=1)
    return two(head_c), two(head_s1), two(head_s2)


def in_projection(h, w_qkv, rope_tabs, seq):
    t, d = h.shape
    n = w_qkv.shape[1]
    tm = min(1024, seq)
    tn = 1536
    assert seq % tm == 0 and n % tn == 0
    rope_lo, rope_hi = COL_DL[0] * LANES, COL_DL[2] * LANES
    assert rope_lo % tn == 0 and rope_hi % tn == 0
    spb = seq // tm
    tab_spec = pl.BlockSpec((tm, LANES), lambda i, j: (i % spb, 0))
    return pl.pallas_call(
        functools.partial(_inproj_kernel, rope_lo=rope_lo // tn, rope_hi=rope_hi // tn, tn=tn),
        grid=(t // tm, n // tn),
        in_specs=[pl.BlockSpec((tm, d), lambda i, j: (i, 0)),
                  pl.BlockSpec((d, tn), lambda i, j: (0, j)),
                  tab_spec, tab_spec, tab_spec],
        out_specs=pl.BlockSpec((tm, tn), lambda i, j: (i, j)),
        out_shape=jax.ShapeDtypeStruct((t, n), BF16),
        compiler_params=_cparams(("arbitrary", "arbitrary"), 48),
        name="in_projection",
    )(h, w_qkv, *rope_tabs)


FEAT_PER_HEAD = 6


def _forget_feature_maps():
    width = (H_FOX // 2) * LANES
    pq = np.zeros((3 * LANES, width), np.float32)
    pk = np.zeros((3 * LANES, width), np.float32)
    cq = np.zeros((1, width), np.float32)
    ck = np.zeros((1, width), np.float32)
    for head in range(H_FOX):
        base = (head // 2) * LANES + (head % 2) * FEAT_PER_HEAD
        for piece in range(3):
            pq[piece * LANES + head, base + piece] = 1.0
            pk[piece * LANES + head, base + 3 + piece] = -1.0
        cq[0, base + 3:base + 6] = 1.0
        ck[0, base:base + 3] = 1.0
    return pq, pk, cq, ck


def _forget_kernel(h_ref, w_ref, b_ref, tri_ref, pq_ref, pk_ref, cq_ref, ck_ref, qf_ref, kf_ref, carry_ref):
    @pl.when(pl.program_id(1) == 0)
    def _():
        carry_ref[...] = jnp.zeros_like(carry_ref)

    f = _dot(h_ref[...], w_ref[...]) + b_ref[...]
    log_f = jnp.minimum(f, 0.0) - _softplus_neg_abs(f)
    tri = tri_ref[...]
    hi, mid, lo = _split3(log_f)
    cum = _dot(tri, hi) + _dot(tri, mid) + _dot(tri, lo) + carry_ref[0:1, :]
    carry_ref[...] = jnp.broadcast_to(cum[-1:, :], carry_ref.shape)
    pieces = jnp.concatenate(_split3(cum), axis=1)
    qf_ref[...] = (_dot(pieces, pq_ref[...]) + cq_ref[...]).astype(qf_ref.dtype)
    kf_ref[...] = (_dot(pieces, pk_ref[...]) + ck_ref[...]).astype(kf_ref.dtype)


def forget_features(h, w_f, b_f, batch, seq):
    t, d = h.shape
    ts = min(256, seq)
    nsb = seq // ts
    w_pad = jnp.zeros((d, LANES), BF16).at[:, :H_FOX].set(w_f.astype(BF16))
    b_pad = jnp.zeros((1, LANES), F32).at[0, :H_FOX].set(b_f)
    tri = (np.arange(ts)[:, None] >= np.arange(ts)[None, :]).astype(np.float32)
    pq, pk, cq, ck = _forget_feature_maps()
    width = pq.shape[1]
    const = lambda shape: pl.BlockSpec(shape, lambda b, s: (0, 0))
    out_spec = pl.BlockSpec((ts, width), lambda b, s: (b * nsb + s, 0))
    out_shape = jax.ShapeDtypeStruct((t, width), BF16)
    return pl.pallas_call(
        _forget_kernel,
        grid=(batch, nsb),
        in_specs=[pl.BlockSpec((ts, d), lambda b, s: (b * nsb + s, 0)),
                  const((d, LANES)), const((1, LANES)), const((ts, ts)),
                  const(pq.shape), const(pk.shape), const(cq.shape), const(ck.shape)],
        out_specs=[out_spec, out_spec],
        out_shape=[out_shape, out_shape],
        scratch_shapes=[pltpu.VMEM((SUBLANES, LANES), F32)],
        compiler_params=_cparams(("arbitrary", "arbitrary"), 32),
        name="forget_features",
    )(h, w_pad, b_pad, jnp.asarray(tri, BF16), jnp.asarray(pq, BF16), jnp.asarray(pk, BF16),
      jnp.asarray(cq), jnp.asarray(ck))


def _pair_masks(tq):
    lane = lax.broadcasted_iota(jnp.int32, (tq, LANES), 1)
    return lane < HEAD_DIM


KEY_SUB = 128
QRY_SUB = 128
UP_CHUNK = 512
SUFFIX_SUB = 256


def _diag_tile_kind(ks, qs, key_sub, qry_sub, strict, key_off):
    if key_off is None:
        return 'full'
    k_lo, k_hi = key_off + ks * key_sub, key_off + ks * key_sub + key_sub - 1
    q_lo, q_hi = qs * qry_sub, qs * qry_sub + qry_sub - 1
    if strict:
        if k_lo >= q_hi:
            return 'dead'
        return 'full' if k_hi < q_lo else 'partial'
    if k_lo > q_hi:
        return 'dead'
    return 'full' if k_hi <= q_lo else 'partial'


def _tile_allowed(ks, qs, key_sub, qry_sub, strict, key_off):
    kpos = key_off + ks * key_sub + lax.broadcasted_iota(jnp.int32, (key_sub, qry_sub), 0)
    qpos = qs * qry_sub + lax.broadcasted_iota(jnp.int32, (key_sub, qry_sub), 1)
    return kpos < qpos if strict else kpos <= qpos


def _pair_transpose_in(qkv3, col, width):
    return lax.slice_in_dim(qkv3, col * LANES, col * LANES + width, axis=2).transpose(0, 2, 1)


def _sb_kernel(q_ref, k_ref, vt_ref, later_ref, o_ref,
               qx_s, acc_s, run_s, first_s, z_s, hi_s, lo_s, btw_s, w_s, *, tq, tk, key_sub, qry_sub):
    q_blk = pl.program_id(2)
    head_a = _pair_masks(tq)
    q = q_ref[...] * QK_SCALE
    qx_s[0] = jnp.where(head_a, q, 0).astype(BF16)
    qx_s[1] = jnp.where(head_a, 0, q).astype(BF16)
    acc_s[...] = jnp.zeros_like(acc_s)
    run_s[...] = jnp.zeros_like(run_s)

    def step(key_off, kv_blk):
        k_ref_blk = k_ref.at[pl.ds(pl.multiple_of(kv_blk * tk, tk), tk), :]
        vt_blk = vt_ref.at[kv_blk]
        _sb_block(key_off, k_ref_blk, vt_blk)

    def _sb_block(key_off, k_ref, vt_ref):
        suf = later_ref.shape[0]
        n_suf = tk // suf
        per_suf = suf // key_sub
        tiles = [(ks, qs) for ks in range(tk // key_sub) for qs in range(tq // qry_sub)]
        kind_of = lambda ks, qs: _diag_tile_kind(ks, qs, key_sub, qry_sub, True, key_off)
        k = k_ref[...]
        for head in range(2):
            z_s[head] = _dot_t(k, qx_s[head])
        for head in range(2):
            for ks, qs in tiles:
                kl = slice(ks * key_sub, (ks + 1) * key_sub)
                ql = slice(qs * qry_sub, (qs + 1) * qry_sub)
                first_row = head * n_suf + ks // per_suf
                kind = kind_of(ks, qs)
                if kind == 'dead':
                    hi_s[head, kl, ql] = jnp.zeros((key_sub, qry_sub), BF16)
                    lo_s[head, kl, ql] = jnp.zeros((key_sub, qry_sub), BF16)
                    if ks % per_suf == 0:
                        first_s[first_row:first_row + 1, ql] = jnp.zeros((1, qry_sub), F32)
                    continue
                z = z_s[head, kl, ql]
                log_stop = jnp.minimum(z, 0.0) - jnp.log(1.0 + jnp.exp(-jnp.abs(z)))
                log_cont = log_stop - z
                if kind == 'partial':
                    log_cont = jnp.where(_tile_allowed(ks, qs, key_sub, qry_sub, True, key_off), log_cont, 0.0)
                z_s[head, kl, ql] = log_stop
                hi = log_cont.astype(BF16)
                hi_s[head, kl, ql] = hi
                lo_s[head, kl, ql] = (log_cont - hi.astype(F32)).astype(BF16)
                if ks % per_suf == 0:
                    first_s[first_row:first_row + 1, ql] = log_cont[0:1, :]
        later = later_ref[...]
        for head in range(2):
            for blk in range(n_suf):
                bl = slice(blk * suf, (blk + 1) * suf)
                btw_s[head, bl, :] = _dot(later, hi_s[head, bl, :]) + _dot(later, lo_s[head, bl, :])

        def block_sum(head, blk, ql):
            row = head * n_suf + blk
            return btw_s[head, blk * suf:blk * suf + 1, ql] + first_s[row:row + 1, ql]

        for head in range(2):
            for ks, qs in tiles:
                kl = slice(ks * key_sub, (ks + 1) * key_sub)
                ql = slice(qs * qry_sub, (qs + 1) * qry_sub)
                kind = kind_of(ks, qs)
                if kind == 'dead':
                    w_s[head, kl, ql] = jnp.zeros((key_sub, qry_sub), BF16)
                    continue
                after = run_s[head:head + 1, ql]
                for blk in range(ks // per_suf + 1, n_suf):
                    after = after + block_sum(head, blk, ql)
                w = jnp.exp(z_s[head, kl, ql] + btw_s[head, kl, ql] + after)
                if kind == 'partial':
                    w = jnp.where(_tile_allowed(ks, qs, key_sub, qry_sub, True, key_off), w, 0.0)
                w_s[head, kl, ql] = w.astype(BF16)
        for head in range(2):
            rows = slice(head * HEAD_DIM, (head + 1) * HEAD_DIM)
            acc_s[rows, :] += _dot(vt_ref[rows, :], w_s[head])
            total = run_s[head:head + 1, :]
            for blk in range(n_suf):
                total = total + block_sum(head, blk, slice(None))
            run_s[head:head + 1, :] = total

    def all_weights_zero():
        return jnp.max(run_s[0:2, :]) < EXP_UNDERFLOW

    step(0, q_blk)

    def more(carry):
        kv_blk, dead = carry
        return jnp.logical_and(kv_blk >= 0, jnp.logical_not(dead))

    def visit(carry):
        kv_blk, _ = carry
        step(None, kv_blk)
        return kv_blk - 1, all_weights_zero()

    lax.while_loop(more, visit, (q_blk - 1, all_weights_zero()))
    o_ref[...] = acc_s[...].astype(o_ref.dtype)


def stick_breaking_attention(qkv, batch, seq):
    tq = tk = min(512, seq)
    key_sub, qry_sub = min(KEY_SUB, tk), min(QRY_SUB, tq)
    n_pairs = W_SB // LANES
    n_kv = seq // tk
    qkv3 = qkv.reshape(batch, seq, qkv.shape[-1])
    qc, kc, vc = COL_SB
    v_t = _pair_transpose_in(qkv3, vc, W_SB).reshape(batch, n_pairs, LANES, n_kv, tk).transpose(0, 1, 3, 2, 4)
    suf = min(SUFFIX_SUB, tk)
    assert 2 * (tk // suf) <= SUBLANES and suf % key_sub == 0
    later = jnp.asarray((np.arange(suf)[None, :] > np.arange(suf)[:, None]).astype(np.float32), BF16)
    block_f32 = pltpu.VMEM((2, tk, tq), F32)
    block_bf16 = pltpu.VMEM((2, tk, tq), BF16)
    row_stats = pltpu.VMEM((SUBLANES, tq), F32)
    out_t = pl.pallas_call(
        functools.partial(_sb_kernel, tq=tq, tk=tk, key_sub=key_sub, qry_sub=qry_sub),
        grid=(batch, n_pairs, seq // tq),
        in_specs=[pl.BlockSpec((None, tq, LANES), lambda b, p, i: (b, i, qc + p)),
                  pl.BlockSpec((None, seq, LANES), lambda b, p, i: (b, 0, kc + p)),
                  pl.BlockSpec((None, None, n_kv, LANES, tk), lambda b, p, i: (b, p, 0, 0, 0)),
                  pl.BlockSpec((suf, suf), lambda b, p, i: (0, 0))],
        out_specs=pl.BlockSpec((None, LANES, tq), lambda b, p, i: (b, p, i)),
        scratch_shapes=[pltpu.VMEM((2, tq, LANES), BF16), pltpu.VMEM((LANES, tq), F32),
                        row_stats, row_stats, block_f32, block_bf16, block_bf16, block_f32, block_bf16],
        out_shape=jax.ShapeDtypeStruct((batch, W_SB, seq), BF16),
        compiler_params=_cparams(("arbitrary", "arbitrary", "arbitrary"), 32),
        name="stick_breaking_attention",
    )(qkv3, qkv3, v_t, later)
    return out_t.transpose(0, 2, 1).reshape(batch * seq, W_SB)


def _fox_kernel(q_ref, qf_ref, k_ref, kf_ref, vt_ref, o_ref,
                qx_s, acc_s, m_s, l_s, alpha_s, z_s, p_s, *, tq, tk, key_sub, qry_sub):
    q_blk = pl.program_id(2)
    lane = lax.broadcasted_iota(jnp.int32, (tq, LANES), 1)
    q = q_ref[...] * QK_SCALE
    qf = qf_ref[...]
    for head in range(2):
        in_head = (lane >= head * HEAD_DIM) & (lane < (head + 1) * HEAD_DIM)
        in_feat = (lane >= head * FEAT_PER_HEAD) & (lane < (head + 1) * FEAT_PER_HEAD)
        qx_s[head] = jnp.concatenate([jnp.where(in_head, q, 0).astype(BF16),
                                      jnp.where(in_feat, qf, 0).astype(BF16)], axis=1)
    acc_s[...] = jnp.zeros_like(acc_s)
    l_s[...] = jnp.zeros_like(l_s)
    m_s[...] = jnp.full_like(m_s, NEG_BIG)

    def step(key_off, kv_blk):
        rows = pl.ds(pl.multiple_of(kv_blk * tk, tk), tk)
        _fox_block(key_off, k_ref.at[rows, :], kf_ref.at[rows, :], vt_ref.at[kv_blk])

    def _fox_block(key_off, k_ref, kf_ref, vt_ref):
        kind_of = lambda ks, qs: _diag_tile_kind(ks, qs, key_sub, qry_sub, False, key_off)
        kx = jnp.concatenate([k_ref[...], kf_ref[...]], axis=1)
        for head in range(2):
            z_s[head] = _dot_t(kx, qx_s[head])
        for head in range(2):
            for qs in range(tq // qry_sub):
                ql = slice(qs * qry_sub, (qs + 1) * qry_sub)

                def score(ks):
                    z = z_s[head, ks * key_sub:(ks + 1) * key_sub, ql]
                    if kind_of(ks, qs) == 'partial':
                        z = jnp.where(_tile_allowed(ks, qs, key_sub, qry_sub, False, key_off), z, NEG_BIG)
                    return z

                live = [ks for ks in range(tk // key_sub) if kind_of(ks, qs) != 'dead']
                m_old = m_s[head:head + 1, ql]
                m_new = m_old
                for ks in live:
                    m_new = jnp.maximum(m_new, jnp.max(score(ks), axis=0, keepdims=True))
                alpha = jnp.exp(m_old - m_new)
                total = alpha * l_s[head:head + 1, ql]
                for ks in range(tk // key_sub):
                    kl = slice(ks * key_sub, (ks + 1) * key_sub)
                    if ks not in live:
                        p_s[head, kl, ql] = jnp.zeros((key_sub, qry_sub), BF16)
                        continue
                    prob = jnp.exp(score(ks) - m_new)
                    total = total + jnp.sum(prob, axis=0, keepdims=True)
                    p_s[head, kl, ql] = prob.astype(BF16)
                m_s[head:head + 1, ql] = m_new
                l_s[head:head + 1, ql] = total
                alpha_s[head:head + 1, ql] = alpha
        for head in range(2):
            rows = slice(head * HEAD_DIM, (head + 1) * HEAD_DIM)
            acc_s[rows, :] = alpha_s[head:head + 1, :] * acc_s[rows, :] + _dot(vt_ref[rows, :], p_s[head])

    per_q = tq // tk
    first_diag = q_blk * per_q

    def visit(kv_blk, carry):
        step(None, kv_blk)
        return carry

    lax.fori_loop(0, first_diag, visit, 0)
    for variant in range(per_q):
        step(variant * tk, first_diag + variant)
    for head in range(2):
        rows = slice(head * HEAD_DIM, (head + 1) * HEAD_DIM)
        o_ref[rows, :] = (acc_s[rows, :] / l_s[head:head + 1, :]).astype(o_ref.dtype)


def forgetting_attention(qkv, q_feat, k_feat, batch, seq):
    tk = min(512, seq)
    tq = min(1024, seq)
    key_sub, qry_sub = min(KEY_SUB, tk), min(QRY_SUB, tq)
    assert tq % tk == 0
    n_pairs = W_FOX // LANES
    n_kv = seq // tk
    qkv3 = qkv.reshape(batch, seq, qkv.shape[-1])
    qf3 = q_feat.reshape(batch, seq, n_pairs * LANES)
    kf3 = k_feat.reshape(batch, seq, n_pairs * LANES)
    qc, kc, vc = COL_FX
    v_t = _pair_transpose_in(qkv3, vc, W_FOX).reshape(batch, n_pairs, LANES, n_kv, tk).transpose(0, 1, 3, 2, 4)
    out_t = pl.pallas_call(
        functools.partial(_fox_kernel, tq=tq, tk=tk, key_sub=key_sub, qry_sub=qry_sub),
        grid=(batch, n_pairs, seq // tq),
        in_specs=[pl.BlockSpec((None, tq, LANES), lambda b, p, i: (b, i, qc + p)),
                  pl.BlockSpec((None, tq, LANES), lambda b, p, i: (b, i, p)),
                  pl.BlockSpec((None, seq, LANES), lambda b, p, i: (b, 0, kc + p)),
                  pl.BlockSpec((None, seq, LANES), lambda b, p, i: (b, 0, p)),
                  pl.BlockSpec((None, None, n_kv, LANES, tk), lambda b, p, i: (b, p, 0, 0, 0))],
        out_specs=pl.BlockSpec((None, LANES, tq), lambda b, p, i: (b, p, i)),
        scratch_shapes=[pltpu.VMEM((2, tq, 2 * LANES), BF16), pltpu.VMEM((LANES, tq), F32),
                        pltpu.VMEM((SUBLANES, tq), F32), pltpu.VMEM((SUBLANES, tq), F32),
                        pltpu.VMEM((SUBLANES, tq), F32), pltpu.VMEM((2, tk, tq), F32),
                        pltpu.VMEM((2, tk, tq), BF16)],
        out_shape=jax.ShapeDtypeStruct((batch, W_FOX, seq), BF16),
        compiler_params=_cparams(("arbitrary", "arbitrary", "arbitrary"), 40),
        name="forgetting_attention",
    )(qkv3, qf3, qkv3, kf3, v_t)
    return out_t.transpose(0, 2, 1).reshape(batch * seq, W_FOX)


def _band_kernel(q_ref, kc_ref, vc_ref, kp_ref, vp_ref, o_ref, lse_ref, *, tq):
    i = pl.program_id(1)
    head_a = _pair_masks(tq)
    diff = lax.broadcasted_iota(jnp.int32, (tq, tq), 0) - lax.broadcasted_iota(jnp.int32, (tq, tq), 1)
    mask_c = jnp.logical_and(diff >= 0, diff <= BAND)
    rp = lax.broadcasted_iota(jnp.int32, (tq, BAND), 0)
    cp = lax.broadcasted_iota(jnp.int32, (tq, BAND), 1)
    mask_p = jnp.logical_and(cp >= rp, i > 0)
    for pair in range(q_ref.shape[1] // LANES):
        pl_ = slice(pair * LANES, (pair + 1) * LANES)
        q = q_ref[:, pl_] * QK_SCALE
        kc, vc, kp, vp = kc_ref[:, pl_], vc_ref[:, pl_], kp_ref[:, pl_], vp_ref[:, pl_]
        outs, lses = [], []
        for sel in (head_a, jnp.logical_not(head_a)):
            qh = jnp.where(sel, q, 0).astype(BF16)
            sc = jnp.where(mask_c, _dot_t(qh, kc), NEG_BIG)
            sp = jnp.where(mask_p, _dot_t(qh, kp), NEG_BIG)
            m = jnp.maximum(jnp.max(sc, axis=1, keepdims=True), jnp.max(sp, axis=1, keepdims=True))
            ec = jnp.exp(sc - m)
            ep = jnp.exp(sp - m)
            den = jnp.sum(ec, axis=1, keepdims=True) + jnp.sum(ep, axis=1, keepdims=True)
            outs.append((_dot(ec.astype(BF16), vc) + _dot(ep.astype(BF16), vp)) / den)
            lses.append(m + jnp.log(den))
        o_ref[:, pl_] = jnp.where(head_a, outs[0], outs[1])
        lse_ref[:, pl_] = jnp.where(head_a, lses[0], lses[1])


def _band_attention(q, k, v):
    nb, u, w = q.shape
    tq = min(256, u)
    assert u % tq == 0 and tq % BAND == 0
    sub = tq // BAND
    cur = pl.BlockSpec((None, tq, w), lambda n, i: (n, i, 0))
    prev = pl.BlockSpec((None, BAND, w), lambda n, i: (n, jnp.maximum(i * sub - 1, 0), 0))
    shp = jax.ShapeDtypeStruct((nb, u, w), F32)
    return pl.pallas_call(
        functools.partial(_band_kernel, tq=tq),
        grid=(nb, u // tq),
        in_specs=[cur, cur, cur, prev, prev],
        out_specs=[cur, cur],
        out_shape=[shp, shp],
        compiler_params=_cparams(("arbitrary", "arbitrary"), 32),
        name="band_attention",
    )(q, k, v, k, v)


def _dil_mix_kernel(o0, o1, o2, l0, l1, l2, out_ref):
    a, b, c = l0[...], l1[...], l2[...]
    m = jnp.maximum(jnp.maximum(a, b), c)
    ea, eb, ec = jnp.exp(a - m), jnp.exp(b - m), jnp.exp(c - m)
    tot = ea + eb + ec
    out_ref[...] = ((o0[...] * ea + o1[...] * eb + o2[...] * ec) / tot).astype(out_ref.dtype)


def dilated_window_attention(qkv, batch, seq):
    t = batch * seq
    qkv3 = qkv.reshape(batch, seq, qkv.shape[-1])
    gw = W_DIL_OUT
    outs, lses = [], []
    for g, (window, dil) in enumerate(DIL_CONFIGS):
        assert window // dil == BAND and seq % (dil * BAND) == 0
        u = seq // dil

        def split(col):
            a = lax.slice_in_dim(qkv3, col * LANES + g * gw, col * LANES + (g + 1) * gw, axis=2)
            return a.reshape(batch, u, dil, gw).transpose(0, 2, 1, 3).reshape(batch * dil, u, gw)

        o, lse = _band_attention(split(COL_DL[0]), split(COL_DL[1]), split(COL_DL[2]))
        merge = lambda a: a.reshape(batch, dil, u, gw).transpose(0, 2, 1, 3).reshape(t, gw)
        outs.append(merge(o))
        lses.append(merge(lse))
    tm = min(1024, t)
    spec = pl.BlockSpec((tm, gw), lambda i: (i, 0))
    return pl.pallas_call(
        _dil_mix_kernel,
        grid=(t // tm,),
        in_specs=[spec] * 6,
        out_specs=spec,
        out_shape=jax.ShapeDtypeStruct((t, gw), BF16),
        compiler_params=_cparams(("arbitrary",), 32),
        name="dilated_mix",
    )(*outs, *lses)


def _merge_kernel(h_ref, wg0, wg1, wg2, bg0, bg1, bg2, o0, o1, o2, wp0, wp1, wp2, out_ref):
    h = h_ref[...]

    def branch(wg, bg, o, wp):
        return jax.nn.sigmoid(_dot(h, wg[...]) + bg[...]) * _dot(o[...], wp[...])

    merged = branch(wg0, bg0, o0, wp0) + branch(wg1, bg1, o1, wp1) + branch(wg2, bg2, o2, wp2)
    out_ref[...] = merged.astype(out_ref.dtype)


def gated_merge(h, w_gate, b_gate, branch_outs, branch_projs):
    t, d = h.shape
    tm, tn = min(1024, t), min(512, d)
    nj = d // tn
    b_gate2 = b_gate.reshape(1, -1)
    gate_w = [pl.BlockSpec((d, tn), lambda j, i, br=br: (0, br * nj + j)) for br in range(3)]
    gate_b = [pl.BlockSpec((1, tn), lambda j, i, br=br: (0, br * nj + j)) for br in range(3)]
    o_specs = [pl.BlockSpec((tm, o.shape[1]), lambda j, i: (i, 0)) for o in branch_outs]
    p_specs = [pl.BlockSpec((w.shape[0], tn), lambda j, i: (0, j)) for w in branch_projs]
    return pl.pallas_call(
        _merge_kernel,
        grid=(nj, t // tm),
        in_specs=[pl.BlockSpec((tm, d), lambda j, i: (i, 0))] + gate_w + gate_b + o_specs + p_specs,
        out_specs=pl.BlockSpec((tm, tn), lambda j, i: (i, j)),
        out_shape=jax.ShapeDtypeStruct((t, d), BF16),
        compiler_params=_cparams(("arbitrary", "arbitrary"), 48),
        name="gated_merge",
    )(h, w_gate, w_gate, w_gate, b_gate2, b_gate2, b_gate2, *branch_outs, *branch_projs)


def _outproj_kernel(m_ref, w_ref, x_ref, gate_ref, g_ref, b_ref, sc_ref, sh_ref, wrh_ref, wrl_ref, br_ref,
                    x1_ref, h2_ref, lg_ref, *, alpha):
    y = _dot(m_ref[...], w_ref[...])
    x1 = _ln(alpha * x_ref[...] + gate_ref[...] * y) * g_ref[...] + b_ref[...]
    x1_ref[...] = x1
    h2 = _ln(x1) * (1.0 + sc_ref[...]) + sh_ref[...]
    h2_ref[...] = h2
    hi = h2.astype(BF16)
    lo = (h2 - hi.astype(F32)).astype(BF16)
    wrh = wrh_ref[...]
    lg_ref[...] = _dot(hi, wrh) + _dot(lo, wrh) + _dot(hi, wrl_ref[...]) + br_ref[...]


def out_projection(merged, w_out, x2d, mod_l, ln_g, ln_b, w_router, b_router, seq, alpha):
    t, d = x2d.shape
    tm = min(256, seq)
    bpb = seq // tm
    n_exp = w_router.shape[1]
    wr = jnp.zeros((d, LANES), F32).at[:, :n_exp].set(w_router)
    wr_hi = wr.astype(BF16)
    wr_lo = (wr - wr_hi.astype(F32)).astype(BF16)
    br = jnp.zeros((1, LANES), F32).at[0, :n_exp].set(b_router)
    row = pl.BlockSpec((tm, d), lambda i: (i, 0))
    vec = pl.BlockSpec((1, d), lambda i: (0, 0))
    wr_spec = pl.BlockSpec((d, LANES), lambda i: (0, 0))
    return pl.pallas_call(
        functools.partial(_outproj_kernel, alpha=alpha),
        grid=(t // tm,),
        in_specs=[row, pl.BlockSpec((d, d), lambda i: (0, 0)), row, _mod_spec(2, d, bpb), vec, vec,
                  _mod_spec(4, d, bpb), _mod_spec(3, d, bpb), wr_spec, wr_spec,
                  pl.BlockSpec((1, LANES), lambda i: (0, 0))],
        out_specs=[row, row, pl.BlockSpec((tm, LANES), lambda i: (i, 0))],
        out_shape=[jax.ShapeDtypeStruct((t, d), F32), jax.ShapeDtypeStruct((t, d), F32),
                   jax.ShapeDtypeStruct((t, LANES), F32)],
        compiler_params=_cparams(("arbitrary",), 48),
        name="out_projection",
    )(merged, w_out, x2d, mod_l, ln_g.reshape(1, d), ln_b.reshape(1, d), mod_l, mod_l, wr_hi, wr_lo, br)


def _route_kernel(lg_ref, tri_ref, ir_ref, w_ref, cnt_ref, carry_ref, *, n_exp):
    @pl.when(pl.program_id(0) == 0)
    def _():
        carry_ref[...] = jnp.zeros_like(carry_ref)

    tr = lg_ref.shape[0]
    lane = lax.broadcasted_iota(jnp.int32, (tr, LANES), 1)
    logits = jnp.where(lane < n_exp, lg_ref[...], -jnp.inf)
    vals, sels, idxs = [], [], []
    for _ in range(TOP_K):
        m = jnp.max(logits, axis=1, keepdims=True)
        idx = jnp.min(jnp.where(logits == m, lane, LANES), axis=1, keepdims=True)
        sel = lane == idx
        vals.append(m)
        idxs.append(idx)
        sels.append(sel)
        logits = jnp.where(sel, -jnp.inf, logits)
    exps = [jnp.exp(v - vals[0]) for v in vals]
    tot = exps[0] + exps[1] + exps[2] + exps[3]
    onehot = jnp.zeros((tr, LANES), F32)
    for sel in sels:
        onehot = onehot + jnp.where(sel, 1.0, 0.0)
    before = _dot(tri_ref[...], onehot.astype(BF16)) + carry_ref[0:1, :]
    ir = jnp.zeros((tr, LANES), jnp.int32)
    wt = jnp.zeros((tr, LANES), F32)
    for k in range(TOP_K):
        rank = jnp.sum(jnp.where(sels[k], before, 0.0), axis=1, keepdims=True).astype(jnp.int32)
        ir = jnp.where(lane == k, idxs[k], ir)
        ir = jnp.where(lane == TOP_K + k, rank, ir)
        wt = jnp.where(lane == k, exps[k] / tot, wt)
    ir_ref[...] = ir
    w_ref[...] = wt
    total = before[-1:, :] + onehot[-1:, :]
    carry_ref[...] = jnp.broadcast_to(total, carry_ref.shape)
    cnt_ref[...] = jnp.broadcast_to(total, cnt_ref.shape)


def route(logits, n_exp):
    t = logits.shape[0]
    tr = min(512, t)
    tri = jnp.asarray((np.arange(tr)[:, None] > np.arange(tr)[None, :]).astype(np.float32), BF16)
    row = pl.BlockSpec((tr, LANES), lambda i: (i, 0))
    return pl.pallas_call(
        functools.partial(_route_kernel, n_exp=n_exp),
        grid=(t // tr,),
        in_specs=[row, pl.BlockSpec((tr, tr), lambda i: (0, 0))],
        out_specs=[row, row, pl.BlockSpec((SUBLANES, LANES), lambda i: (0, 0))],
        out_shape=[jax.ShapeDtypeStruct((t, LANES), jnp.int32), jax.ShapeDtypeStruct((t, LANES), F32),
                   jax.ShapeDtypeStruct((SUBLANES, LANES), F32)],
        scratch_shapes=[pltpu.VMEM((SUBLANES, LANES), F32)],
        compiler_params=_cparams(("arbitrary",), 32),
        name="route",
    )(logits, tri)


def _prep_up_kernel(w_ref, even_ref, odd_ref, g_ref, l_ref):
    w = w_ref[...].astype(BF16)
    g_ref[...] = _dot(w, even_ref[...]).astype(g_ref.dtype)
    l_ref[...] = _dot(w, odd_ref[...]).astype(l_ref.dtype)


def prepare_up_weights(w_up):
    d, f2 = w_up.shape[-2:]
    w3 = w_up.reshape(-1, d, f2)
    n = w3.shape[0]
    tn = min(512, f2)
    half = tn // 2
    even = np.zeros((tn, half), np.float32)
    odd = np.zeros((tn, half), np.float32)
    even[2 * np.arange(half), np.arange(half)] = 1.0
    odd[2 * np.arange(half) + 1, np.arange(half)] = 1.0
    sel = pl.BlockSpec((tn, half), lambda e, j: (0, 0))
    out_spec = pl.BlockSpec((None, d, half), lambda e, j: (e, 0, j))
    out_shape = jax.ShapeDtypeStruct((n, d, f2 // 2), BF16)
    return pl.pallas_call(
        _prep_up_kernel,
        grid=(n, f2 // tn),
        in_specs=[pl.BlockSpec((None, d, tn), lambda e, j: (e, 0, j)), sel, sel],
        out_specs=[out_spec, out_spec],
        out_shape=[out_shape, out_shape],
        compiler_params=_cparams(("arbitrary", "arbitrary"), 32),
        name="prepare_up_weights",
    )(w3, jnp.asarray(even, BF16), jnp.asarray(odd, BF16))


def _expert_kernel(be_ref, tok_cur_ref, tok_next_ref, h_ref, wg_ref, wl_ref, bg_ref, bl_ref, wd_ref, bd_ref,
                   y_ref, xbuf, act_s, sem, *, row_block):
    i = pl.program_id(0)
    n_blocks = pl.num_programs(0)
    slot = i % 2

    def block_wait(dst_slot):
        pltpu.make_async_copy(h_ref.at[pl.ds(0, row_block), :], xbuf.at[dst_slot], sem.at[dst_slot]).wait()

    def start_gather(tok_ref, dst_slot, rows, anchor):
        for r in rows:
            pltpu.make_async_copy(h_ref.at[pl.ds(tok_ref[0, r] + anchor, 1), :],
                                  xbuf.at[dst_slot, pl.ds(r, 1), :], sem.at[dst_slot]).start()

    @pl.when(i == 0)
    def _():
        start_gather(tok_cur_ref, 0, range(row_block), 0)

    block_wait(slot)
    x = xbuf[slot].astype(BF16)
    f = wg_ref.shape[1]
    up_chunk = min(UP_CHUNK, f)
    n_chunks = f // up_chunk
    rows_per_chunk = -(-row_block // n_chunks)
    def anchored_gather(result, rows):
        bits = lax.bitcast_convert_type(result[0:1, 0:1], jnp.int32)[0, 0]
        start_gather(tok_next_ref, 1 - slot, rows, (bits & 0x7FFFFFFF) >> 31)

    for c in range(n_chunks):
        cl = slice(c * up_chunk, (c + 1) * up_chunk)
        lo = min(c * rows_per_chunk, row_block)
        hi = min((c + 1) * rows_per_chunk, row_block)
        mid = (lo + hi) // 2
        glu = jnp.minimum(_dot(x, wg_ref[:, cl]) + bg_ref[:, cl], SWIGLU_LIMIT)
        anchored_gather(glu, range(lo, mid))
        lin = jnp.clip(_dot(x, wl_ref[:, cl]) + bl_ref[:, cl], -SWIGLU_LIMIT, SWIGLU_LIMIT)
        anchored_gather(lin, range(mid, hi))
        act_s[:, cl] = (glu * jax.nn.sigmoid(SWIGLU_ALPHA * glu) * (lin + 1.0)).astype(BF16)
    y_ref[...] = _dot(act_s[...], wd_ref[...]) + bd_ref[...]

    @pl.when(i == n_blocks - 1)
    def _():
        block_wait(1 - slot)


def expert_ffn(h2, row_tok, block_e, w_glu, w_lin, up_offset, b_glu, b_lin, w_down, b_down, row_block):
    d = h2.shape[1]
    n_rows = row_tok.shape[0]
    n_blocks = n_rows // row_block
    n_exp = w_down.shape[0]
    f = w_glu.shape[2]
    weight_buffers = pl.Buffered(2)
    tok3 = row_tok.reshape(n_blocks, 1, row_block)
    grid_spec = pltpu.PrefetchScalarGridSpec(
        num_scalar_prefetch=1,
        grid=(n_blocks,),
        in_specs=[pl.BlockSpec((None, 1, row_block), lambda i, be: (i, 0, 0), memory_space=pltpu.SMEM),
                  pl.BlockSpec((None, 1, row_block), lambda i, be: (jnp.minimum(i + 1, n_blocks - 1), 0, 0),
                               memory_space=pltpu.SMEM),
                  pl.BlockSpec(memory_space=pl.ANY),
                  pl.BlockSpec((None, d, f), lambda i, be: (be[i] + up_offset, 0, 0), pipeline_mode=weight_buffers),
                  pl.BlockSpec((None, d, f), lambda i, be: (be[i] + up_offset, 0, 0), pipeline_mode=weight_buffers),
                  pl.BlockSpec((None, 1, f), lambda i, be: (be[i], 0, 0)),
                  pl.BlockSpec((None, 1, f), lambda i, be: (be[i], 0, 0)),
                  pl.BlockSpec((None, f, d), lambda i, be: (be[i], 0, 0), pipeline_mode=weight_buffers),
                  pl.BlockSpec((None, 1, d), lambda i, be: (be[i], 0, 0))],
        out_specs=pl.BlockSpec((row_block, d), lambda i, be: (i, 0)),
        scratch_shapes=[pltpu.VMEM((2, row_block, d), F32), pltpu.VMEM((row_block, f), BF16),
                        pltpu.SemaphoreType.DMA((2,))])
    return pl.pallas_call(
        functools.partial(_expert_kernel, row_block=row_block),
        grid_spec=grid_spec,
        out_shape=jax.ShapeDtypeStruct((n_rows, d), F32),
        compiler_params=_cparams(("arbitrary",)),
        name="expert_ffn",
    )(block_e, tok3, tok3, h2, w_glu, w_lin, b_glu.reshape(n_exp, 1, f), b_lin.reshape(n_exp, 1, f),
      w_down, b_down.reshape(n_exp, 1, d))


def _combine_kernel(dest_cur_ref, dest_next_ref, ys_ref, w_ref, x_ref, gate_ref, g_ref, b_ref, sc_ref, sh_ref,
                    x2_ref, h_ref, buf, sem, *, tc, alpha):
    i = pl.program_id(0)
    n_tiles = pl.num_programs(0)
    slot = i % 2

    def start_gather(dest_ref, dst_slot):
        for tok in range(tc):
            for k in range(TOP_K):
                pltpu.make_async_copy(ys_ref.at[pl.ds(dest_ref[0, tok * TOP_K + k], 1), :],
                                      buf.at[dst_slot, k, pl.ds(tok, 1), :], sem.at[dst_slot]).start(priority=k % 2)

    def tile_wait(dst_slot):
        for k in range(TOP_K):
            pltpu.make_async_copy(ys_ref.at[pl.ds(0, tc), :], buf.at[dst_slot, k], sem.at[dst_slot]).wait()

    @pl.when(i == 0)
    def _():
        start_gather(dest_cur_ref, 0)

    tile_wait(slot)
    start_gather(dest_next_ref, 1 - slot)
    w = w_ref[...]
    y = w[:, 0:1] * buf[slot, 0]
    for k in range(1, TOP_K):
        y = y + w[:, k:k + 1] * buf[slot, k]
    x2 = _ln(alpha * x_ref[...] + gate_ref[...] * y) * g_ref[...] + b_ref[...]
    x2_ref[...] = x2
    h_ref[...] = (_ln(x2) * (1.0 + sc_ref[...]) + sh_ref[...]).astype(h_ref.dtype)

    @pl.when(i == n_tiles - 1)
    def _():
        tile_wait(1 - slot)


def moe_combine(ys, dest, top_w, x1, mod_l, mod_next, ln_g, ln_b, seq, alpha):
    t, d = x1.shape
    tc = min(128, seq)
    bpb = seq // tc
    row = pl.BlockSpec((tc, d), lambda i: (i, 0))
    vec = pl.BlockSpec((1, d), lambda i: (0, 0))
    n_tiles = t // tc
    dest3 = dest.reshape(n_tiles, 1, tc * TOP_K)
    return pl.pallas_call(
        functools.partial(_combine_kernel, tc=tc, alpha=alpha),
        grid=(n_tiles,),
        in_specs=[pl.BlockSpec((None, 1, tc * TOP_K), lambda i: (i, 0, 0), memory_space=pltpu.SMEM),
                  pl.BlockSpec((None, 1, tc * TOP_K), lambda i: (jnp.minimum(i + 1, n_tiles - 1), 0, 0),
                               memory_space=pltpu.SMEM),
                  pl.BlockSpec(memory_space=pl.ANY),
                  pl.BlockSpec((tc, LANES), lambda i: (i, 0)),
                  row, _mod_spec(5, d, bpb), vec, vec, _mod_spec(1, d, bpb), _mod_spec(0, d, bpb)],
        out_specs=[row, row],
        out_shape=[jax.ShapeDtypeStruct((t, d), F32), jax.ShapeDtypeStruct((t, d), BF16)],
        scratch_shapes=[pltpu.VMEM((2, TOP_K, tc, d), F32), pltpu.SemaphoreType.DMA((2,))],
        compiler_params=_cparams(("arbitrary",), 32),
        name="moe_combine",
    )(dest3, dest3, ys, top_w, x1, mod_l, ln_g.reshape(1, d), ln_b.reshape(1, d), mod_next, mod_next)


def moe_ffn(h2, logits, w_glu, w_lin, up_offset, b_glu, b_lin, w_down, b_down):
    t, d = h2.shape
    n_exp = w_down.shape[0]
    row_block = 256
    idx_rank, top_w, counts = route(logits, n_exp)
    top_idx = idx_rank[:, :TOP_K]
    rank = idx_rank[:, TOP_K:2 * TOP_K]
    counts = counts[0, :n_exp].astype(jnp.int32)
    padded = (counts + row_block - 1) // row_block * row_block
    padded_end = jnp.cumsum(padded)
    padded_start = padded_end - padded
    dest = padded_start[top_idx] + rank
    n_rows = t * TOP_K + n_exp * row_block
    tok_ids = jnp.broadcast_to(jnp.arange(t, dtype=jnp.int32)[:, None], (t, TOP_K))
    row_tok = jnp.zeros((n_rows,), jnp.int32).at[dest.reshape(-1)].set(tok_ids.reshape(-1), unique_indices=True)
    n_blocks = n_rows // row_block
    block_start = jnp.arange(n_blocks, dtype=jnp.int32) * row_block
    block_e = jnp.minimum(jnp.sum(padded_end[None, :] <= block_start[:, None], axis=1), n_exp - 1).astype(jnp.int32)
    ys = expert_ffn(h2, row_tok, block_e, w_glu, w_lin, up_offset, b_glu, b_lin, w_down, b_down, row_block)
    return ys, dest.astype(jnp.int32), top_w


def kernel(x, c, w_ada, b_ada, w_in, b_forget, w_gate, b_gate, w_proj_sb, w_proj_dil, w_proj_fox, w_out,
           ln1_g, ln1_b, w_router, b_router, w_up, b_up, w_down, b_down, ln2_g, ln2_b):
    batch, seq, d = x.shape
    depth = w_ada.shape[0]
    t = batch * seq
    alpha = (2.0 * depth) ** 0.25
    rope_tabs = rope_pair_tables(seq)
    mod = adaln_mod(c, w_ada, b_ada)
    x2d = x.reshape(t, d)
    h = ln_mod(x2d, mod[0], seq, comp_shift=0, comp_scale=1)
    n_exp = w_up.shape[1]
    w_glu, w_lin = prepare_up_weights(w_up)
    for l in range(depth):
        w_in_l = w_in[l]
        qkv = in_projection(h, w_in_l[:, :N_QKV].astype(BF16), rope_tabs, seq)
        q_feat, k_feat = forget_features(h, w_in_l[:, N_QKV:], b_forget[l], batch, seq)
        o_sb = stick_breaking_attention(qkv, batch, seq)
        o_dl = dilated_window_attention(qkv, batch, seq)
        o_fx = forgetting_attention(qkv, q_feat, k_feat, batch, seq)
        merged = gated_merge(h, w_gate[l].astype(BF16), b_gate[l], (o_sb, o_dl, o_fx),
                             (w_proj_sb[l].astype(BF16), w_proj_dil[l].astype(BF16),
                              w_proj_fox[l].astype(BF16)))
        x1, h2, logits = out_projection(merged, w_out[l].astype(BF16), x2d, mod[l], ln1_g[l], ln1_b[l],
                                        w_router[l], b_router[l], seq, alpha)
        ys, dest, top_w = moe_ffn(h2, logits, w_glu, w_lin, l * n_exp,
                                  b_up[l][:, 0::2], b_up[l][:, 1::2], w_down[l].astype(BF16), b_down[l])
        mod_next = mod[min(l + 1, depth - 1)]
        x2d, h = moe_combine(ys, dest, top_w, x1, mod[l], mod_next, ln2_g[l], ln2_b[l], seq, alpha)
    return x2d.reshape(batch, seq, d)
```

```python
import functools

import numpy as np
import jax
import jax.numpy as jnp
from jax import lax
from jax.experimental import pallas as pl
from jax.experimental.pallas import tpu as pltpu

F32 = jnp.float32
BF16 = jnp.bfloat16

HEAD_DIM = 64
H_SB = 8
DIL_CONFIGS = ((128, 1), (512, 4), (2048, 16))
H_DIL_PER_GROUP = 4
H_FOX = 12
ROPE_THETA = 500000.0
ROPE_DIMS = HEAD_DIM // 4
TOP_K = 4
SWIGLU_ALPHA = 1.702
SWIGLU_LIMIT = 7.0
LN_EPS = 1e-5
QK_SCALE = HEAD_DIM ** -0.5

LANES = 128
SUBLANES = 8
VMEM_BUDGET_MB = 56

N_DIL_GROUPS = len(DIL_CONFIGS)
W_SB = H_SB * HEAD_DIM
W_DIL = N_DIL_GROUPS * H_DIL_PER_GROUP * HEAD_DIM
W_DIL_OUT = H_DIL_PER_GROUP * HEAD_DIM
W_FOX = H_FOX * HEAD_DIM
N_QKV = 3 * (W_SB + W_DIL + W_FOX)
COL_SB = (0, W_SB // LANES, 2 * W_SB // LANES)
_DL0 = 3 * W_SB // LANES
COL_DL = (_DL0, _DL0 + W_DIL // LANES, _DL0 + 2 * W_DIL // LANES)
_FX0 = _DL0 + 3 * W_DIL // LANES
COL_FX = (_FX0, _FX0 + W_FOX // LANES, _FX0 + 2 * W_FOX // LANES)
BAND = 128
NEG_BIG = -1e30
EXP_UNDERFLOW = -104.0


def _cparams(semantics, vmem_mb=VMEM_BUDGET_MB, fuse_inputs=None):
    return pltpu.CompilerParams(dimension_semantics=semantics, vmem_limit_bytes=vmem_mb * 2 ** 20,
                                allow_input_fusion=fuse_inputs)


def _dot(a, b):
    return jnp.dot(a, b, preferred_element_type=F32)


def _dot_t(a, b):
    return lax.dot_general(a, b, (((1,), (1,)), ((), ())), preferred_element_type=F32)


def _ln(x):
    mu = jnp.mean(x, axis=-1, keepdims=True)
    xc = x - mu
    var = jnp.mean(xc * xc, axis=-1, keepdims=True)
    return xc * lax.rsqrt(var + LN_EPS)


def _softplus_neg_abs(z):
    return jnp.log1p(jnp.exp(-jnp.abs(z)))


def _split3(x):
    hi = x.astype(BF16)
    r1 = x - hi.astype(F32)
    mid = r1.astype(BF16)
    lo = (r1 - mid.astype(F32)).astype(BF16)
    return hi, mid, lo


def _mod_kernel(c_ref, w_ref, b_ref, o_ref):
    c = c_ref[...]
    act = (c * jax.nn.sigmoid(c)).astype(BF16)
    o_ref[...] = _dot(act, w_ref[...].astype(BF16)) + b_ref[...]


def adaln_mod(c, w_ada, b_ada):
    n_layers, d, n = w_ada.shape
    b = c.shape[0]
    assert b <= SUBLANES
    tn = 512
    c_pad = jnp.zeros((SUBLANES, d), F32).at[:b].set(c)
    out = pl.pallas_call(
        _mod_kernel,
        grid=(n_layers, n // tn),
        in_specs=[pl.BlockSpec((SUBLANES, d), lambda l, j: (0, 0)),
                  pl.BlockSpec((None, d, tn), lambda l, j: (l, 0, j)),
                  pl.BlockSpec((None, 1, tn), lambda l, j: (l, 0, j))],
        out_specs=pl.BlockSpec((None, SUBLANES, tn), lambda l, j: (l, 0, j)),
        out_shape=jax.ShapeDtypeStruct((n_layers, SUBLANES, n), F32),
        compiler_params=_cparams(("arbitrary", "arbitrary"), 32),
        name="adaln_mod",
    )(c_pad, w_ada, b_ada.reshape(n_layers, 1, n))
    return out[:, :b].reshape(n_layers, b, 6, d).transpose(0, 2, 1, 3)[:, :, :, None, :]


def _mod_spec(comp, d, rows_per_batch_blocks):
    return pl.BlockSpec((None, None, 1, d), lambda i, *_: (comp, i // rows_per_batch_blocks, 0, 0))


def _ln_mod_kernel(x_ref, sc_ref, sh_ref, h_ref):
    h_ref[...] = (_ln(x_ref[...]) * (1.0 + sc_ref[...]) + sh_ref[...]).astype(h_ref.dtype)


def ln_mod(x2d, mod_l, seq, comp_shift, comp_scale):
    t, d = x2d.shape
    ts = min(512, seq)
    bpb = seq // ts
    return pl.pallas_call(
        _ln_mod_kernel,
        grid=(t // ts,),
        in_specs=[pl.BlockSpec((ts, d), lambda i: (i, 0)),
                  _mod_spec(comp_scale, d, bpb), _mod_spec(comp_shift, d, bpb)],
        out_specs=pl.BlockSpec((ts, d), lambda i: (i, 0)),
        out_shape=jax.ShapeDtypeStruct((t, d), BF16),
        compiler_params=_cparams(("arbitrary",), 32),
        name="ln_mod",
    )(x2d, mod_l, mod_l)


def _inproj_kernel(h_ref, w_ref, cos_ref, s1_ref, s2_ref, o_ref, *, rope_lo, rope_hi, tn):
    j = pl.program_id(1)
    acc = _dot(h_ref[...], w_ref[...])
    is_rope = jnp.logical_and(j >= rope_lo, j < rope_hi)

    @pl.when(is_rope)
    def _():
        c, s1, s2 = cos_ref[...], s1_ref[...], s2_ref[...]
        for blk in range(tn // LANES):
            a = acc[:, blk * LANES:(blk + 1) * LANES]
            r = a * c + pltpu.roll(a, LANES - ROPE_DIMS // 2, 1) * s1 + pltpu.roll(a, ROPE_DIMS // 2, 1) * s2
            o_ref[:, blk * LANES:(blk + 1) * LANES] = r.astype(o_ref.dtype)

    @pl.when(jnp.logical_not(is_rope))
    def _():
        o_ref[...] = acc.astype(o_ref.dtype)


def rope_pair_tables(seq):
    pos = jnp.arange(seq, dtype=F32)
    inv = ROPE_THETA ** (-jnp.arange(0, ROPE_DIMS, 2, dtype=F32) / ROPE_DIMS)
    ang = pos[:, None] * inv[None, :]
    cos, sin = jnp.cos(ang), jnp.sin(ang)
    half = ROPE_DIMS // 2
    head_c = jnp.concatenate([cos, cos, jnp.ones((seq, HEAD_DIM - ROPE_DIMS), F32)], axis=1)
    head_s1 = jnp.concatenate([-sin, jnp.zeros((seq, HEAD_DIM - half), F32)], axis=1)
    head_s2 = jnp.concatenate([jnp.zeros((seq, half), F32), sin,
                               jnp.zeros((seq, HEAD_DIM - ROPE_DIMS), F32)], axis=1)
    two = lambda a: jnp.concatenate([a, a], axis=1)
    return two(head_c), two(head_s1), two(head_s2)


def in_projection(h, w_qkv, rope_tabs, seq):
    t, d = h.shape
    n = w_qkv.shape[1]
    tm = min(1024, seq)
    tn = 1536
    assert seq % tm == 0 and n % tn == 0
    rope_lo, rope_hi = COL_DL[0] * LANES, COL_DL[2] * LANES
    assert rope_lo % tn == 0 and rope_hi % tn == 0
    spb = seq // tm
    tab_spec = pl.BlockSpec((tm, LANES), lambda i, j: (i % spb, 0))
    return pl.pallas_call(
        functools.partial(_inproj_kernel, rope_lo=rope_lo // tn, rope_hi=rope_hi // tn, tn=tn),
        grid=(t // tm, n // tn),
        in_specs=[pl.BlockSpec((tm, d), lambda i, j: (i, 0)),
                  pl.BlockSpec((d, tn), lambda i, j: (0, j)),
                  tab_spec, tab_spec, tab_spec],
        out_specs=pl.BlockSpec((tm, tn), lambda i, j: (i, j)),
        out_shape=jax.ShapeDtypeStruct((t, n), BF16),
        compiler_params=_cparams(("arbitrary", "arbitrary"), 48, fuse_inputs=[False, True, False, False, False]),
        name="in_projection",
    )(h, w_qkv, *rope_tabs)


FEAT_PER_HEAD = 6


def _forget_feature_maps():
    width = (H_FOX // 2) * LANES
    pq = np.zeros((3 * LANES, width), np.float32)
    pk = np.zeros((3 * LANES, width), np.float32)
    cq = np.zeros((1, width), np.float32)
    ck = np.zeros((1, width), np.float32)
    for head in range(H_FOX):
        base = (head // 2) * LANES + (head % 2) * FEAT_PER_HEAD
        for piece in range(3):
            pq[piece * LANES + head, base + piece] = 1.0
            pk[piece * LANES + head, base + 3 + piece] = -1.0
        cq[0, base + 3:base + 6] = 1.0
        ck[0, base:base + 3] = 1.0
    return pq, pk, cq, ck


def _forget_kernel(h_ref, w_ref, b_ref, tri_ref, pq_ref, pk_ref, cq_ref, ck_ref, qf_ref, kf_ref, carry_ref):
    @pl.when(pl.program_id(1) == 0)
    def _():
        carry_ref[...] = jnp.zeros_like(carry_ref)

    f = _dot(h_ref[...], w_ref[...]) + b_ref[...]
    log_f = jnp.minimum(f, 0.0) - _softplus_neg_abs(f)
    tri = tri_ref[...]
    hi, mid, lo = _split3(log_f)
    cum = _dot(tri, hi) + _dot(tri, mid) + _dot(tri, lo) + carry_ref[0:1, :]
    carry_ref[...] = jnp.broadcast_to(cum[-1:, :], carry_ref.shape)
    pieces = jnp.concatenate(_split3(cum), axis=1)
    qf_ref[...] = (_dot(pieces, pq_ref[...]) + cq_ref[...]).astype(qf_ref.dtype)
    kf_ref[...] = (_dot(pieces, pk_ref[...]) + ck_ref[...]).astype(kf_ref.dtype)


def forget_features(h, w_f, b_f, batch, seq):
    t, d = h.shape
    ts = min(256, seq)
    nsb = seq // ts
    w_pad = jnp.zeros((d, LANES), BF16).at[:, :H_FOX].set(w_f.astype(BF16))
    b_pad = jnp.zeros((1, LANES), F32).at[0, :H_FOX].set(b_f)
    tri = (np.arange(ts)[:, None] >= np.arange(ts)[None, :]).astype(np.float32)
    pq, pk, cq, ck = _forget_feature_maps()
    width = pq.shape[1]
    const = lambda shape: pl.BlockSpec(shape, lambda b, s: (0, 0))
    out_spec = pl.BlockSpec((ts, width), lambda b, s: (b * nsb + s, 0))
    out_shape = jax.ShapeDtypeStruct((t, width), BF16)
    return pl.pallas_call(
        _forget_kernel,
        grid=(batch, nsb),
        in_specs=[pl.BlockSpec((ts, d), lambda b, s: (b * nsb + s, 0)),
                  const((d, LANES)), const((1, LANES)), const((ts, ts)),
                  const(pq.shape), const(pk.shape), const(cq.shape), const(ck.shape)],
        out_specs=[out_spec, out_spec],
        out_shape=[out_shape, out_shape],
        scratch_shapes=[pltpu.VMEM((SUBLANES, LANES), F32)],
        compiler_params=_cparams(("arbitrary", "arbitrary"), 32),
        name="forget_features",
    )(h, w_pad, b_pad, jnp.asarray(tri, BF16), jnp.asarray(pq, BF16), jnp.asarray(pk, BF16),
      jnp.asarray(cq), jnp.asarray(ck))


def _pair_masks(tq):
    lane = lax.broadcasted_iota(jnp.int32, (tq, LANES), 1)
    return lane < HEAD_DIM


KEY_SUB = 128
QRY_SUB = 128
UP_CHUNK = 512
SUFFIX_SUB = 256


def _diag_tile_kind(ks, qs, key_sub, qry_sub, strict, key_off):
    if key_off is None:
        return 'full'
    k_lo, k_hi = key_off + ks * key_sub, key_off + ks * key_sub + key_sub - 1
    q_lo, q_hi = qs * qry_sub, qs * qry_sub + qry_sub - 1
    if strict:
        if k_lo >= q_hi:
            return 'dead'
        return 'full' if k_hi < q_lo else 'partial'
    if k_lo > q_hi:
        return 'dead'
    return 'full' if k_hi <= q_lo else 'partial'


def _tile_allowed(ks, qs, key_sub, qry_sub, strict, key_off):
    kpos = key_off + ks * key_sub + lax.broadcasted_iota(jnp.int32, (key_sub, qry_sub), 0)
    qpos = qs * qry_sub + lax.broadcasted_iota(jnp.int32, (key_sub, qry_sub), 1)
    return kpos < qpos if strict else kpos <= qpos


def _pair_transpose_in(qkv3, col, width):
    return lax.slice_in_dim(qkv3, col * LANES, col * LANES + width, axis=2).transpose(0, 2, 1)


def _sb_kernel(q_ref, k_ref, vt_ref, later_ref, o_ref,
               qx_s, acc_s, run_s, first_s, z_s, hi_s, lo_s, btw_s, w_s, *, tq, tk, key_sub, qry_sub):
    q_blk = pl.program_id(2)
    head_a = _pair_masks(tq)
    q = q_ref[...] * QK_SCALE
    qx_s[0] = jnp.where(head_a, q, 0).astype(BF16)
    qx_s[1] = jnp.where(head_a, 0, q).astype(BF16)
    acc_s[...] = jnp.zeros_like(acc_s)
    run_s[...] = jnp.zeros_like(run_s)

    def step(key_off, kv_blk):
        k_ref_blk = k_ref.at[pl.ds(pl.multiple_of(kv_blk * tk, tk), tk), :]
        vt_blk = vt_ref.at[kv_blk]
        _sb_block(key_off, k_ref_blk, vt_blk)

    def _sb_block(key_off, k_ref, vt_ref):
        suf = later_ref.shape[0]
        n_suf = tk // suf
        per_suf = suf // key_sub
        tiles = [(ks, qs) for ks in range(tk // key_sub) for qs in range(tq // qry_sub)]
        kind_of = lambda ks, qs: _diag_tile_kind(ks, qs, key_sub, qry_sub, True, key_off)
        k = k_ref[...]
        for head in range(2):
            z_s[head] = _dot_t(k, qx_s[head])
        for head in range(2):
            for ks, qs in tiles:
                kl = slice(ks * key_sub, (ks + 1) * key_sub)
                ql = slice(qs * qry_sub, (qs + 1) * qry_sub)
                first_row = head * n_suf + ks // per_suf
                kind = kind_of(ks, qs)
                if kind == 'dead':
                    hi_s[head, kl, ql] = jnp.zeros((key_sub, qry_sub), BF16)
                    lo_s[head, kl, ql] = jnp.zeros((key_sub, qry_sub), BF16)
                    if ks % per_suf == 0:
                        first_s[first_row:first_row + 1, ql] = jnp.zeros((1, qry_sub), F32)
                    continue
                z = z_s[head, kl, ql]
                log_stop = jnp.minimum(z, 0.0) - jnp.log(1.0 + jnp.exp(-jnp.abs(z)))
                log_cont = log_stop - z
                if kind == 'partial':
                    log_cont = jnp.where(_tile_allowed(ks, qs, key_sub, qry_sub, True, key_off), log_cont, 0.0)
                z_s[head, kl, ql] = log_stop
                hi = log_cont.astype(BF16)
                hi_s[head, kl, ql] = hi
                lo_s[head, kl, ql] = (log_cont - hi.astype(F32)).astype(BF16)
                if ks % per_suf == 0:
                    first_s[first_row:first_row + 1, ql] = log_cont[0:1, :]
        later = later_ref[...]
        for head in range(2):
            for blk in range(n_suf):
                bl = slice(blk * suf, (blk + 1) * suf)
                btw_s[head, bl, :] = _dot(later, hi_s[head, bl, :]) + _dot(later, lo_s[head, bl, :])

        def block_sum(head, blk, ql):
            row = head * n_suf + blk
            return btw_s[head, blk * suf:blk * suf + 1, ql] + first_s[row:row + 1, ql]

        for head in range(2):
            for ks, qs in tiles:
                kl = slice(ks * key_sub, (ks + 1) * key_sub)
                ql = slice(qs * qry_sub, (qs + 1) * qry_sub)
                kind = kind_of(ks, qs)
                if kind == 'dead':
                    w_s[head, kl, ql] = jnp.zeros((key_sub, qry_sub), BF16)
                    continue
                after = run_s[head:head + 1, ql]
                for blk in range(ks // per_suf + 1, n_suf):
                    after = after + block_sum(head, blk, ql)
                w = jnp.exp(z_s[head, kl, ql] + btw_s[head, kl, ql] + after)
                if kind == 'partial':
                    w = jnp.where(_tile_allowed(ks, qs, key_sub, qry_sub, True, key_off), w, 0.0)
                w_s[head, kl, ql] = w.astype(BF16)
        for head in range(2):
            rows = slice(head * HEAD_DIM, (head + 1) * HEAD_DIM)
            acc_s[rows, :] += _dot(vt_ref[rows, :], w_s[head])
            total = run_s[head:head + 1, :]
            for blk in range(n_suf):
                total = total + block_sum(head, blk, slice(None))
            run_s[head:head + 1, :] = total

    def all_weights_zero():
        return jnp.max(run_s[0:2, :]) < EXP_UNDERFLOW

    step(0, q_blk)

    def more(carry):
        kv_blk, dead = carry
        return jnp.logical_and(kv_blk >= 0, jnp.logical_not(dead))

    def visit(carry):
        kv_blk, _ = carry
        step(None, kv_blk)
        return kv_blk - 1, all_weights_zero()

    lax.while_loop(more, visit, (q_blk - 1, all_weights_zero()))
    o_ref[...] = acc_s[...].astype(o_ref.dtype)


def stick_breaking_attention(qkv, batch, seq):
    tq = tk = min(512, seq)
    key_sub, qry_sub = min(KEY_SUB, tk), min(QRY_SUB, tq)
    n_pairs = W_SB // LANES
    n_kv = seq // tk
    qkv3 = qkv.reshape(batch, seq, qkv.shape[-1])
    qc, kc, vc = COL_SB
    v_t = _pair_transpose_in(qkv3, vc, W_SB).reshape(batch, n_pairs, LANES, n_kv, tk).transpose(0, 1, 3, 2, 4)
    suf = min(SUFFIX_SUB, tk)
    assert 2 * (tk // suf) <= SUBLANES and suf % key_sub == 0
    later = jnp.asarray((np.arange(suf)[None, :] > np.arange(suf)[:, None]).astype(np.float32), BF16)
    block_f32 = pltpu.VMEM((2, tk, tq), F32)
    block_bf16 = pltpu.VMEM((2, tk, tq), BF16)
    row_stats = pltpu.VMEM((SUBLANES, tq), F32)
    out_t = pl.pallas_call(
        functools.partial(_sb_kernel, tq=tq, tk=tk, key_sub=key_sub, qry_sub=qry_sub),
        grid=(batch, n_pairs, seq // tq),
        in_specs=[pl.BlockSpec((None, tq, LANES), lambda b, p, i: (b, i, qc + p)),
                  pl.BlockSpec((None, seq, LANES), lambda b, p, i: (b, 0, kc + p)),
                  pl.BlockSpec((None, None, n_kv, LANES, tk), lambda b, p, i: (b, p, 0, 0, 0)),
                  pl.BlockSpec((suf, suf), lambda b, p, i: (0, 0))],
        out_specs=pl.BlockSpec((None, LANES, tq), lambda b, p, i: (b, p, i)),
        scratch_shapes=[pltpu.VMEM((2, tq, LANES), BF16), pltpu.VMEM((LANES, tq), F32),
                        row_stats, row_stats, block_f32, block_bf16, block_bf16, block_f32, block_bf16],
        out_shape=jax.ShapeDtypeStruct((batch, W_SB, seq), BF16),
        compiler_params=_cparams(("arbitrary", "arbitrary", "arbitrary"), 32),
        name="stick_breaking_attention",
    )(qkv3, qkv3, v_t, later)
    return out_t.transpose(0, 2, 1).reshape(batch * seq, W_SB)


def _fox_kernel(q_ref, qf_ref, k_ref, kf_ref, vt_ref, o_ref,
                qx_s, acc_s, m_s, l_s, alpha_s, z_s, p_s, *, tq, tk, key_sub, qry_sub):
    q_blk = pl.program_id(2)
    lane = lax.broadcasted_iota(jnp.int32, (tq, LANES), 1)
    q = q_ref[...] * QK_SCALE
    qf = qf_ref[...]
    for head in range(2):
        in_head = (lane >= head * HEAD_DIM) & (lane < (head + 1) * HEAD_DIM)
        in_feat = (lane >= head * FEAT_PER_HEAD) & (lane < (head + 1) * FEAT_PER_HEAD)
        qx_s[head] = jnp.concatenate([jnp.where(in_head, q, 0).astype(BF16),
                                      jnp.where(in_feat, qf, 0).astype(BF16)], axis=1)
    acc_s[...] = jnp.zeros_like(acc_s)
    l_s[...] = jnp.zeros_like(l_s)
    m_s[...] = jnp.full_like(m_s, NEG_BIG)

    def step(key_off, kv_blk):
        rows = pl.ds(pl.multiple_of(kv_blk * tk, tk), tk)
        _fox_block(key_off, k_ref.at[rows, :], kf_ref.at[rows, :], vt_ref.at[kv_blk])

    def _fox_block(key_off, k_ref, kf_ref, vt_ref):
        kind_of = lambda ks, qs: _diag_tile_kind(ks, qs, key_sub, qry_sub, False, key_off)
        kx = jnp.concatenate([k_ref[...], kf_ref[...]], axis=1)
        for head in range(2):
            z_s[head] = _dot_t(kx, qx_s[head])
        for head in range(2):
            for qs in range(tq // qry_sub):
                ql = slice(qs * qry_sub, (qs + 1) * qry_sub)

                def score(ks):
                    z = z_s[head, ks * key_sub:(ks + 1) * key_sub, ql]
                    if kind_of(ks, qs) == 'partial':
                        z = jnp.where(_tile_allowed(ks, qs, key_sub, qry_sub, False, key_off), z, NEG_BIG)
                    return z

                live = [ks for ks in range(tk // key_sub) if kind_of(ks, qs) != 'dead']
                m_old = m_s[head:head + 1, ql]
                m_new = m_old
                for ks in live:
                    m_new = jnp.maximum(m_new, jnp.max(score(ks), axis=0, keepdims=True))
                alpha = jnp.exp(m_old - m_new)
                total = alpha * l_s[head:head + 1, ql]
                for ks in range(tk // key_sub):
                    kl = slice(ks * key_sub, (ks + 1) * key_sub)
                    if ks not in live:
                        p_s[head, kl, ql] = jnp.zeros((key_sub, qry_sub), BF16)
                        continue
                    prob = jnp.exp(score(ks) - m_new)
                    total = total + jnp.sum(prob, axis=0, keepdims=True)
                    p_s[head, kl, ql] = prob.astype(BF16)
                m_s[head:head + 1, ql] = m_new
                l_s[head:head + 1, ql] = total
                alpha_s[head:head + 1, ql] = alpha
        for head in range(2):
            rows = slice(head * HEAD_DIM, (head + 1) * HEAD_DIM)
            acc_s[rows, :] = alpha_s[head:head + 1, :] * acc_s[rows, :] + _dot(vt_ref[rows, :], p_s[head])

    per_q = tq // tk
    first_diag = q_blk * per_q

    def visit(kv_blk, carry):
        step(None, kv_blk)
        return carry

    lax.fori_loop(0, first_diag, visit, 0)
    for variant in range(per_q):
        step(variant * tk, first_diag + variant)
    for head in range(2):
        rows = slice(head * HEAD_DIM, (head + 1) * HEAD_DIM)
        o_ref[rows, :] = (acc_s[rows, :] / l_s[head:head + 1, :]).astype(o_ref.dtype)


def forgetting_attention(qkv, q_feat, k_feat, batch, seq):
    tk = min(512, seq)
    tq = min(1024, seq)
    key_sub, qry_sub = min(KEY_SUB, tk), min(QRY_SUB, tq)
    assert tq % tk == 0
    n_pairs = W_FOX // LANES
    n_kv = seq // tk
    qkv3 = qkv.reshape(batch, seq, qkv.shape[-1])
    qf3 = q_feat.reshape(batch, seq, n_pairs * LANES)
    kf3 = k_feat.reshape(batch, seq, n_pairs * LANES)
    qc, kc, vc = COL_FX
    v_t = _pair_transpose_in(qkv3, vc, W_FOX).reshape(batch, n_pairs, LANES, n_kv, tk).transpose(0, 1, 3, 2, 4)
    out_t = pl.pallas_call(
        functools.partial(_fox_kernel, tq=tq, tk=tk, key_sub=key_sub, qry_sub=qry_sub),
        grid=(batch, n_pairs, seq // tq),
        in_specs=[pl.BlockSpec((None, tq, LANES), lambda b, p, i: (b, i, qc + p)),
                  pl.BlockSpec((None, tq, LANES), lambda b, p, i: (b, i, p)),
                  pl.BlockSpec((None, seq, LANES), lambda b, p, i: (b, 0, kc + p)),
                  pl.BlockSpec((None, seq, LANES), lambda b, p, i: (b, 0, p)),
                  pl.BlockSpec((None, None, n_kv, LANES, tk), lambda b, p, i: (b, p, 0, 0, 0))],
        out_specs=pl.BlockSpec((None, LANES, tq), lambda b, p, i: (b, p, i)),
        scratch_shapes=[pltpu.VMEM((2, tq, 2 * LANES), BF16), pltpu.VMEM((LANES, tq), F32),
                        pltpu.VMEM((SUBLANES, tq), F32), pltpu.VMEM((SUBLANES, tq), F32),
                        pltpu.VMEM((SUBLANES, tq), F32), pltpu.VMEM((2, tk, tq), F32),
                        pltpu.VMEM((2, tk, tq), BF16)],
        out_shape=jax.ShapeDtypeStruct((batch, W_FOX, seq), BF16),
        compiler_params=_cparams(("arbitrary", "arbitrary", "arbitrary"), 40),
        name="forgetting_attention",
    )(qkv3, qf3, qkv3, kf3, v_t)
    return out_t.transpose(0, 2, 1).reshape(batch * seq, W_FOX)


def _band_kernel(q_ref, kc_ref, vc_ref, kp_ref, vp_ref, o_ref, lse_ref, *, tq):
    i = pl.program_id(1)
    head_a = _pair_masks(tq)
    diff = lax.broadcasted_iota(jnp.int32, (tq, tq), 0) - lax.broadcasted_iota(jnp.int32, (tq, tq), 1)
    mask_c = jnp.logical_and(diff >= 0, diff <= BAND)
    rp = lax.broadcasted_iota(jnp.int32, (tq, BAND), 0)
    cp = lax.broadcasted_iota(jnp.int32, (tq, BAND), 1)
    mask_p = jnp.logical_and(cp >= rp, i > 0)
    for pair in range(q_ref.shape[1] // LANES):
        pl_ = slice(pair * LANES, (pair + 1) * LANES)
        q = q_ref[:, pl_] * QK_SCALE
        kc, vc, kp, vp = kc_ref[:, pl_], vc_ref[:, pl_], kp_ref[:, pl_], vp_ref[:, pl_]
        outs, lses = [], []
        for sel in (head_a, jnp.logical_not(head_a)):
            qh = jnp.where(sel, q, 0).astype(BF16)
            sc = jnp.where(mask_c, _dot_t(qh, kc), NEG_BIG)
            sp = jnp.where(mask_p, _dot_t(qh, kp), NEG_BIG)
            m = jnp.maximum(jnp.max(sc, axis=1, keepdims=True), jnp.max(sp, axis=1, keepdims=True))
            ec = jnp.exp(sc - m)
            ep = jnp.exp(sp - m)
            den = jnp.sum(ec, axis=1, keepdims=True) + jnp.sum(ep, axis=1, keepdims=True)
            outs.append((_dot(ec.astype(BF16), vc) + _dot(ep.astype(BF16), vp)) / den)
            lses.append(m + jnp.log(den))
        o_ref[:, pl_] = jnp.where(head_a, outs[0], outs[1])
        lse_ref[:, pl_] = jnp.where(head_a, lses[0], lses[1])


def _band_attention(q, k, v):
    nb, u, w = q.shape
    tq = min(256, u)
    assert u % tq == 0 and tq % BAND == 0
    sub = tq // BAND
    cur = pl.BlockSpec((None, tq, w), lambda n, i: (n, i, 0))
    prev = pl.BlockSpec((None, BAND, w), lambda n, i: (n, jnp.maximum(i * sub - 1, 0), 0))
    shp = jax.ShapeDtypeStruct((nb, u, w), F32)
    return pl.pallas_call(
        functools.partial(_band_kernel, tq=tq),
        grid=(nb, u // tq),
        in_specs=[cur, cur, cur, prev, prev],
        out_specs=[cur, cur],
        out_shape=[shp, shp],
        compiler_params=_cparams(("arbitrary", "arbitrary"), 32),
        name="band_attention",
    )(q, k, v, k, v)


def _dil_mix_kernel(o0, o1, o2, l0, l1, l2, out_ref):
    a, b, c = l0[...], l1[...], l2[...]
    m = jnp.maximum(jnp.maximum(a, b), c)
    ea, eb, ec = jnp.exp(a - m), jnp.exp(b - m), jnp.exp(c - m)
    tot = ea + eb + ec
    out_ref[...] = ((o0[...] * ea + o1[...] * eb + o2[...] * ec) / tot).astype(out_ref.dtype)


def dilated_window_attention(qkv, batch, seq):
    t = batch * seq
    qkv3 = qkv.reshape(batch, seq, qkv.shape[-1])
    gw = W_DIL_OUT
    outs, lses = [], []
    for g, (window, dil) in enumerate(DIL_CONFIGS):
        assert window // dil == BAND and seq % (dil * BAND) == 0
        u = seq // dil

        def split(col):
            a = lax.slice_in_dim(qkv3, col * LANES + g * gw, col * LANES + (g + 1) * gw, axis=2)
            return a.reshape(batch, u, dil, gw).transpose(0, 2, 1, 3).reshape(batch * dil, u, gw)

        o, lse = _band_attention(split(COL_DL[0]), split(COL_DL[1]), split(COL_DL[2]))
        merge = lambda a: a.reshape(batch, dil, u, gw).transpose(0, 2, 1, 3).reshape(t, gw)
        outs.append(merge(o))
        lses.append(merge(lse))
    tm = min(1024, t)
    spec = pl.BlockSpec((tm, gw), lambda i: (i, 0))
    return pl.pallas_call(
        _dil_mix_kernel,
        grid=(t // tm,),
        in_specs=[spec] * 6,
        out_specs=spec,
        out_shape=jax.ShapeDtypeStruct((t, gw), BF16),
        compiler_params=_cparams(("arbitrary",), 32),
        name="dilated_mix",
    )(*outs, *lses)


def _merge_kernel(h_ref, wg0, wg1, wg2, bg0, bg1, bg2, o0, o1, o2, wp0, wp1, wp2, out_ref):
    h = h_ref[...]

    def branch(wg, bg, o, wp):
        return jax.nn.sigmoid(_dot(h, wg[...]) + bg[...]) * _dot(o[...], wp[...])

    merged = branch(wg0, bg0, o0, wp0) + branch(wg1, bg1, o1, wp1) + branch(wg2, bg2, o2, wp2)
    out_ref[...] = merged.astype(out_ref.dtype)


def gated_merge(h, w_gate, b_gate, branch_outs, branch_projs):
    t, d = h.shape
    tm, tn = min(1024, t), min(512, d)
    nj = d // tn
    b_gate2 = b_gate.reshape(1, -1)
    gate_w = [pl.BlockSpec((d, tn), lambda j, i, br=br: (0, br * nj + j)) for br in range(3)]
    gate_b = [pl.BlockSpec((1, tn), lambda j, i, br=br: (0, br * nj + j)) for br in range(3)]
    o_specs = [pl.BlockSpec((tm, o.shape[1]), lambda j, i: (i, 0)) for o in branch_outs]
    p_specs = [pl.BlockSpec((w.shape[0], tn), lambda j, i: (0, j)) for w in branch_projs]
    return pl.pallas_call(
        _merge_kernel,
        grid=(nj, t // tm),
        in_specs=[pl.BlockSpec((tm, d), lambda j, i: (i, 0))] + gate_w + gate_b + o_specs + p_specs,
        out_specs=pl.BlockSpec((tm, tn), lambda j, i: (i, j)),
        out_shape=jax.ShapeDtypeStruct((t, d), BF16),
        compiler_params=_cparams(("arbitrary", "arbitrary"), 48,
                                 fuse_inputs=[False, True, True, True] + [False] * 6 + [True] * 3),
        name="gated_merge",
    )(h, w_gate, w_gate, w_gate, b_gate2, b_gate2, b_gate2, *branch_outs, *branch_projs)


def _outproj_kernel(m_ref, w_ref, x_ref, gate_ref, g_ref, b_ref, sc_ref, sh_ref, wrh_ref, wrl_ref, br_ref,
                    x1_ref, h2_ref, lg_ref, *, alpha):
    y = _dot(m_ref[...], w_ref[...])
    x1 = _ln(alpha * x_ref[...] + gate_ref[...] * y) * g_ref[...] + b_ref[...]
    x1_ref[...] = x1
    h2 = _ln(x1) * (1.0 + sc_ref[...]) + sh_ref[...]
    h2_ref[...] = h2
    hi = h2.astype(BF16)
    lo = (h2 - hi.astype(F32)).astype(BF16)
    wrh = wrh_ref[...]
    lg_ref[...] = _dot(hi, wrh) + _dot(lo, wrh) + _dot(hi, wrl_ref[...]) + br_ref[...]


def out_projection(merged, w_out, x2d, mod_l, ln_g, ln_b, w_router, b_router, seq, alpha):
    t, d = x2d.shape
    tm = min(256, seq)
    bpb = seq // tm
    n_exp = w_router.shape[1]
    wr = jnp.zeros((d, LANES), F32).at[:, :n_exp].set(w_router)
    wr_hi = wr.astype(BF16)
    wr_lo = (wr - wr_hi.astype(F32)).astype(BF16)
    br = jnp.zeros((1, LANES), F32).at[0, :n_exp].set(b_router)
    row = pl.BlockSpec((tm, d), lambda i: (i, 0))
    vec = pl.BlockSpec((1, d), lambda i: (0, 0))
    wr_spec = pl.BlockSpec((d, LANES), lambda i: (0, 0))
    return pl.pallas_call(
        functools.partial(_outproj_kernel, alpha=alpha),
        grid=(t // tm,),
        in_specs=[row, pl.BlockSpec((d, d), lambda i: (0, 0)), row, _mod_spec(2, d, bpb), vec, vec,
                  _mod_spec(4, d, bpb), _mod_spec(3, d, bpb), wr_spec, wr_spec,
                  pl.BlockSpec((1, LANES), lambda i: (0, 0))],
        out_specs=[row, row, pl.BlockSpec((tm, LANES), lambda i: (i, 0))],
        out_shape=[jax.ShapeDtypeStruct((t, d), F32), jax.ShapeDtypeStruct((t, d), F32),
                   jax.ShapeDtypeStruct((t, LANES), F32)],
        compiler_params=_cparams(("arbitrary",), 48),
        name="out_projection",
    )(merged, w_out, x2d, mod_l, ln_g.reshape(1, d), ln_b.reshape(1, d), mod_l, mod_l, wr_hi, wr_lo, br)


def _route_kernel(lg_ref, tri_ref, ir_ref, w_ref, cnt_ref, carry_ref, *, n_exp):
    @pl.when(pl.program_id(0) == 0)
    def _():
        carry_ref[...] = jnp.zeros_like(carry_ref)

    tr = lg_ref.shape[0]
    lane = lax.broadcasted_iota(jnp.int32, (tr, LANES), 1)
    logits = jnp.where(lane < n_exp, lg_ref[...], -jnp.inf)
    vals, sels, idxs = [], [], []
    for _ in range(TOP_K):
        m = jnp.max(logits, axis=1, keepdims=True)
        idx = jnp.min(jnp.where(logits == m, lane, LANES), axis=1, keepdims=True)
        sel = lane == idx
        vals.append(m)
        idxs.append(idx)
        sels.append(sel)
        logits = jnp.where(sel, -jnp.inf, logits)
    exps = [jnp.exp(v - vals[0]) for v in vals]
    tot = exps[0] + exps[1] + exps[2] + exps[3]
    onehot = jnp.zeros((tr, LANES), F32)
    for sel in sels:
        onehot = onehot + jnp.where(sel, 1.0, 0.0)
    before = _dot(tri_ref[...], onehot.astype(BF16)) + carry_ref[0:1, :]
    ir = jnp.zeros((tr, LANES), jnp.int32)
    wt = jnp.zeros((tr, LANES), F32)
    for k in range(TOP_K):
        rank = jnp.sum(jnp.where(sels[k], before, 0.0), axis=1, keepdims=True).astype(jnp.int32)
        ir = jnp.where(lane == k, idxs[k], ir)
        ir = jnp.where(lane == TOP_K + k, rank, ir)
        wt = jnp.where(lane == k, exps[k] / tot, wt)
    ir_ref[...] = ir
    w_ref[...] = wt
    total = before[-1:, :] + onehot[-1:, :]
    carry_ref[...] = jnp.broadcast_to(total, carry_ref.shape)
    cnt_ref[...] = jnp.broadcast_to(total, cnt_ref.shape)


def route(logits, n_exp):
    t = logits.shape[0]
    tr = min(512, t)
    tri = jnp.asarray((np.arange(tr)[:, None] > np.arange(tr)[None, :]).astype(np.float32), BF16)
    row = pl.BlockSpec((tr, LANES), lambda i: (i, 0))
    return pl.pallas_call(
        functools.partial(_route_kernel, n_exp=n_exp),
        grid=(t // tr,),
        in_specs=[row, pl.BlockSpec((tr, tr), lambda i: (0, 0))],
        out_specs=[row, row, pl.BlockSpec((SUBLANES, LANES), lambda i: (0, 0))],
        out_shape=[jax.ShapeDtypeStruct((t, LANES), jnp.int32), jax.ShapeDtypeStruct((t, LANES), F32),
                   jax.ShapeDtypeStruct((SUBLANES, LANES), F32)],
        scratch_shapes=[pltpu.VMEM((SUBLANES, LANES), F32)],
        compiler_params=_cparams(("arbitrary",), 32),
        name="route",
    )(logits, tri)


def _prep_up_kernel(w_ref, even_ref, odd_ref, g_ref, l_ref):
    w = w_ref[...].astype(BF16)
    g_ref[...] = _dot(w, even_ref[...]).astype(g_ref.dtype)
    l_ref[...] = _dot(w, odd_ref[...]).astype(l_ref.dtype)


def prepare_up_weights(w_up):
    d, f2 = w_up.shape[-2:]
    w3 = w_up.reshape(-1, d, f2)
    n = w3.shape[0]
    tn = min(512, f2)
    half = tn // 2
    even = np.zeros((tn, half), np.float32)
    odd = np.zeros((tn, half), np.float32)
    even[2 * np.arange(half), np.arange(half)] = 1.0
    odd[2 * np.arange(half) + 1, np.arange(half)] = 1.0
    sel = pl.BlockSpec((tn, half), lambda e, j: (0, 0))
    out_spec = pl.BlockSpec((None, d, half), lambda e, j: (e, 0, j))
    out_shape = jax.ShapeDtypeStruct((n, d, f2 // 2), BF16)
    return pl.pallas_call(
        _prep_up_kernel,
        grid=(n, f2 // tn),
        in_specs=[pl.BlockSpec((None, d, tn), lambda e, j: (e, 0, j)), sel, sel],
        out_specs=[out_spec, out_spec],
        out_shape=[out_shape, out_shape],
        compiler_params=_cparams(("arbitrary", "arbitrary"), 32),
        name="prepare_up_weights",
    )(w3, jnp.asarray(even, BF16), jnp.asarray(odd, BF16))


def _expert_kernel(be_ref, tok_cur_ref, tok_next_ref, h_ref, wg_ref, wl_ref, bg_ref, bl_ref, wd_ref, bd_ref,
                   y_ref, xbuf, act_s, sem, *, row_block):
    i = pl.program_id(0)
    n_blocks = pl.num_programs(0)
    slot = i % 2

    def block_wait(dst_slot):
        pltpu.make_async_copy(h_ref.at[pl.ds(0, row_block), :], xbuf.at[dst_slot], sem.at[dst_slot]).wait()

    def start_gather(tok_ref, dst_slot, rows, anchor):
        for r in rows:
            pltpu.make_async_copy(h_ref.at[pl.ds(tok_ref[0, r] + anchor, 1), :],
                                  xbuf.at[dst_slot, pl.ds(r, 1), :], sem.at[dst_slot]).start()

    @pl.when(i == 0)
    def _():
        start_gather(tok_cur_ref, 0, range(row_block), 0)

    block_wait(slot)
    x = xbuf[slot].astype(BF16)
    f = wg_ref.shape[1]
    up_chunk = min(UP_CHUNK, f)
    n_chunks = f // up_chunk
    rows_per_chunk = -(-row_block // n_chunks)
    def anchored_gather(result, rows):
        bits = lax.bitcast_convert_type(result[0:1, 0:1], jnp.int32)[0, 0]
        start_gather(tok_next_ref, 1 - slot, rows, (bits & 0x7FFFFFFF) >> 31)

    for c in range(n_chunks):
        cl = slice(c * up_chunk, (c + 1) * up_chunk)
        lo = min(c * rows_per_chunk, row_block)
        hi = min((c + 1) * rows_per_chunk, row_block)
        mid = (lo + hi) // 2
        glu = jnp.minimum(_dot(x, wg_ref[:, cl]) + bg_ref[:, cl], SWIGLU_LIMIT)
        anchored_gather(glu, range(lo, mid))
        lin = jnp.clip(_dot(x, wl_ref[:, cl]) + bl_ref[:, cl], -SWIGLU_LIMIT, SWIGLU_LIMIT)
        anchored_gather(lin, range(mid, hi))
        act_s[:, cl] = (glu * jax.nn.sigmoid(SWIGLU_ALPHA * glu) * (lin + 1.0)).astype(BF16)
    y_ref[...] = _dot(act_s[...], wd_ref[...]) + bd_ref[...]

    @pl.when(i == n_blocks - 1)
    def _():
        block_wait(1 - slot)


def expert_ffn(h2, row_tok, block_e, w_glu, w_lin, up_offset, b_glu, b_lin, w_down, b_down, row_block):
    d = h2.shape[1]
    n_rows = row_tok.shape[0]
    n_blocks = n_rows // row_block
    n_exp = w_down.shape[0]
    f = w_glu.shape[2]
    weight_buffers = pl.Buffered(2)
    tok3 = row_tok.reshape(n_blocks, 1, row_block)
    grid_spec = pltpu.PrefetchScalarGridSpec(
        num_scalar_prefetch=1,
        grid=(n_blocks,),
        in_specs=[pl.BlockSpec((None, 1, row_block), lambda i, be: (i, 0, 0), memory_space=pltpu.SMEM),
                  pl.BlockSpec((None, 1, row_block), lambda i, be: (jnp.minimum(i + 1, n_blocks - 1), 0, 0),
                               memory_space=pltpu.SMEM),
                  pl.BlockSpec(memory_space=pl.ANY),
                  pl.BlockSpec((None, d, f), lambda i, be: (be[i] + up_offset, 0, 0), pipeline_mode=weight_buffers),
                  pl.BlockSpec((None, d, f), lambda i, be: (be[i] + up_offset, 0, 0), pipeline_mode=weight_buffers),
                  pl.BlockSpec((None, 1, f), lambda i, be: (be[i], 0, 0)),
                  pl.BlockSpec((None, 1, f), lambda i, be: (be[i], 0, 0)),
                  pl.BlockSpec((None, f, d), lambda i, be: (be[i], 0, 0), pipeline_mode=weight_buffers),
                  pl.BlockSpec((None, 1, d), lambda i, be: (be[i], 0, 0))],
        out_specs=pl.BlockSpec((row_block, d), lambda i, be: (i, 0)),
        scratch_shapes=[pltpu.VMEM((2, row_block, d), F32), pltpu.VMEM((row_block, f), BF16),
                        pltpu.SemaphoreType.DMA((2,))])
    return pl.pallas_call(
        functools.partial(_expert_kernel, row_block=row_block),
        grid_spec=grid_spec,
        out_shape=jax.ShapeDtypeStruct((n_rows, d), F32),
        compiler_params=_cparams(("arbitrary",)),
        name="expert_ffn",
    )(block_e, tok3, tok3, h2, w_glu, w_lin, b_glu.reshape(n_exp, 1, f), b_lin.reshape(n_exp, 1, f),
      w_down, b_down.reshape(n_exp, 1, d))


def _combine_kernel(dest_cur_ref, dest_next_ref, ys_ref, w_ref, x_ref, gate_ref, g_ref, b_ref, sc_ref, sh_ref,
                    x2_ref, h_ref, buf, sem, *, tc, alpha):
    i = pl.program_id(0)
    n_tiles = pl.num_programs(0)
    slot = i % 2

    def start_gather(dest_ref, dst_slot):
        for tok in range(tc):
            for k in range(TOP_K):
                pltpu.make_async_copy(ys_ref.at[pl.ds(dest_ref[0, tok * TOP_K + k], 1), :],
                                      buf.at[dst_slot, k, pl.ds(tok, 1), :], sem.at[dst_slot]).start(priority=k % 2)

    def tile_wait(dst_slot):
        for k in range(TOP_K):
            pltpu.make_async_copy(ys_ref.at[pl.ds(0, tc), :], buf.at[dst_slot, k], sem.at[dst_slot]).wait()

    @pl.when(i == 0)
    def _():
        start_gather(dest_cur_ref, 0)

    tile_wait(slot)
    start_gather(dest_next_ref, 1 - slot)
    w = w_ref[...]
    y = w[:, 0:1] * buf[slot, 0]
    for k in range(1, TOP_K):
        y = y + w[:, k:k + 1] * buf[slot, k]
    x2 = _ln(alpha * x_ref[...] + gate_ref[...] * y) * g_ref[...] + b_ref[...]
    x2_ref[...] = x2
    h_ref[...] = (_ln(x2) * (1.0 + sc_ref[...]) + sh_ref[...]).astype(h_ref.dtype)

    @pl.when(i == n_tiles - 1)
    def _():
        tile_wait(1 - slot)


def moe_combine(ys, dest, top_w, x1, mod_l, mod_next, ln_g, ln_b, seq, alpha):
    t, d = x1.shape
    tc = min(128, seq)
    bpb = seq // tc
    row = pl.BlockSpec((tc, d), lambda i: (i, 0))
    vec = pl.BlockSpec((1, d), lambda i: (0, 0))
    n_tiles = t // tc
    dest3 = dest.reshape(n_tiles, 1, tc * TOP_K)
    return pl.pallas_call(
        functools.partial(_combine_kernel, tc=tc, alpha=alpha),
        grid=(n_tiles,),
        in_specs=[pl.BlockSpec((None, 1, tc * TOP_K), lambda i: (i, 0, 0), memory_space=pltpu.SMEM),
                  pl.BlockSpec((None, 1, tc * TOP_K), lambda i: (jnp.minimum(i + 1, n_tiles - 1), 0, 0),
                               memory_space=pltpu.SMEM),
                  pl.BlockSpec(memory_space=pl.ANY),
                  pl.BlockSpec((tc, LANES), lambda i: (i, 0)),
                  row, _mod_spec(5, d, bpb), vec, vec, _mod_spec(1, d, bpb), _mod_spec(0, d, bpb)],
        out_specs=[row, row],
        out_shape=[jax.ShapeDtypeStruct((t, d), F32), jax.ShapeDtypeStruct((t, d), BF16)],
        scratch_shapes=[pltpu.VMEM((2, TOP_K, tc, d), F32), pltpu.SemaphoreType.DMA((2,))],
        compiler_params=_cparams(("arbitrary",), 32),
        name="moe_combine",
    )(dest3, dest3, ys, top_w, x1, mod_l, ln_g.reshape(1, d), ln_b.reshape(1, d), mod_next, mod_next)


def moe_ffn(h2, logits, w_glu, w_lin, up_offset, b_glu, b_lin, w_down, b_down):
    t, d = h2.shape
    n_exp = w_down.shape[0]
    row_block = 256
    idx_rank, top_w, counts = route(logits, n_exp)
    top_idx = idx_rank[:, :TOP_K]
    rank = idx_rank[:, TOP_K:2 * TOP_K]
    counts = counts[0, :n_exp].astype(jnp.int32)
    padded = (counts + row_block - 1) // row_block * row_block
    padded_end = jnp.cumsum(padded)
    padded_start = padded_end - padded
    dest = padded_start[top_idx] + rank
    n_rows = t * TOP_K + n_exp * row_block
    tok_ids = jnp.broadcast_to(jnp.arange(t, dtype=jnp.int32)[:, None], (t, TOP_K))
    row_tok = jnp.zeros((n_rows,), jnp.int32).at[dest.reshape(-1)].set(tok_ids.reshape(-1), unique_indices=True)
    n_blocks = n_rows // row_block
    block_start = jnp.arange(n_blocks, dtype=jnp.int32) * row_block
    block_e = jnp.minimum(jnp.sum(padded_end[None, :] <= block_start[:, None], axis=1), n_exp - 1).astype(jnp.int32)
    ys = expert_ffn(h2, row_tok, block_e, w_glu, w_lin, up_offset, b_glu, b_lin, w_down, b_down, row_block)
    return ys, dest.astype(jnp.int32), top_w


def kernel(x, c, w_ada, b_ada, w_in, b_forget, w_gate, b_gate, w_proj_sb, w_proj_dil, w_proj_fox, w_out,
           ln1_g, ln1_b, w_router, b_router, w_up, b_up, w_down, b_down, ln2_g, ln2_b):
    batch, seq, d = x.shape
    depth = w_ada.shape[0]
    t = batch * seq
    alpha = (2.0 * depth) ** 0.25
    rope_tabs = rope_pair_tables(seq)
    mod = adaln_mod(c, w_ada, b_ada)
    x2d = x.reshape(t, d)
    h = ln_mod(x2d, mod[0], seq, comp_shift=0, comp_scale=1)
    n_exp = w_up.shape[1]
    w_glu, w_lin = prepare_up_weights(w_up)
    for l in range(depth):
        w_in_l = w_in[l]
        qkv = in_projection(h, w_in_l[:, :N_QKV].astype(BF16), rope_tabs, seq)
        q_feat, k_feat = forget_features(h, w_in_l[:, N_QKV:], b_forget[l], batch, seq)
        o_sb = stick_breaking_attention(qkv, batch, seq)
        o_dl = dilated_window_attention(qkv, batch, seq)
        o_fx = forgetting_attention(qkv, q_feat, k_feat, batch, seq)
        merged = gated_merge(h, w_gate[l].astype(BF16), b_gate[l], (o_sb, o_dl, o_fx),
                             (w_proj_sb[l].astype(BF16), w_proj_dil[l].astype(BF16),
                              w_proj_fox[l].astype(BF16)))
        x1, h2, logits = out_projection(merged, w_out[l].astype(BF16), x2d, mod[l], ln1_g[l], ln1_b[l],
                                        w_router[l], b_router[l], seq, alpha)
        ys, dest, top_w = moe_ffn(h2, logits, w_glu, w_lin, l * n_exp,
                                  b_up[l][:, 0::2], b_up[l][:, 1::2], w_down[l].astype(BF16), b_down[l])
        mod_next = mod[min(l + 1, depth - 1)]
        x2d, h = moe_combine(ys, dest, top_w, x1, mod[l], mod_next, ln2_g[l], ln2_b[l], seq, alpha)
    return x2d.reshape(batch, seq, d)
```
